```python
import math
import jax, jax.numpy as jnp
from jax import lax
import numpy as np

D_MODEL = 2048
BATCH = 4
SEQ = 4096
DEPTH = 2

D_FF = 5632
HEAD_DIM = 128
CONV_CH = 1024
CONV_WIDTH = 31
ATT_HEADS = 8
ATT_DIM = ATT_HEADS * HEAD_DIM
DIL_PATTERNS = ((128, 1), (512, 4), (2048, 16))
Q_BLOCK = 128
ROPE_THETA = 10000.0
HGRN_HEADS = 16
HGRN_KDIM = 128
HGRN_VDIM = 128
HGRN_WIDTH = HGRN_HEADS * HGRN_KDIM
CHUNK = 64
EPS = 1e-6
N_EVEN = (DEPTH + 1) // 2
N_ODD = DEPTH // 2
EVEN_IN = 2 * CONV_CH + 3 * ATT_DIM
EVEN_OUT = CONV_CH + ATT_DIM
ODD_IN = 4 * HGRN_WIDTH

kernel_name = "hybrid_conformer_dilated_hgrn2_macaron"

F32 = jnp.float32


def rmsnorm(x, g):
    xf = x.astype(F32)
    y = xf * lax.rsqrt(jnp.mean(xf * xf, axis=-1, keepdims=True) + EPS)
    return (y * g.astype(F32)).astype(x.dtype)


def layernorm(x, g, b):
    xf = x.astype(F32)
    mu = jnp.mean(xf, axis=-1, keepdims=True)
    xc = xf - mu
    var = jnp.mean(xc * xc, axis=-1, keepdims=True)
    return (xc * lax.rsqrt(var + EPS) * g.astype(F32) + b.astype(F32)).astype(x.dtype)


def swiglu(x, wg, wu, wd):
    return (jax.nn.silu(x @ wg) * (x @ wu)) @ wd


def rope(x, pos):
    half = HEAD_DIM // 2
    inv = jnp.exp(-math.log(ROPE_THETA) * jnp.arange(half, dtype=F32) / half)
    ang = pos.astype(F32)[:, None] * inv[None, :]
    cos = jnp.cos(ang)[None, :, None, :]
    sin = jnp.sin(ang)[None, :, None, :]
    xf = x.astype(F32)
    x1, x2 = xf[..., :half], xf[..., half:]
    return jnp.concatenate([x1 * cos - x2 * sin, x2 * cos + x1 * sin], axis=-1).astype(x.dtype)


def dilated_branch(q, k, v, dil, reach):
    b, s, h, hd = q.shape
    L = s // dil
    nb = -(-L // Q_BLOCK)
    Lp = nb * Q_BLOCK

    def to_classes(t):
        return t.reshape(b, L, dil, h, hd).transpose(0, 2, 3, 1, 4)

    qc, kc, vc = to_classes(q), to_classes(k), to_classes(v)
    pad_q = [(0, 0)] * 3 + [(0, Lp - L), (0, 0)]
    pad_kv = [(0, 0)] * 3 + [(Q_BLOCK, Lp - L), (0, 0)]
    qb = jnp.pad(qc, pad_q).reshape(b, dil, h, nb, Q_BLOCK, hd)
    kp = jnp.pad(kc, pad_kv).reshape(b, dil, h, nb + 1, Q_BLOCK, hd)
    vp = jnp.pad(vc, pad_kv).reshape(b, dil, h, nb + 1, Q_BLOCK, hd)
    kb = jnp.concatenate([kp[:, :, :, :-1], kp[:, :, :, 1:]], axis=-2)
    vb = jnp.concatenate([vp[:, :, :, :-1], vp[:, :, :, 1:]], axis=-2)
    scores = jnp.einsum('bchnqd,bchnkd->bchnqk', qb, kb, preferred_element_type=F32)
    qi = jnp.arange(Q_BLOCK)[:, None]
    km = jnp.arange(2 * Q_BLOCK)[None, :]
    dist = Q_BLOCK + qi - km
    band = (dist >= 0) & (dist <= reach)
    blk = jnp.arange(nb)[:, None, None]
    valid = band[None] & ((blk * Q_BLOCK + km[None] - Q_BLOCK) >= 0)
    scores = jnp.where(valid, scores, -jnp.inf)
    mx = jnp.max(scores, axis=-1, keepdims=True)
    p = jnp.exp(scores - mx)
    den = jnp.sum(p, axis=-1, keepdims=True)
    o = jnp.einsum('bchnqk,bchnkd->bchnqd', p, vb.astype(F32)) / den
    lse = (mx + jnp.log(den))[..., 0]
    o = o.reshape(b, dil, h, Lp, hd)[:, :, :, :L].transpose(0, 3, 1, 2, 4).reshape(b, s, h, hd)
    lse = lse.reshape(b, dil, h, Lp)[..., :L].transpose(0, 3, 1, 2).reshape(b, s, h)
    return o, lse


def conv_attn_mixer(hn, pos, w_in, conv_w, conv_b, cn_g, cn_b, qn_g, kn_g, w_out):
    b, s, _ = hn.shape
    u = hn @ w_in
    a_val = u[..., :CONV_CH]
    a_gate = u[..., CONV_CH:2 * CONV_CH]
    o0 = 2 * CONV_CH
    q = u[..., o0:o0 + ATT_DIM].reshape(b, s, ATT_HEADS, HEAD_DIM)
    k = u[..., o0 + ATT_DIM:o0 + 2 * ATT_DIM].reshape(b, s, ATT_HEADS, HEAD_DIM)
    v = u[..., o0 + 2 * ATT_DIM:].reshape(b, s, ATT_HEADS, HEAD_DIM)
    a = a_val * jax.nn.sigmoid(a_gate)
    a = lax.conv_general_dilated(
        a, conv_w[:, None, :].astype(a.dtype), window_strides=(1,),
        padding=[(CONV_WIDTH - 1, 0)], dimension_numbers=('NWC', 'WIO', 'NWC'),
        feature_group_count=CONV_CH) + conv_b
    a = jax.nn.silu(layernorm(a, cn_g, cn_b))
    scale = HEAD_DIM ** -0.5
    q = (rope(rmsnorm(q, qn_g), pos).astype(F32) * scale).astype(hn.dtype)
    k = rope(rmsnorm(k, kn_g), pos)
    outs, lses = [], []
    for window, dil in DIL_PATTERNS:
        o_i, l_i = dilated_branch(q, k, v, dil, window // dil)
        outs.append(o_i)
        lses.append(l_i)
    wts = jax.nn.softmax(jnp.stack(lses, axis=0), axis=0)
    o = jnp.einsum('pbsh,pbshd->bshd', wts, jnp.stack(outs, axis=0))
    o = o.reshape(b, s, ATT_DIM).astype(hn.dtype)
    return jnp.concatenate([a, o], axis=-1) @ w_out


def hgrn2_mixer(hn, w_in, lb, gn_g, w_out):
    b, s, _ = hn.shape
    u = (hn @ w_in).astype(F32)
    qz = u[..., :HGRN_WIDTH]
    fz = u[..., HGRN_WIDTH:2 * HGRN_WIDTH]
    iz = u[..., 2 * HGRN_WIDTH:3 * HGRN_WIDTH]
    gz = u[..., 3 * HGRN_WIDTH:]
    lb = lb.astype(F32)
    q = jax.nn.silu(qz)
    logf = jnp.logaddexp(jnp.log(lb), jnp.log1p(-lb) + jax.nn.log_sigmoid(fz))
    kk = (1.0 - lb) * jax.nn.sigmoid(-fz)
    nc = s // CHUNK

    def heads(t, d):
        return t.reshape(b, nc, CHUNK, HGRN_HEADS, d).transpose(1, 0, 3, 2, 4)

    qh, kh = heads(q, HGRN_KDIM), heads(kk, HGRN_KDIM)
    vh, gh = heads(iz, HGRN_VDIM), heads(logf, HGRN_KDIM)
    bcum = jnp.cumsum(gh, axis=-2)
    causal = jnp.tril(jnp.ones((CHUNK, CHUNK), dtype=bool))

    def step(S, xs):
        qc, kc, vc, bc = xs
        diff = bc[:, :, :, None, :] - bc[:, :, None, :, :]
        decay = jnp.exp(jnp.where(causal[:, :, None], diff, -jnp.inf))
        attn = jnp.einsum('bhtd,bhsd,bhtsd->bhts', qc, kc, decay)
        o = jnp.einsum('bhts,bhsv->bhtv', attn, vc) + \
            jnp.einsum('bhtd,bhdv->bhtv', qc * jnp.exp(bc), S)
        blast = bc[:, :, -1:, :]
        S = jnp.exp(blast[:, :, 0, :])[..., None] * S + \
            jnp.einsum('bhsd,bhsv->bhdv', kc * jnp.exp(blast - bc), vc)
        return S, o

    S0 = jnp.zeros((b, HGRN_HEADS, HGRN_KDIM, HGRN_VDIM), F32)
    _, o = lax.scan(step, S0, (qh, kh, vh, bcum))
    o = o.transpose(1, 0, 3, 2, 4).reshape(b, s, HGRN_HEADS, HGRN_VDIM)
    o = o * lax.rsqrt(jnp.mean(o * o, axis=-1, keepdims=True) + EPS)
    o = o.reshape(b, s, HGRN_WIDTH) * gn_g.astype(F32) * jax.nn.silu(gz)
    return o.astype(hn.dtype) @ w_out


def setup_inputs(seed: int = 0) -> dict:
    key = jax.random.key(seed)
    ks = iter(jax.random.split(key, 32))

    def nrm(shape, fan_in):
        return jax.random.normal(next(ks), shape, F32) * (fan_in ** -0.5)

    def gain(shape):
        return 1.0 + 0.02 * jax.random.normal(next(ks), shape, F32)

    def bias(shape):
        return 0.01 * jax.random.normal(next(ks), shape, F32)

    return {
        "x": jax.random.normal(next(ks), (BATCH, SEQ, D_MODEL), F32),
        "norm_ffn1": gain((DEPTH, D_MODEL)),
        "ffn1_wg": nrm((DEPTH, D_MODEL, D_FF), D_MODEL),
        "ffn1_wu": nrm((DEPTH, D_MODEL, D_FF), D_MODEL),
        "ffn1_wd": nrm((DEPTH, D_FF, D_MODEL), D_FF),
        "norm_mix": gain((DEPTH, D_MODEL)),
        "norm_ffn2": gain((DEPTH, D_MODEL)),
        "ffn2_wg": nrm((DEPTH, D_MODEL, D_FF), D_MODEL),
        "ffn2_wu": nrm((DEPTH, D_MODEL, D_FF), D_MODEL),
        "ffn2_wd": nrm((DEPTH, D_FF, D_MODEL), D_FF),
        "ev_w_in": nrm((N_EVEN, D_MODEL, EVEN_IN), D_MODEL),
        "ev_conv_w": nrm((N_EVEN, CONV_WIDTH, CONV_CH), CONV_WIDTH),
        "ev_conv_b": bias((N_EVEN, CONV_CH)),
        "ev_cn_g": gain((N_EVEN, CONV_CH)),
        "ev_cn_b": bias((N_EVEN, CONV_CH)),
        "ev_qn_g": gain((N_EVEN, HEAD_DIM)),
        "ev_kn_g": gain((N_EVEN, HEAD_DIM)),
        "ev_w_out": nrm((N_EVEN, EVEN_OUT, D_MODEL), EVEN_OUT),
        "od_w_in": nrm((N_ODD, D_MODEL, ODD_IN), D_MODEL),
        "od_lb_logits": jax.random.normal(next(ks), (DEPTH, HGRN_WIDTH), F32),
        "od_gn_g": gain((N_ODD, HGRN_WIDTH)),
        "od_w_out": nrm((N_ODD, HGRN_WIDTH, D_MODEL), HGRN_WIDTH),
    }


def reference(x, norm_ffn1, ffn1_wg, ffn1_wu, ffn1_wd, norm_mix, norm_ffn2, ffn2_wg,
              ffn2_wu, ffn2_wd, ev_w_in, ev_conv_w, ev_conv_b, ev_cn_g, ev_cn_b,
              ev_qn_g, ev_kn_g, ev_w_out, od_w_in, od_lb_logits, od_gn_g, od_w_out):
    pos = jnp.arange(x.shape[1], dtype=jnp.int32)
    p = jax.nn.softmax(od_lb_logits.astype(F32), axis=0)
    lower_bounds = jnp.cumsum(p, axis=0) - p[0:1]
    for l in range(DEPTH):
        j = l // 2
        x = x + 0.5 * swiglu(rmsnorm(x, norm_ffn1[l]), ffn1_wg[l], ffn1_wu[l], ffn1_wd[l])
        hn = rmsnorm(x, norm_mix[l])
        if l % 2 == 0:
            x = x + conv_attn_mixer(hn, pos, ev_w_in[j], ev_conv_w[j], ev_conv_b[j],
                                    ev_cn_g[j], ev_cn_b[j], ev_qn_g[j], ev_kn_g[j], ev_w_out[j])
        else:
            x = x + hgrn2_mixer(hn, od_w_in[j], lower_bounds[l], od_gn_g[j], od_w_out[j])
        x = x + 0.5 * swiglu(rmsnorm(x, norm_ffn2[l]), ffn2_wg[l], ffn2_wu[l], ffn2_wd[l])
    return x
```

```python
import functools
import math

import jax
import jax.numpy as jnp
from jax import lax
from jax.experimental import pallas as pl
from jax.experimental.pallas import tpu as pltpu

F32 = jnp.float32
BF16 = jnp.bfloat16
EPS = 1e-6

LANES = 128
HEAD_DIM = 128
CONV_CH = 1024
CONV_WIDTH = 31
CONV_HALO = 32
ATT_HEADS = 8
ATT_DIM = ATT_HEADS * HEAD_DIM
DIL_PATTERNS = ((128, 1), (512, 4), (2048, 16))
Q_BLOCK = 128
ROPE_THETA = 10000.0
HGRN_HEADS = 16
HGRN_WIDTH = HGRN_HEADS * HEAD_DIM
ML_REP = LANES // ATT_HEADS

VMEM_LIMIT = 56 * 1024 * 1024


def _sigmoid(x):
    return 1.0 / (1.0 + jnp.exp(-x))


def _rms_scale(x):
    return x * lax.rsqrt(jnp.mean(x * x, axis=-1, keepdims=True) + EPS)


def _dot_nt(a, b):
    return lax.dot_general(a, b, (((1,), (1,)), ((), ())), preferred_element_type=F32)


def _ffn_body(x_ref, g_ref, wg_ref, wu_ref, wd_ref, o_ref, xn_ref):
    @pl.when(pl.program_id(1) == 0)
    def _():
        x = x_ref[...]
        xn_ref[...] = (_rms_scale(x) * g_ref[...]).astype(BF16)
        o_ref[...] = x

    xn = xn_ref[...]
    hg = jnp.dot(xn, wg_ref[...], preferred_element_type=F32)
    hu = jnp.dot(xn, wu_ref[...], preferred_element_type=F32)
    h = (hg * _sigmoid(hg)) * hu * 0.5
    o_ref[...] += jnp.dot(h.astype(BF16), wd_ref[...], preferred_element_type=F32)


def _ffn(x2, g, wg, wu, wd, *, tm=512, tf=512):
    m, d = x2.shape
    f = wg.shape[1]
    tm = min(tm, m)
    return pl.pallas_call(
        _ffn_body,
        grid=(m // tm, f // tf),
        in_specs=[
            pl.BlockSpec((tm, d), lambda i, k: (i, 0)),
            pl.BlockSpec((1, d), lambda i, k: (0, 0)),
            pl.BlockSpec((d, tf), lambda i, k: (0, k)),
            pl.BlockSpec((d, tf), lambda i, k: (0, k)),
            pl.BlockSpec((tf, d), lambda i, k: (k, 0)),
        ],
        out_specs=pl.BlockSpec((tm, d), lambda i, k: (i, 0)),
        out_shape=jax.ShapeDtypeStruct((m, d), F32),
        scratch_shapes=[pltpu.VMEM((tm, d), BF16)],
        compiler_params=pltpu.CompilerParams(
            dimension_semantics=("parallel", "arbitrary"), vmem_limit_bytes=VMEM_LIMIT),
        name="ffn",
    )(x2, g, wg, wu, wd)


def _norm_matmul_body(x_ref, g_ref, w_ref, o_ref, xn_ref):
    @pl.when(pl.program_id(1) == 0)
    def _():
        xn_ref[...] = (_rms_scale(x_ref[...]) * g_ref[...]).astype(BF16)

    o_ref[...] = jnp.dot(xn_ref[...], w_ref[...], preferred_element_type=F32).astype(o_ref.dtype)


def _norm_matmul(x2, g, w, out_dtype, *, tm=512, tn=1024):
    m, d = x2.shape
    n = w.shape[1]
    tm = min(tm, m)
    return pl.pallas_call(
        _norm_matmul_body,
        grid=(m // tm, n // tn),
        in_specs=[
            pl.BlockSpec((tm, d), lambda i, k: (i, 0)),
            pl.BlockSpec((1, d), lambda i, k: (0, 0)),
            pl.BlockSpec((d, tn), lambda i, k: (0, k)),
        ],
        out_specs=pl.BlockSpec((tm, tn), lambda i, k: (i, k)),
        out_shape=jax.ShapeDtypeStruct((m, n), out_dtype),
        scratch_shapes=[pltpu.VMEM((tm, d), BF16)],
        compiler_params=pltpu.CompilerParams(
            dimension_semantics=("parallel", "arbitrary"), vmem_limit_bytes=VMEM_LIMIT),
        name="norm_matmul",
    )(x2, g, w)


def _outproj_body(*refs, n_in):
    x_ref, o_ref = refs[0], refs[-1]
    acc = x_ref[...]
    for y_ref, w_ref in zip(refs[1:1 + n_in], refs[1 + n_in:1 + 2 * n_in]):
        acc = acc + jnp.dot(y_ref[...], w_ref[...], preferred_element_type=F32)
    o_ref[...] = acc


def _outproj(x2, ys, ws, *, tm=512):
    m, d = x2.shape
    tm = min(tm, m)
    in_specs = [pl.BlockSpec((tm, d), lambda i: (i, 0))]
    in_specs += [pl.BlockSpec((tm, y.shape[1]), lambda i: (i, 0)) for y in ys]
    in_specs += [pl.BlockSpec(w.shape, lambda i: (0, 0)) for w in ws]
    return pl.pallas_call(
        functools.partial(_outproj_body, n_in=len(ys)),
        grid=(m // tm,),
        in_specs=in_specs,
        out_specs=pl.BlockSpec((tm, d), lambda i: (i, 0)),
        out_shape=jax.ShapeDtypeStruct((m, d), F32),
        compiler_params=pltpu.CompilerParams(
            dimension_semantics=("parallel",), vmem_limit_bytes=VMEM_LIMIT),
        name="outproj",
    )(x2, *ys, *ws)


def _conv_body(val_ref, gate_ref, hval_ref, hgate_ref, w_ref, b_ref, lg_ref, lb_ref,
               o_ref, buf_ref, y_ref, *, ts):
    a = val_ref[0].astype(F32) * _sigmoid(gate_ref[0].astype(F32))
    ah = hval_ref[0].astype(F32) * _sigmoid(hgate_ref[0].astype(F32))
    ah = jnp.where(pl.program_id(1) > 0, ah, 0.0)
    buf_ref[0:CONV_HALO, :] = ah
    buf_ref[CONV_HALO:, :] = a
    first = CONV_HALO - (CONV_WIDTH - 1)
    for cb in range(CONV_CH // LANES):
        cs = slice(cb * LANES, (cb + 1) * LANES)
        acc = jnp.broadcast_to(b_ref[:, cs], (ts, LANES))
        for k in range(CONV_WIDTH):
            acc = acc + w_ref[k:k + 1, cs] * buf_ref[first + k:first + k + ts, cs]
        y_ref[:, cs] = acc
    y = y_ref[...]
    mu = jnp.mean(y, axis=-1, keepdims=True)
    yc = y - mu
    var = jnp.mean(yc * yc, axis=-1, keepdims=True)
    z = yc * lax.rsqrt(var + EPS) * lg_ref[...] + lb_ref[...]
    o_ref[0] = (z * _sigmoid(z)).astype(o_ref.dtype)


def _conv_module(u3, conv_w, conv_b, cn_g, cn_b, *, ts=128):
    b, s, _ = u3.shape
    ts = min(ts, s)
    hb = ts // CONV_HALO

    def halo_idx(col):
        return lambda i, t: (i, jnp.maximum(t * hb - 1, 0), col)

    vec = pl.BlockSpec((1, CONV_CH), lambda i, t: (0, 0))
    return pl.pallas_call(
        functools.partial(_conv_body, ts=ts),
        grid=(b, s // ts),
        in_specs=[
            pl.BlockSpec((1, ts, CONV_CH), lambda i, t: (i, t, 0)),
            pl.BlockSpec((1, ts, CONV_CH), lambda i, t: (i, t, 1)),
            pl.BlockSpec((1, CONV_HALO, CONV_CH), halo_idx(0)),
            pl.BlockSpec((1, CONV_HALO, CONV_CH), halo_idx(1)),
            pl.BlockSpec((CONV_WIDTH, CONV_CH), lambda i, t: (0, 0)),
            vec, vec, vec,
        ],
        out_specs=pl.BlockSpec((1, ts, CONV_CH), lambda i, t: (i, t, 0)),
        out_shape=jax.ShapeDtypeStruct((b, s, CONV_CH), BF16),
        scratch_shapes=[pltpu.VMEM((ts + CONV_HALO, CONV_CH), F32),
                        pltpu.VMEM((ts, CONV_CH), F32)],
        compiler_params=pltpu.CompilerParams(
            dimension_semantics=("parallel", "parallel"), vmem_limit_bytes=VMEM_LIMIT),
        name="conv_module",
    )(u3, u3, u3, u3, conv_w, conv_b, cn_g, cn_b)


def _qkprep_body(q_ref, k_ref, cos_ref, sin_ref, qg_ref, kg_ref, qo_ref, ko_ref):
    cos = cos_ref[...]
    sin = sin_ref[...]
    scale = HEAD_DIM ** -0.5
    for h in range(ATT_HEADS):
        hs = slice(h * HEAD_DIM, (h + 1) * HEAD_DIM)
        for src, g_ref, dst, mult in ((q_ref, qg_ref, qo_ref, scale), (k_ref, kg_ref, ko_ref, None)):
            y = _rms_scale(src[0, :, hs].astype(F32)) * g_ref[...]
            r = y * cos + pltpu.roll(y, HEAD_DIM // 2, 1) * sin
            if mult is not None:
                r = r * mult
            dst[0, :, hs] = r.astype(dst.dtype)


def _qkprep(u3, cos, sin, qn_g, kn_g, *, ts=256):
    b, s, _ = u3.shape
    ts = min(ts, s)
    q_col = 2 * CONV_CH // ATT_DIM
    tab = pl.BlockSpec((ts, HEAD_DIM), lambda i, t: (t, 0))
    vec = pl.BlockSpec((1, HEAD_DIM), lambda i, t: (0, 0))
    out = pl.BlockSpec((1, ts, ATT_DIM), lambda i, t: (i, t, 0))
    return pl.pallas_call(
        _qkprep_body,
        grid=(b, s // ts),
        in_specs=[
            pl.BlockSpec((1, ts, ATT_DIM), lambda i, t: (i, t, q_col)),
            pl.BlockSpec((1, ts, ATT_DIM), lambda i, t: (i, t, q_col + 1)),
            tab, tab, vec, vec,
        ],
        out_specs=[out, out],
        out_shape=[jax.ShapeDtypeStruct((b, s, ATT_DIM), BF16)] * 2,
        compiler_params=pltpu.CompilerParams(
            dimension_semantics=("parallel", "parallel"), vmem_limit_bytes=VMEM_LIMIT),
        name="qkprep",
    )(u3, u3, cos, sin, qn_g, kn_g)


def _attn_body(*refs, first, last):
    q_ref, kp_ref, kc_ref, vp_ref, vc_ref = refs[:5]
    if first:
        outs = refs[5:]
    else:
        acc_in, m_in, l_in = refs[5:8]
        outs = refs[8:]
    n = pl.program_id(2)
    qi = lax.broadcasted_iota(jnp.int32, (Q_BLOCK, Q_BLOCK), 0)
    ki = lax.broadcasted_iota(jnp.int32, (Q_BLOCK, Q_BLOCK), 1)
    mask_prev = (ki >= qi) & (n > 0)
    mask_cur = ki <= qi
    lane_head = lax.broadcasted_iota(jnp.int32, (Q_BLOCK, LANES), 1) // ML_REP
    neg = -jnp.inf
    if not first:
        m_all = m_in[0]
        l_all = l_in[0]
    m_pack = jnp.zeros((Q_BLOCK, LANES), F32)
    l_pack = jnp.zeros((Q_BLOCK, LANES), F32)
    for h in range(ATT_HEADS):
        hs = slice(h * HEAD_DIM, (h + 1) * HEAD_DIM)
        q = q_ref[0, :, hs]
        sp = jnp.where(mask_prev, _dot_nt(q, kp_ref[0, :, hs]), neg)
        sc = jnp.where(mask_cur, _dot_nt(q, kc_ref[0, :, hs]), neg)
        mx = jnp.maximum(jnp.max(sp, axis=-1, keepdims=True), jnp.max(sc, axis=-1, keepdims=True))
        if first:
            m_new = mx
        else:
            sel = lane_head == h
            m_old = jnp.max(jnp.where(sel, m_all, neg), axis=-1, keepdims=True)
            l_old = jnp.max(jnp.where(sel, l_all, 0.0), axis=-1, keepdims=True)
            m_new = jnp.maximum(m_old, mx)
            alpha = jnp.exp(m_old - m_new)
        pp = jnp.exp(sp - m_new)
        pc = jnp.exp(sc - m_new)
        l_new = jnp.sum(pp, axis=-1, keepdims=True) + jnp.sum(pc, axis=-1, keepdims=True)
        acc = (jnp.dot(pp.astype(BF16), vp_ref[0, :, hs], preferred_element_type=F32)
               + jnp.dot(pc.astype(BF16), vc_ref[0, :, hs], preferred_element_type=F32))
        if not first:
            l_new = alpha * l_old + l_new
            acc = alpha * acc_in[0, :, hs] + acc
        if last:
            outs[0][0, :, hs] = (acc / l_new).astype(outs[0].dtype)
        else:
            outs[0][0, :, hs] = acc
            m_pack = jnp.where(lane_head == h, m_new, m_pack)
            l_pack = jnp.where(lane_head == h, l_new, l_pack)
    if not last:
        outs[1][0] = m_pack
        outs[2][0] = l_pack


def _attn_branch(qp, kp, u3, dil, state, *, first, last):
    b, s, _ = qp.shape
    cls_len = s // dil
    nb = cls_len // Q_BLOCK
    u_blocks = u3.shape[2] // ATT_DIM
    v_col = u_blocks - 1

    def cur(i, c, n):
        return (i, n, c)

    def prev(i, c, n):
        return (i, jnp.maximum(n - 1, 0), c)

    wide = (1, Q_BLOCK, ATT_DIM)
    narrow = (1, Q_BLOCK, LANES)
    in_specs = [
        pl.BlockSpec(wide, cur),
        pl.BlockSpec(wide, prev),
        pl.BlockSpec(wide, cur),
        pl.BlockSpec(wide, lambda i, c, n: (i, jnp.maximum(n - 1, 0), c * u_blocks + v_col)),
        pl.BlockSpec(wide, lambda i, c, n: (i, n, c * u_blocks + v_col)),
    ]
    qv = qp.reshape(b, cls_len, dil * ATT_DIM)
    kv = kp.reshape(b, cls_len, dil * ATT_DIM)
    uv = u3.reshape(b, cls_len, dil * u3.shape[2])
    args = [qv, kv, kv, uv, uv]
    if not first:
        acc, m_run, l_run = state
        in_specs += [pl.BlockSpec(wide, cur), pl.BlockSpec(narrow, cur), pl.BlockSpec(narrow, cur)]
        args += [acc.reshape(b, cls_len, dil * ATT_DIM),
                 m_run.reshape(b, cls_len, dil * LANES),
                 l_run.reshape(b, cls_len, dil * LANES)]
    if last:
        out_specs = [pl.BlockSpec(wide, cur)]
        out_shape = [jax.ShapeDtypeStruct((b, cls_len, dil * ATT_DIM), BF16)]
    else:
        out_specs = [pl.BlockSpec(wide, cur), pl.BlockSpec(narrow, cur), pl.BlockSpec(narrow, cur)]
        out_shape = [jax.ShapeDtypeStruct((b, cls_len, dil * ATT_DIM), F32),
                     jax.ShapeDtypeStruct((b, cls_len, dil * LANES), F32),
                     jax.ShapeDtypeStruct((b, cls_len, dil * LANES), F32)]
    outs = pl.pallas_call(
        functools.partial(_attn_body, first=first, last=last),
        grid=(b, dil, nb),
        in_specs=in_specs,
        out_specs=out_specs,
        out_shape=out_shape,
        compiler_params=pltpu.CompilerParams(
            dimension_semantics=("parallel", "parallel", "parallel"), vmem_limit_bytes=VMEM_LIMIT),
        name="dilated_attn",
    )(*args)
    if last:
        return outs[0].reshape(b, s, ATT_DIM)
    return (outs[0].reshape(b, s, ATT_DIM), outs[1].reshape(b, s, LANES), outs[2].reshape(b, s, LANES))


def _hgrn_body(q_ref, f_ref, i_ref, g_ref, lb_ref, gn_ref, o_ref, st_ref, b_ref, *, rows, chunk):
    @pl.when(pl.program_id(2) == 0)
    def _():
        st_ref[...] = jnp.zeros_like(st_ref)

    lb = lb_ref[...]
    log_lb = jnp.log(lb)
    log_1m_lb = jnp.log1p(-lb)
    one_m_lb = 1.0 - lb
    gn = gn_ref[...]
    row = lax.broadcasted_iota(jnp.int32, (chunk, HEAD_DIM), 0)
    ri = lax.broadcasted_iota(jnp.int32, (chunk, chunk), 0)
    ci = lax.broadcasted_iota(jnp.int32, (chunk, chunk), 1)

    def shifted(a, off):
        return a if off == 0 else pltpu.roll(a, (-off) % chunk, 0)

    def one_chunk(c, carry):
        r0 = pl.multiple_of(c * chunk, chunk)
        rs = pl.ds(r0, chunk)
        qz = q_ref[0, rs, :]
        fz = f_ref[0, rs, :]
        v = i_ref[0, rs, :]
        gz = g_ref[0, rs, :]
        q = qz * _sigmoid(qz)
        log_sig = jnp.minimum(fz, 0.0) - jnp.log1p(jnp.exp(-jnp.abs(fz)))
        t1 = jnp.broadcast_to(log_lb, fz.shape)
        t2 = log_1m_lb + log_sig
        logf = jnp.maximum(t1, t2) + jnp.log1p(jnp.exp(-jnp.abs(t1 - t2)))
        kk = one_m_lb * _sigmoid(-fz)
        bcum = logf
        sh = 1
        while sh < chunk:
            bcum = bcum + jnp.where(row >= sh, shifted(bcum, -sh), 0.0)
            sh *= 2
        b_ref[...] = bcum

        att = jnp.zeros((chunk, chunk), F32)
        half = chunk // 2
        while half >= 1:
            blk = 2 * half
            pos = row & (blk - 1)
            is_q = pos >= half
            if blk >= 8:
                parts = [jnp.broadcast_to(b_ref[s0 + half - 1:s0 + half, :], (blk, HEAD_DIM))
                         for s0 in range(0, chunk, blk)]
                b_mid = parts[0] if len(parts) == 1 else jnp.concatenate(parts, axis=0)
            else:
                b_mid = bcum
                for p in range(blk):
                    if half - 1 - p != 0:
                        b_mid = jnp.where(pos == p, shifted(bcum, half - 1 - p), b_mid)
            e = jnp.exp(jnp.where(is_q, bcum - b_mid, b_mid - bcum))
            w = jnp.where(is_q, q, kk) * e
            gq = jnp.where(is_q, w, 0.0).astype(BF16)
            hk = jnp.where(is_q, 0.0, w).astype(BF16)
            sc = _dot_nt(gq, hk)
            if blk < chunk:
                sc = jnp.where((ri // blk) == (ci // blk), sc, 0.0)
            att = att + sc
            half //= 2

        vb = v.astype(BF16)
        st = st_ref[...]
        o = jnp.dot(att.astype(BF16), vb, preferred_element_type=F32)
        o = o + jnp.sum(q * kk, axis=-1, keepdims=True) * v
        o = o + _dot_nt((q * jnp.exp(bcum)).astype(BF16), st.astype(BF16))
        b_last = b_ref[chunk - 1:chunk, :]
        k_dec = (kk * jnp.exp(b_last - bcum)).astype(BF16)
        st_ref[...] = st * jnp.exp(b_last) + jnp.dot(v.T.astype(BF16), k_dec, preferred_element_type=F32)

        on = _rms_scale(o) * gn * (gz * _sigmoid(gz))
        o_ref[0, rs, :] = on.astype(o_ref.dtype)
        return carry

    lax.fori_loop(0, rows // chunk, one_chunk, 0)


def _hgrn(u3, lb, gn_g, *, rows=1024, chunk=128):
    b, s, _ = u3.shape
    rows = min(rows, s)
    nh = HGRN_HEADS

    def col(k):
        return pl.BlockSpec((1, rows, HEAD_DIM), lambda i, h, r: (i, r, k * nh + h))

    vec = pl.BlockSpec((1, HEAD_DIM), lambda i, h, r: (0, h))
    return pl.pallas_call(
        functools.partial(_hgrn_body, rows=rows, chunk=chunk),
        grid=(b, nh, s // rows),
        in_specs=[col(0), col(1), col(2), col(3), vec, vec],
        out_specs=pl.BlockSpec((1, rows, HEAD_DIM), lambda i, h, r: (i, r, h)),
        out_shape=jax.ShapeDtypeStruct((b, s, HGRN_WIDTH), BF16),
        scratch_shapes=[pltpu.VMEM((HEAD_DIM, HEAD_DIM), F32),
                        pltpu.VMEM((chunk, HEAD_DIM), F32)],
        compiler_params=pltpu.CompilerParams(
            dimension_semantics=("parallel", "parallel", "arbitrary"), vmem_limit_bytes=VMEM_LIMIT),
        name="hgrn2",
    )(u3, u3, u3, u3, lb, gn_g)


def _rope_tables(s):
    half = HEAD_DIM // 2
    inv = jnp.exp(-math.log(ROPE_THETA) * jnp.arange(half, dtype=F32) / half)
    ang = jnp.arange(s, dtype=jnp.int32).astype(F32)[:, None] * inv[None, :]
    cos, sin = jnp.cos(ang), jnp.sin(ang)
    return jnp.concatenate([cos, cos], axis=-1), jnp.concatenate([-sin, sin], axis=-1)


def _conv_attn_mixer(x2, b, s, g, w_in, conv_w, conv_b, cn_g, cn_b, qn_g, kn_g, w_out, cos, sin):
    m = b * s
    u = _norm_matmul(x2, g, w_in.astype(BF16), BF16)
    u3 = u.reshape(b, s, u.shape[1])
    a = _conv_module(u3, conv_w, conv_b[None], cn_g[None], cn_b[None])
    qp, kp = _qkprep(u3, cos, sin, qn_g[None], kn_g[None])
    state = None
    for idx, (window, dil) in enumerate(DIL_PATTERNS):
        assert window // dil == Q_BLOCK and (s // dil) % Q_BLOCK == 0
        state = _attn_branch(qp, kp, u3, dil, state, first=idx == 0, last=idx == len(DIL_PATTERNS) - 1)
    w_out = w_out.astype(BF16)
    return _outproj(x2, [a.reshape(m, CONV_CH), state.reshape(m, ATT_DIM)],
                    [w_out[:CONV_CH], w_out[CONV_CH:]])


def _hgrn2_mixer(x2, b, s, g, w_in, lb, gn_g, w_out):
    u = _norm_matmul(x2, g, w_in.astype(BF16), F32)
    og = _hgrn(u.reshape(b, s, u.shape[1]), lb[None], gn_g[None])
    return _outproj(x2, [og.reshape(b * s, HGRN_WIDTH)], [w_out.astype(BF16)])


def kernel(x, norm_ffn1, ffn1_wg, ffn1_wu, ffn1_wd, norm_mix, norm_ffn2, ffn2_wg, ffn2_wu, ffn2_wd, ev_w_in, ev_conv_w, ev_conv_b, ev_cn_g, ev_cn_b, ev_qn_g, ev_kn_g, ev_w_out, od_w_in, od_lb_logits, od_gn_g, od_w_out):
    b, s, d = x.shape
    depth = norm_ffn1.shape[0]
    cos, sin = _rope_tables(s)
    p = jax.nn.softmax(od_lb_logits.astype(F32), axis=0)
    lower_bounds = jnp.cumsum(p, axis=0) - p[0:1]
    x2 = x.reshape(b * s, d)
    for l in range(depth):
        j = l // 2
        x2 = _ffn(x2, norm_ffn1[l][None], ffn1_wg[l].astype(BF16), ffn1_wu[l].astype(BF16),
                  ffn1_wd[l].astype(BF16))
        if l % 2 == 0:
            x2 = _conv_attn_mixer(x2, b, s, norm_mix[l][None], ev_w_in[j], ev_conv_w[j], ev_conv_b[j],
                                  ev_cn_g[j], ev_cn_b[j], ev_qn_g[j], ev_kn_g[j], ev_w_out[j], cos, sin)
        else:
            x2 = _hgrn2_mixer(x2, b, s, norm_mix[l][None], od_w_in[j], lower_bounds[l], od_gn_g[j],
                              od_w_out[j])
        x2 = _ffn(x2, norm_ffn2[l][None], ffn2_wg[l].astype(BF16), ffn2_wu[l].astype(BF16),
                  ffn2_wd[l].astype(BF16))
    return x2.reshape(b, s, d)
```

```python
import functools
import math

import jax
import jax.numpy as jnp
from jax import lax
from jax.experimental import pallas as pl
from jax.experimental.pallas import tpu as pltpu

F32 = jnp.float32
BF16 = jnp.bfloat16
EPS = 1e-6

LANES = 128
SUBLANES = 8
HEAD_DIM = 128
CONV_CH = 1024
CONV_WIDTH = 31
CONV_HALO = 32
ATT_HEADS = 8
ATT_DIM = ATT_HEADS * HEAD_DIM
DIL_PATTERNS = ((128, 1), (512, 4), (2048, 16))
Q_BLOCK = 128
ATT_GROUP = Q_BLOCK * max(d for _, d in DIL_PATTERNS)
ROPE_THETA = 10000.0
HGRN_HEADS = 16
HGRN_WIDTH = HGRN_HEADS * HEAD_DIM
HGRN_HEADS_PER_STEP = 2

VMEM_LIMIT = 58 * 1024 * 1024


def _sigmoid(x):
    return 1.0 / (1.0 + jnp.exp(-x))


def _rms_scale(x):
    return x * lax.rsqrt(jnp.mean(x * x, axis=-1, keepdims=True) + EPS)


def _dot_nt(a, b):
    return lax.dot_general(a, b, (((1,), (1,)), ((), ())), preferred_element_type=F32)


def _ffn_body(x_ref, g_ref, wgu_ref, wd_ref, o_ref, xn_ref, *, tf):
    @pl.when(pl.program_id(1) == 0)
    def _():
        x = x_ref[...]
        xn_ref[...] = (_rms_scale(x) * g_ref[...]).astype(BF16)
        o_ref[...] = x

    hgu = jnp.dot(xn_ref[...], wgu_ref[...], preferred_element_type=F32)
    hg = hgu[:, :tf]
    hu = hgu[:, tf:]
    h = (hg * _sigmoid(hg)) * hu * 0.5
    o_ref[...] += jnp.dot(h.astype(BF16), wd_ref[...], preferred_element_type=F32)


def _ffn(x2, g, wgu, wd, layer, *, tm=1024):
    m, d = x2.shape
    _, nf, _, tf2 = wgu.shape
    tf = tf2 // 2
    tm = min(tm, m)
    return pl.pallas_call(
        functools.partial(_ffn_body, tf=tf),
        grid=(m // tm, nf),
        in_specs=[
            pl.BlockSpec((tm, d), lambda i, k: (i, 0)),
            pl.BlockSpec((None, 1, d), lambda i, k: (layer, 0, 0)),
            pl.BlockSpec((None, None, d, tf2), lambda i, k: (layer, k, 0, 0)),
            pl.BlockSpec((None, tf, d), lambda i, k: (layer, k, 0)),
        ],
        out_specs=pl.BlockSpec((tm, d), lambda i, k: (i, 0)),
        out_shape=jax.ShapeDtypeStruct((m, d), F32),
        scratch_shapes=[pltpu.VMEM((tm, d), BF16)],
        compiler_params=pltpu.CompilerParams(
            dimension_semantics=("parallel", "arbitrary"), vmem_limit_bytes=VMEM_LIMIT),
        name="ffn",
    )(x2, g, wgu, wd)


def _pack_gate_up(wg, wu, tf):
    nl, d, f = wg.shape
    nf = f // tf
    both = jnp.concatenate([wg.reshape(nl, d, nf, tf), wu.reshape(nl, d, nf, tf)], axis=-1)
    return both.astype(BF16).transpose(0, 2, 1, 3)


def _norm_matmul_body(x_ref, g_ref, w_ref, o_ref, xn_ref):
    @pl.when(pl.program_id(1) == 0)
    def _():
        xn_ref[...] = (_rms_scale(x_ref[...]) * g_ref[...]).astype(BF16)

    o_ref[...] = jnp.dot(xn_ref[...], w_ref[...], preferred_element_type=F32).astype(o_ref.dtype)


def _norm_matmul(x2, g, w, layer, wl, col_tiles, out_dtype, *, tm=1024, tn=1024):
    m, d = x2.shape
    tm = min(tm, m)
    tiles = tuple(col_tiles)
    if tiles == tuple(range(tiles[0], tiles[0] + len(tiles))):
        first = tiles[0]
        w_idx = lambda i, k: (wl, 0, k + first)
    else:
        gap_at = next(p for p in range(1, len(tiles)) if tiles[p] != tiles[p - 1] + 1)
        jump = tiles[gap_at] - tiles[gap_at - 1] - 1
        assert tiles == tuple(range(tiles[0], tiles[0] + gap_at)) + tuple(
            range(tiles[gap_at], tiles[gap_at] + len(tiles) - gap_at))
        first = tiles[0]
        w_idx = lambda i, k: (wl, 0, k + first + jnp.where(k >= gap_at, jump, 0))
    return pl.pallas_call(
        _norm_matmul_body,
        grid=(m // tm, len(tiles)),
        in_specs=[
            pl.BlockSpec((tm, d), lambda i, k: (i, 0)),
            pl.BlockSpec((None, 1, d), lambda i, k: (layer, 0, 0)),
            pl.BlockSpec((None, d, tn), w_idx),
        ],
        out_specs=pl.BlockSpec((tm, tn), lambda i, k: (i, k)),
        out_shape=jax.ShapeDtypeStruct((m, tn * len(tiles)), out_dtype),
        scratch_shapes=[pltpu.VMEM((tm, d), BF16)],
        compiler_params=pltpu.CompilerParams(
            dimension_semantics=("parallel", "arbitrary"), vmem_limit_bytes=VMEM_LIMIT),
        name="norm_matmul",
    )(x2, g, w)


def _outproj_body(*refs, n_in):
    x_ref, o_ref = refs[0], refs[-1]
    acc = x_ref[...]
    for y_ref, w_ref in zip(refs[1:1 + n_in], refs[1 + n_in:1 + 2 * n_in]):
        acc = acc + jnp.dot(y_ref[...], w_ref[...], preferred_element_type=F32)
    o_ref[...] = acc


def _outproj(x2, ys, w, wl, *, tm=512):
    m, d = x2.shape
    tm = min(tm, m)
    in_specs = [pl.BlockSpec((tm, d), lambda i: (i, 0))]
    in_specs += [pl.BlockSpec((tm, y.shape[1]), lambda i: (i, 0)) for y in ys]
    row0 = 0
    for y in ys:
        rows = y.shape[1]
        assert row0 % rows == 0
        in_specs.append(pl.BlockSpec((None, rows, d), functools.partial(lambda i, rb: (wl, rb, 0), rb=row0 // rows)))
        row0 += rows
    return pl.pallas_call(
        functools.partial(_outproj_body, n_in=len(ys)),
        grid=(m // tm,),
        in_specs=in_specs,
        out_specs=pl.BlockSpec((tm, d), lambda i: (i, 0)),
        out_shape=jax.ShapeDtypeStruct((m, d), F32),
        compiler_params=pltpu.CompilerParams(
            dimension_semantics=("parallel",), vmem_limit_bytes=VMEM_LIMIT),
        name="outproj",
    )(x2, *ys, *([w] * len(ys)))


def _conv_body(val_ref, gate_ref, hval_ref, hgate_ref, w_ref, b_ref, lg_ref, lb_ref,
               o_ref, buf_ref, sh_ref, y_ref, *, ts):
    a = val_ref[0].astype(F32) * _sigmoid(gate_ref[0].astype(F32))
    ah = hval_ref[0].astype(F32) * _sigmoid(hgate_ref[0].astype(F32))
    ah = jnp.where(pl.program_id(1) > 0, ah, 0.0)
    buf_ref[0:CONV_HALO, :] = ah
    buf_ref[CONV_HALO:, :] = a
    span = ts + CONV_HALO - SUBLANES
    for r in range(1, SUBLANES):
        sh_ref[r - 1, :, :] = buf_ref[r:r + span, :]
    first = CONV_HALO - (CONV_WIDTH - 1)
    rb = min(ts, 128)
    for r0 in range(0, ts, rb):
        for cb in range(CONV_CH // LANES):
            cs = slice(cb * LANES, (cb + 1) * LANES)
            acc = jnp.broadcast_to(b_ref[:, cs], (rb, LANES))
            for k in range(CONV_WIDTH):
                off = first + k
                base = r0 + off - off % SUBLANES
                if off % SUBLANES == 0:
                    tap = buf_ref[base:base + rb, cs]
                else:
                    tap = sh_ref[off % SUBLANES - 1, base:base + rb, cs]
                acc = acc + w_ref[k:k + 1, cs] * tap
            y_ref[r0:r0 + rb, cs] = acc
    y = y_ref[...]
    mu = jnp.mean(y, axis=-1, keepdims=True)
    yc = y - mu
    var = jnp.mean(yc * yc, axis=-1, keepdims=True)
    z = yc * lax.rsqrt(var + EPS) * lg_ref[...] + lb_ref[...]
    o_ref[0] = (z * _sigmoid(z)).astype(o_ref.dtype)


def _conv_module(u3, conv_w, conv_b, cn_g, cn_b, wl, *, ts=256):
    b, s, _ = u3.shape
    ts = min(ts, s)
    hb = ts // CONV_HALO

    def halo_idx(col):
        return lambda i, t: (i, jnp.maximum(t * hb - 1, 0), col)

    vec = pl.BlockSpec((None, 1, CONV_CH), lambda i, t: (wl, 0, 0))
    return pl.pallas_call(
        functools.partial(_conv_body, ts=ts),
        grid=(b, s // ts),
        in_specs=[
            pl.BlockSpec((1, ts, CONV_CH), lambda i, t: (i, t, 0)),
            pl.BlockSpec((1, ts, CONV_CH), lambda i, t: (i, t, 1)),
            pl.BlockSpec((1, CONV_HALO, CONV_CH), halo_idx(0)),
            pl.BlockSpec((1, CONV_HALO, CONV_CH), halo_idx(1)),
            pl.BlockSpec((None, CONV_WIDTH, CONV_CH), lambda i, t: (wl, 0, 0)),
            vec, vec, vec,
        ],
        out_specs=pl.BlockSpec((1, ts, CONV_CH), lambda i, t: (i, t, 0)),
        out_shape=jax.ShapeDtypeStruct((b, s, CONV_CH), BF16),
        scratch_shapes=[pltpu.VMEM((ts + CONV_HALO, CONV_CH), F32),
                        pltpu.VMEM((SUBLANES - 1, ts + CONV_HALO - SUBLANES, CONV_CH), F32),
                        pltpu.VMEM((ts, CONV_CH), F32)],
        compiler_params=pltpu.CompilerParams(
            dimension_semantics=("parallel", "parallel"), vmem_limit_bytes=VMEM_LIMIT),
        name="conv_module",
    )(u3, u3, u3, u3, conv_w, conv_b, cn_g, cn_b)


def _attn_body(q_ref, k_ref, v_ref, cos_ref, sin_ref, qg_ref, kg_ref, o_ref,
               qs_ref, ks_ref, vs_ref, acc_ref, m_ref, l_ref):
    g = pl.program_id(2)
    grp = ATT_GROUP

    @pl.when(g == 0)
    def _():
        ks_ref[0:grp, :] = jnp.zeros((grp, HEAD_DIM), F32)
        vs_ref[0:grp, :] = jnp.zeros((grp, HEAD_DIM), F32)

    @pl.when(g > 0)
    def _():
        ks_ref[0:grp, :] = ks_ref[grp:2 * grp, :]
        vs_ref[0:grp, :] = vs_ref[grp:2 * grp, :]

    scale = HEAD_DIM ** -0.5
    prep_rows = 256

    def prep(i, carry):
        r0 = pl.multiple_of(i * prep_rows, prep_rows)
        rs = pl.ds(r0, prep_rows)
        cur = pl.ds(grp + r0, prep_rows)
        cos = cos_ref[rs, :]
        sin = sin_ref[rs, :]

        def rope(ref, gain_ref):
            y = _rms_scale(ref[0, rs, :].astype(F32)) * gain_ref[...]
            return y * cos + pltpu.roll(y, HEAD_DIM // 2, 1) * sin

        qs_ref[rs, :] = rope(q_ref, qg_ref) * scale
        ks_ref[cur, :] = rope(k_ref, kg_ref)
        vs_ref[cur, :] = v_ref[0, rs, :].astype(F32)
        return carry

    lax.fori_loop(0, grp // prep_rows, prep, 0)

    row = lax.broadcasted_iota(jnp.int32, (Q_BLOCK, 2 * Q_BLOCK), 0)
    col = lax.broadcasted_iota(jnp.int32, (Q_BLOCK, 2 * Q_BLOCK), 1)
    key_rank = jnp.where(col < Q_BLOCK, jnp.where(col >= row, 0, 1), jnp.where(col - Q_BLOCK <= row, -1, 1))

    def branch(dil, first, last):
        blocks_per_class = grp // (Q_BLOCK * dil)

        def unit(u, carry):
            c = u % dil
            nb = u // dil
            q0 = nb * (Q_BLOCK * dil) + c
            k0 = grp + (nb - 1) * (Q_BLOCK * dil) + c
            if dil == 1:
                qsl = pl.ds(pl.multiple_of(q0, Q_BLOCK), Q_BLOCK)
                ksl = pl.ds(pl.multiple_of(k0, Q_BLOCK), 2 * Q_BLOCK)
            else:
                qsl = pl.ds(q0, Q_BLOCK, stride=dil)
                ksl = pl.ds(k0, 2 * Q_BLOCK, stride=dil)
            rank_limit = jnp.where((g > 0) | (nb > 0), 1, 0)
            q = qs_ref[qsl, :].astype(BF16)
            k = ks_ref[ksl, :].astype(BF16)
            v = vs_ref[ksl, :].astype(BF16)
            s = _dot_nt(q, k)
            s = jnp.where(key_rank < rank_limit, s, -jnp.inf)
            mx = jnp.max(s, axis=-1, keepdims=True)
            if first:
                m_new = jnp.broadcast_to(mx, (Q_BLOCK, LANES))
            else:
                m_old = m_ref[qsl, :]
                m_new = jnp.maximum(m_old, mx)
            p = jnp.exp(s - jnp.concatenate([m_new, m_new], axis=1))
            l_new = jnp.broadcast_to(jnp.sum(p, axis=-1, keepdims=True), (Q_BLOCK, LANES))
            acc = jnp.dot(p.astype(BF16), v, preferred_element_type=F32)
            if not first:
                alpha = jnp.exp(m_old - m_new)
                l_new = alpha * l_ref[qsl, :] + l_new
                acc = alpha * acc_ref[qsl, :] + acc
            if last:
                acc_ref[qsl, :] = acc / l_new
            else:
                acc_ref[qsl, :] = acc
                m_ref[qsl, :] = m_new
                l_ref[qsl, :] = l_new
            return carry

        lax.fori_loop(0, dil * blocks_per_class, unit, 0)

    for idx, (window, dil) in enumerate(DIL_PATTERNS):
        assert window // dil == Q_BLOCK and grp % (Q_BLOCK * dil) == 0
        branch(dil, idx == 0, idx == len(DIL_PATTERNS) - 1)
    o_ref[0] = acc_ref[...].astype(o_ref.dtype)


def _fused_attention(u3, cos, sin, qn_g, kn_g, wl=0):
    b, s, width = u3.shape
    grp = ATT_GROUP
    assert s % grp == 0
    q_col = (width - 3 * ATT_DIM) // HEAD_DIM

    def col(slab):
        return pl.BlockSpec((1, grp, HEAD_DIM), lambda i, h, t: (i, t, q_col + slab * ATT_HEADS + h))

    tab = pl.BlockSpec((grp, HEAD_DIM), lambda i, h, t: (t, 0))
    vec = pl.BlockSpec((None, 1, HEAD_DIM), lambda i, h, t: (wl, 0, 0))
    scr = lambda rows: pltpu.VMEM((rows, HEAD_DIM), F32)
    return pl.pallas_call(
        _attn_body,
        grid=(b, ATT_HEADS, s // grp),
        in_specs=[col(0), col(1), col(2), tab, tab, vec, vec],
        out_specs=pl.BlockSpec((1, grp, HEAD_DIM), lambda i, h, t: (i, t, h)),
        out_shape=jax.ShapeDtypeStruct((b, s, ATT_DIM), BF16),
        scratch_shapes=[scr(grp), scr(2 * grp), scr(2 * grp), scr(grp), scr(grp), scr(grp)],
        compiler_params=pltpu.CompilerParams(
            dimension_semantics=("parallel", "parallel", "arbitrary"), vmem_limit_bytes=VMEM_LIMIT),
        name="dilated_attn",
    )(u3, u3, u3, cos, sin, qn_g, kn_g)


def _hgrn_body(qig_q, qig_i, qig_g, f_ref, lb_ref, gn_ref, o_ref, st_ref, b_ref, *, rows, chunk):
    @pl.when(pl.program_id(2) == 0)
    def _():
        st_ref[...] = jnp.zeros_like(st_ref)

    row = lax.broadcasted_iota(jnp.int32, (chunk, HEAD_DIM), 0)
    ri = lax.broadcasted_iota(jnp.int32, (chunk, chunk), 0)
    ci = lax.broadcasted_iota(jnp.int32, (chunk, chunk), 1)
    tril = (ri >= ci).astype(BF16)

    def shifted(a, off):
        return a if off == 0 else pltpu.roll(a, (-off) % chunk, 0)

    def head_chunk(hh, rs):
        hs = slice(hh * HEAD_DIM, (hh + 1) * HEAD_DIM)
        lb = lb_ref[:, hs]
        log_lb = jnp.log(lb)
        log_1m_lb = jnp.log1p(-lb)
        qz = qig_q[0, rs, hs].astype(F32)
        fz = f_ref[0, rs, hs]
        v = qig_i[0, rs, hs]
        gz = qig_g[0, rs, hs].astype(F32)
        q = qz * _sigmoid(qz)
        e = jnp.exp(-jnp.abs(fz))
        r = 1.0 / (1.0 + e)
        log_sig = jnp.minimum(fz, 0.0) + jnp.log(r)
        kk = (1.0 - lb) * jnp.where(fz >= 0.0, e * r, r)
        t1 = jnp.broadcast_to(log_lb, fz.shape)
        t2 = log_1m_lb + log_sig
        logf = jnp.maximum(t1, t2) + jnp.log(1.0 + jnp.exp(-jnp.abs(t1 - t2)))
        hi = logf.astype(BF16)
        rem = logf - hi.astype(F32)
        mid = rem.astype(BF16)
        lo = (rem - mid.astype(F32)).astype(BF16)
        parts = jnp.dot(tril, jnp.concatenate([hi, mid, lo], axis=1), preferred_element_type=F32)
        bcum = parts[:, :HEAD_DIM] + parts[:, HEAD_DIM:2 * HEAD_DIM] + parts[:, 2 * HEAD_DIM:]
        b_ref[hh] = bcum

        att = jnp.zeros((chunk, chunk), F32)
        half = chunk // 2
        while half >= 1:
            blk = 2 * half
            pos = row & (blk - 1)
            is_q = pos >= half
            if blk >= SUBLANES:
                pieces = [jnp.broadcast_to(b_ref[hh, s0 + half - 1:s0 + half, :], (blk, HEAD_DIM))
                          for s0 in range(0, chunk, blk)]
                b_mid = pieces[0] if len(pieces) == 1 else jnp.concatenate(pieces, axis=0)
            else:
                b_mid = bcum
                for p in range(blk):
                    if half - 1 - p != 0:
                        b_mid = jnp.where(pos == p, shifted(bcum, half - 1 - p), b_mid)
            dec = jnp.exp(jnp.where(is_q, bcum - b_mid, b_mid - bcum))
            w = jnp.where(is_q, q, kk) * dec
            gq = jnp.where(is_q, w, 0.0).astype(BF16)
            hk = jnp.where(is_q, 0.0, w).astype(BF16)
            sc = _dot_nt(gq, hk)
            if blk < chunk:
                sc = jnp.where((ri // blk) == (ci // blk), sc, 0.0)
            att = att + sc
            half //= 2

        st = st_ref[hh]
        o = jnp.dot(att.astype(BF16), v, preferred_element_type=F32)
        vf = v.astype(F32)
        o = o + jnp.sum(q * kk, axis=-1, keepdims=True) * vf
        o = o + _dot_nt((q * jnp.exp(bcum)).astype(BF16), st.astype(BF16))
        b_last = b_ref[hh, chunk - 1:chunk, :]
        k_dec = (kk * jnp.exp(b_last - bcum)).astype(BF16)
        st_ref[hh] = st * jnp.exp(b_last) + jnp.dot(vf.T.astype(BF16), k_dec, preferred_element_type=F32)

        on = _rms_scale(o) * gn_ref[:, hs] * (gz * _sigmoid(gz))
        o_ref[0, rs, hs] = on.astype(o_ref.dtype)

    def one_chunk(c, carry):
        rs = pl.ds(pl.multiple_of(c * chunk, chunk), chunk)
        for hh in range(HGRN_HEADS_PER_STEP):
            head_chunk(hh, rs)
        return carry

    lax.fori_loop(0, rows // chunk, one_chunk, 0)


def _hgrn(qig3, f3, lb, gn_g, wl, *, rows=1024, chunk=128):
    b, s, _ = f3.shape
    rows = min(rows, s)
    hp = HGRN_HEADS_PER_STEP
    wide = hp * HEAD_DIM
    ncol = HGRN_WIDTH // wide

    def col(k):
        return pl.BlockSpec((1, rows, wide), lambda i, h, r: (i, r, k * ncol + h))

    return pl.pallas_call(
        functools.partial(_hgrn_body, rows=rows, chunk=chunk),
        grid=(b, ncol, s // rows),
        in_specs=[col(0), col(1), col(2), col(0),
                  pl.BlockSpec((1, wide), lambda i, h, r: (0, h)),
                  pl.BlockSpec((None, 1, wide), lambda i, h, r: (wl, 0, h))],
        out_specs=pl.BlockSpec((1, rows, wide), lambda i, h, r: (i, r, h)),
        out_shape=jax.ShapeDtypeStruct((b, s, HGRN_WIDTH), BF16),
        scratch_shapes=[pltpu.VMEM((hp, HEAD_DIM, HEAD_DIM), F32),
                        pltpu.VMEM((hp, chunk, HEAD_DIM), F32)],
        compiler_params=pltpu.CompilerParams(
            dimension_semantics=("parallel", "parallel", "arbitrary"), vmem_limit_bytes=VMEM_LIMIT),
        name="hgrn2",
    )(qig3, qig3, qig3, f3, lb, gn_g)


def _rope_tables(s):
    half = HEAD_DIM // 2
    inv = jnp.exp(-math.log(ROPE_THETA) * jnp.arange(half, dtype=F32) / half)
    ang = jnp.arange(s, dtype=jnp.int32).astype(F32)[:, None] * inv[None, :]
    cos, sin = jnp.cos(ang), jnp.sin(ang)
    return jnp.concatenate([cos, cos], axis=-1), jnp.concatenate([-sin, sin], axis=-1)


def _conv_attn_mixer(x2, b, s, norm_mix, layer, j, w_in, conv_w, conv_b, cn_g, cn_b, qn_g, kn_g, w_out, cos, sin):
    m = b * s
    n_in = w_in.shape[2]
    u = _norm_matmul(x2, norm_mix, w_in, layer, j, range(n_in // 1024), BF16)
    u3 = u.reshape(b, s, n_in)
    a = _conv_module(u3, conv_w, conv_b, cn_g, cn_b, j)
    o = _fused_attention(u3, cos, sin, qn_g, kn_g, j)
    return _outproj(x2, [a.reshape(m, CONV_CH), o.reshape(m, ATT_DIM)], w_out, j)


def _hgrn2_mixer(x2, b, s, norm_mix, layer, j, w_in, lb, gn_g, w_out):
    per = HGRN_WIDTH // 1024
    qig = _norm_matmul(x2, norm_mix, w_in, layer, j, list(range(per)) + list(range(2 * per, 4 * per)), BF16)
    fz = _norm_matmul(x2, norm_mix, w_in, layer, j, range(per, 2 * per), F32)
    og = _hgrn(qig.reshape(b, s, 3 * HGRN_WIDTH), fz.reshape(b, s, HGRN_WIDTH), lb, gn_g, j)
    return _outproj(x2, [og.reshape(b * s, HGRN_WIDTH)], w_out, j)


def kernel(x, norm_ffn1, ffn1_wg, ffn1_wu, ffn1_wd, norm_mix, norm_ffn2, ffn2_wg, ffn2_wu, ffn2_wd, ev_w_in, ev_conv_w, ev_conv_b, ev_cn_g, ev_cn_b, ev_qn_g, ev_kn_g, ev_w_out, od_w_in, od_lb_logits, od_gn_g, od_w_out):
    b, s, d = x.shape
    depth = norm_ffn1.shape[0]
    cos, sin = _rope_tables(s)
    p = jax.nn.softmax(od_lb_logits.astype(F32), axis=0)
    lower_bounds = jnp.cumsum(p, axis=0) - p[0:1]
    tf = 512
    wgu1, wd1 = _pack_gate_up(ffn1_wg, ffn1_wu, tf), ffn1_wd.astype(BF16)
    wgu2, wd2 = _pack_gate_up(ffn2_wg, ffn2_wu, tf), ffn2_wd.astype(BF16)
    ev_w_in_b, ev_w_out_b = ev_w_in.astype(BF16), ev_w_out.astype(BF16)
    od_w_in_b, od_w_out_b = od_w_in.astype(BF16), od_w_out.astype(BF16)
    row3 = lambda a: a[:, None, :]
    g1, gm, g2 = row3(norm_ffn1), row3(norm_mix), row3(norm_ffn2)
    x2 = x.reshape(b * s, d)
    for l in range(depth):
        j = l // 2
        x2 = _ffn(x2, g1, wgu1, wd1, l)
        if l % 2 == 0:
            x2 = _conv_attn_mixer(x2, b, s, gm, l, j, ev_w_in_b, ev_conv_w, row3(ev_conv_b), row3(ev_cn_g),
                                  row3(ev_cn_b), row3(ev_qn_g), row3(ev_kn_g), ev_w_out_b, cos, sin)
        else:
            x2 = _hgrn2_mixer(x2, b, s, gm, l, j, od_w_in_b, lower_bounds[l][None], row3(od_gn_g), od_w_out_b)
        x2 = _ffn(x2, g2, wgu2, wd2, l)
    return x2.reshape(b, s, d)
```

```python
import functools
import math

import jax
import jax.numpy as jnp
from jax import lax
from jax.experimental import pallas as pl
from jax.experimental.pallas import tpu as pltpu

F32 = jnp.float32
BF16 = jnp.bfloat16
EPS = 1e-6
LOG2E = 1.4426950408889634

LANES = 128
SUBLANES = 8
HEAD_DIM = 128
CONV_CH = 1024
CONV_WIDTH = 31
CONV_HALO = 32
ATT_HEADS = 8
ATT_DIM = ATT_HEADS * HEAD_DIM
DIL_PATTERNS = ((128, 1), (512, 4), (2048, 16))
Q_BLOCK = 128
ATT_GROUP = Q_BLOCK * max(d for _, d in DIL_PATTERNS)
ROPE_THETA = 10000.0
HGRN_HEADS = 16
HGRN_WIDTH = HGRN_HEADS * HEAD_DIM
HGRN_HEADS_PER_STEP = 4
ATT_UNROLL = 4

VMEM_LIMIT = 58 * 1024 * 1024


def _sigmoid(x):
    return 0.5 * jnp.tanh(0.5 * x) + 0.5


def _rms_scale(x):
    return x * lax.rsqrt(jnp.mean(x * x, axis=-1, keepdims=True) + EPS)


def _dot_nt(a, b):
    return lax.dot_general(a, b, (((1,), (1,)), ((), ())), preferred_element_type=F32)


def _ffn_body(x_ref, g_ref, wgu_ref, wd_ref, o_ref, xn_ref, *, tf):
    @pl.when(pl.program_id(1) == 0)
    def _():
        x = x_ref[...]
        xn_ref[...] = (_rms_scale(x) * g_ref[...]).astype(BF16)
        o_ref[...] = x

    hgu = jnp.dot(xn_ref[...], wgu_ref[...], preferred_element_type=F32)
    hg = hgu[:, :tf]
    hu = hgu[:, tf:]
    h = (hg * _sigmoid(hg)) * hu * 0.5
    o_ref[...] += jnp.dot(h.astype(BF16), wd_ref[...], preferred_element_type=F32)


def _ffn(x2, g, wgu, wd, layer, tf, *, tm=1024):
    m, d = x2.shape
    tf2 = 2 * tf
    nf = wgu.shape[2] // tf2
    tm = min(tm, m)
    return pl.pallas_call(
        functools.partial(_ffn_body, tf=tf),
        grid=(m // tm, nf),
        in_specs=[
            pl.BlockSpec((tm, d), lambda i, k: (i, 0)),
            pl.BlockSpec((None, 1, d), lambda i, k: (layer, 0, 0)),
            pl.BlockSpec((None, d, tf2), lambda i, k: (layer, 0, k)),
            pl.BlockSpec((None, tf, d), lambda i, k: (layer, k, 0)),
        ],
        out_specs=pl.BlockSpec((tm, d), lambda i, k: (i, 0)),
        out_shape=jax.ShapeDtypeStruct((m, d), F32),
        scratch_shapes=[pltpu.VMEM((tm, d), BF16)],
        compiler_params=pltpu.CompilerParams(
            dimension_semantics=("parallel", "arbitrary"), vmem_limit_bytes=VMEM_LIMIT),
        name="ffn",
    )(x2, g, wgu, wd)


def _pack_gate_up(wg, wu, tf):
    nl, d, f = wg.shape
    nf = f // tf
    both = jnp.concatenate([wg.reshape(nl, d, nf, tf), wu.reshape(nl, d, nf, tf)], axis=-1)
    return both.astype(BF16).reshape(nl, d, 2 * f)


def _norm_matmul_body(x_ref, g_ref, w_ref, o_ref, xn_ref):
    @pl.when(pl.program_id(1) == 0)
    def _():
        xn_ref[...] = (_rms_scale(x_ref[...]) * g_ref[...]).astype(BF16)

    o_ref[...] = jnp.dot(xn_ref[...], w_ref[...], preferred_element_type=F32).astype(o_ref.dtype)


def _norm_matmul(x2, g, w, layer, wl, col_tiles, out_dtype, *, tm=1024, tn=1024):
    m, d = x2.shape
    tm = min(tm, m)
    tiles = tuple(col_tiles)
    if tiles == tuple(range(tiles[0], tiles[0] + len(tiles))):
        first = tiles[0]
        w_idx = lambda i, k: (wl, 0, k + first)
    else:
        gap_at = next(p for p in range(1, len(tiles)) if tiles[p] != tiles[p - 1] + 1)
        jump = tiles[gap_at] - tiles[gap_at - 1] - 1
        assert tiles == tuple(range(tiles[0], tiles[0] + gap_at)) + tuple(
            range(tiles[gap_at], tiles[gap_at] + len(tiles) - gap_at))
        first = tiles[0]
        w_idx = lambda i, k: (wl, 0, k + first + jnp.where(k >= gap_at, jump, 0))
    return pl.pallas_call(
        _norm_matmul_body,
        grid=(m // tm, len(tiles)),
        in_specs=[
            pl.BlockSpec((tm, d), lambda i, k: (i, 0)),
            pl.BlockSpec((None, 1, d), lambda i, k: (layer, 0, 0)),
            pl.BlockSpec((None, d, tn), w_idx),
        ],
        out_specs=pl.BlockSpec((tm, tn), lambda i, k: (i, k)),
        out_shape=jax.ShapeDtypeStruct((m, tn * len(tiles)), out_dtype),
        scratch_shapes=[pltpu.VMEM((tm, d), BF16)],
        compiler_params=pltpu.CompilerParams(
            dimension_semantics=("parallel", "arbitrary"), vmem_limit_bytes=VMEM_LIMIT),
        name="norm_matmul",
    )(x2, g, w)


def _outproj_body(*refs, n_in):
    x_ref, o_ref = refs[0], refs[-1]
    acc = x_ref[...]
    for y_ref, w_ref in zip(refs[1:1 + n_in], refs[1 + n_in:1 + 2 * n_in]):
        acc = acc + jnp.dot(y_ref[...], w_ref[...], preferred_element_type=F32)
    o_ref[...] = acc


def _outproj(x2, ys, w, wl, *, tm=512):
    m, d = x2.shape
    tm = min(tm, m)
    in_specs = [pl.BlockSpec((tm, d), lambda i: (i, 0))]
    in_specs += [pl.BlockSpec((tm, y.shape[1]), lambda i: (i, 0)) for y in ys]
    row0 = 0
    for y in ys:
        rows = y.shape[1]
        assert row0 % rows == 0
        in_specs.append(pl.BlockSpec((None, rows, d), functools.partial(lambda i, rb: (wl, rb, 0), rb=row0 // rows)))
        row0 += rows
    return pl.pallas_call(
        functools.partial(_outproj_body, n_in=len(ys)),
        grid=(m // tm,),
        in_specs=in_specs,
        out_specs=pl.BlockSpec((tm, d), lambda i: (i, 0)),
        out_shape=jax.ShapeDtypeStruct((m, d), F32),
        compiler_params=pltpu.CompilerParams(
            dimension_semantics=("parallel",), vmem_limit_bytes=VMEM_LIMIT),
        name="outproj",
    )(x2, *ys, *([w] * len(ys)))


def _conv_body(val_ref, gate_ref, hval_ref, hgate_ref, w_ref, b_ref, lg_ref, lb_ref,
               o_ref, buf_ref, sh_ref, y_ref, *, ts):
    a = val_ref[0].astype(F32) * _sigmoid(gate_ref[0].astype(F32))
    ah = hval_ref[0].astype(F32) * _sigmoid(hgate_ref[0].astype(F32))
    ah = jnp.where(pl.program_id(1) > 0, ah, 0.0)
    buf_ref[0:CONV_HALO, :] = ah
    buf_ref[CONV_HALO:, :] = a
    span = ts + CONV_HALO - SUBLANES
    for r in range(1, SUBLANES):
        sh_ref[r - 1, :, :] = buf_ref[r:r + span, :]
    first = CONV_HALO - (CONV_WIDTH - 1)
    rb = min(ts, 128)
    for r0 in range(0, ts, rb):
        for cb in range(CONV_CH // LANES):
            cs = slice(cb * LANES, (cb + 1) * LANES)
            acc = jnp.broadcast_to(b_ref[:, cs], (rb, LANES))
            for k in range(CONV_WIDTH):
                off = first + k
                base = r0 + off - off % SUBLANES
                if off % SUBLANES == 0:
                    tap = buf_ref[base:base + rb, cs]
                else:
                    tap = sh_ref[off % SUBLANES - 1, base:base + rb, cs]
                acc = acc + w_ref[k:k + 1, cs] * tap
            y_ref[r0:r0 + rb, cs] = acc
    y = y_ref[...]
    mu = jnp.mean(y, axis=-1, keepdims=True)
    yc = y - mu
    var = jnp.mean(yc * yc, axis=-1, keepdims=True)
    z = yc * lax.rsqrt(var + EPS) * lg_ref[...] + lb_ref[...]
    o_ref[0] = (z * _sigmoid(z)).astype(o_ref.dtype)


def _conv_module(u3, conv_w, conv_b, cn_g, cn_b, wl, *, ts=256):
    b, s, _ = u3.shape
    ts = min(ts, s)
    hb = ts // CONV_HALO

    def halo_idx(col):
        return lambda i, t: (i, jnp.maximum(t * hb - 1, 0), col)

    vec = pl.BlockSpec((None, 1, CONV_CH), lambda i, t: (wl, 0, 0))
    return pl.pallas_call(
        functools.partial(_conv_body, ts=ts),
        grid=(b, s // ts),
        in_specs=[
            pl.BlockSpec((1, ts, CONV_CH), lambda i, t: (i, t, 0)),
            pl.BlockSpec((1, ts, CONV_CH), lambda i, t: (i, t, 1)),
            pl.BlockSpec((1, CONV_HALO, CONV_CH), halo_idx(0)),
            pl.BlockSpec((1, CONV_HALO, CONV_CH), halo_idx(1)),
            pl.BlockSpec((None, CONV_WIDTH, CONV_CH), lambda i, t: (wl, 0, 0)),
            vec, vec, vec,
        ],
        out_specs=pl.BlockSpec((1, ts, CONV_CH), lambda i, t: (i, t, 0)),
        out_shape=jax.ShapeDtypeStruct((b, s, CONV_CH), BF16),
        scratch_shapes=[pltpu.VMEM((ts + CONV_HALO, CONV_CH), F32),
                        pltpu.VMEM((SUBLANES - 1, ts + CONV_HALO - SUBLANES, CONV_CH), F32),
                        pltpu.VMEM((ts, CONV_CH), F32)],
        compiler_params=pltpu.CompilerParams(
            dimension_semantics=("parallel", "parallel"), vmem_limit_bytes=VMEM_LIMIT),
        name="conv_module",
    )(u3, u3, u3, u3, conv_w, conv_b, cn_g, cn_b)


def _attn_body(q_ref, k_ref, v_ref, cos_ref, sin_ref, qg_ref, kg_ref, o_ref,
               qn_ref, kn_ref, vn_ref, qc_ref, kc_ref, vc_ref,
               acc_n, m_n, l_n, acc_c, m_c, l_c):
    g = pl.program_id(2)
    grp = ATT_GROUP
    assert tuple(d for _, d in DIL_PATTERNS) == (1, 4, 16) and all(w // d == Q_BLOCK for w, d in DIL_PATTERNS)
    ncls = 4
    qcls = grp // ncls
    kcls = 2 * grp // ncls

    @pl.when(g == 0)
    def _():
        kn_ref[0:grp, :] = jnp.zeros((grp, HEAD_DIM), F32)
        vn_ref[0:grp, :] = jnp.zeros((grp, HEAD_DIM), F32)

    @pl.when(g > 0)
    def _():
        kn_ref[0:grp, :] = kn_ref[grp:2 * grp, :]
        vn_ref[0:grp, :] = vn_ref[grp:2 * grp, :]

    scale = HEAD_DIM ** -0.5
    prep_rows = 256

    def prep(i, carry):
        r0 = pl.multiple_of(i * prep_rows, prep_rows)
        rs = pl.ds(r0, prep_rows)
        cur = pl.ds(grp + r0, prep_rows)
        cos = cos_ref[rs, :]
        sin = sin_ref[rs, :]

        def rope(ref, gain_ref):
            y = _rms_scale(ref[0, rs, :].astype(F32)) * gain_ref[...]
            return y * cos + pltpu.roll(y, HEAD_DIM // 2, 1) * sin

        qn_ref[rs, :] = rope(q_ref, qg_ref) * scale
        kn_ref[cur, :] = rope(k_ref, kg_ref)
        vn_ref[cur, :] = v_ref[0, rs, :].astype(F32)
        return carry

    lax.fori_loop(0, grp // prep_rows, prep, 0)

    for c in range(ncls):
        qc_ref[c * qcls:(c + 1) * qcls, :] = qn_ref[pl.ds(c, qcls, stride=ncls), :]
        kc_ref[c * kcls:(c + 1) * kcls, :] = kn_ref[pl.ds(c, kcls, stride=ncls), :]
        vc_ref[c * kcls:(c + 1) * kcls, :] = vn_ref[pl.ds(c, kcls, stride=ncls), :]

    row = lax.broadcasted_iota(jnp.int32, (Q_BLOCK, 2 * Q_BLOCK), 0)
    col = lax.broadcasted_iota(jnp.int32, (Q_BLOCK, 2 * Q_BLOCK), 1)
    key_rank = jnp.where(col < Q_BLOCK, jnp.where(col >= row, 0, 1), jnp.where(col - Q_BLOCK <= row, -1, 1))

    def aligned(start, size):
        return pl.ds(pl.multiple_of(start, Q_BLOCK), size)

    def unit_dil1(u):
        qsl = aligned(u * Q_BLOCK, Q_BLOCK)
        return qsl, aligned(grp + (u - 1) * Q_BLOCK, 2 * Q_BLOCK), None, qsl, u > 0

    def unit_dil4(u):
        c, nb = u % ncls, u // ncls
        qsl = aligned(c * qcls + nb * Q_BLOCK, Q_BLOCK)
        ksl = aligned(c * kcls + grp // ncls + (nb - 1) * Q_BLOCK, 2 * Q_BLOCK)
        return qsl, ksl, pl.ds(nb * (Q_BLOCK * ncls) + c, Q_BLOCK, stride=ncls), qsl, nb > 0

    def unit_dil16(u):
        c, a = u % ncls, u // ncls
        qsl = pl.ds(c * qcls + a, Q_BLOCK, stride=ncls)
        return qsl, pl.ds(c * kcls + a, 2 * Q_BLOCK, stride=ncls), qsl, qsl, False

    def branch(unit, qkv, st_in, st_out, first, last):
        qb, kb, vb = qkv

        def units(it, carry):
            sl = [unit(it * ATT_UNROLL + t) for t in range(ATT_UNROLL)]
            scores = []
            for qsl, ksl, _, _, has_prev in sl:
                s = _dot_nt(qb[qsl, :].astype(BF16), kb[ksl, :].astype(BF16))
                rank_limit = jnp.where((g > 0) | has_prev, 1, 0)
                scores.append(jnp.where(key_rank < rank_limit, s, -jnp.inf))
            probs, m_news, alphas, l_news = [], [], [], []
            for (_, _, isl, _, _), s in zip(sl, scores):
                mx = jnp.max(s, axis=-1, keepdims=True)
                if first:
                    m_new = jnp.broadcast_to(mx, (Q_BLOCK, LANES))
                else:
                    m_old = st_in[1][isl, :]
                    m_new = jnp.maximum(m_old, mx)
                    alphas.append(jnp.exp(m_old - m_new))
                p = jnp.exp(s - jnp.concatenate([m_new, m_new], axis=1))
                l_news.append(jnp.broadcast_to(jnp.sum(p, axis=-1, keepdims=True), (Q_BLOCK, LANES)))
                probs.append(p.astype(BF16))
                m_news.append(m_new)
            accs = [jnp.dot(p, vb[ksl, :].astype(BF16), preferred_element_type=F32)
                    for (_, ksl, _, _, _), p in zip(sl, probs)]
            if not first:
                l_news = [a * st_in[2][isl, :] + ln for (_, _, isl, _, _), a, ln in zip(sl, alphas, l_news)]
                accs = [a * st_in[0][isl, :] + ac for (_, _, isl, _, _), a, ac in zip(sl, alphas, accs)]
            for (_, _, _, osl, _), ac, mn, ln in zip(sl, accs, m_news, l_news):
                if last:
                    st_out[0][osl, :] = ac / ln
                else:
                    st_out[0][osl, :] = ac
                    st_out[1][osl, :] = mn
                    st_out[2][osl, :] = ln
            return carry

        lax.fori_loop(0, grp // Q_BLOCK // ATT_UNROLL, units, 0)

    natural, classes = (acc_n, m_n, l_n), (acc_c, m_c, l_c)
    branch(unit_dil1, (qn_ref, kn_ref, vn_ref), None, natural, True, False)
    branch(unit_dil4, (qc_ref, kc_ref, vc_ref), natural, classes, False, False)
    branch(unit_dil16, (qc_ref, kc_ref, vc_ref), classes, classes, False, True)
    for c in range(ncls):
        acc_n[pl.ds(c, qcls, stride=ncls), :] = acc_c[c * qcls:(c + 1) * qcls, :]
    o_ref[0] = acc_n[...].astype(o_ref.dtype)


def _fused_attention(u3, cos, sin, qn_g, kn_g, wl=0):
    b, s, width = u3.shape
    grp = ATT_GROUP
    assert s % grp == 0
    q_col = (width - 3 * ATT_DIM) // HEAD_DIM

    def col(slab):
        return pl.BlockSpec((1, grp, HEAD_DIM), lambda i, h, t: (i, t, q_col + slab * ATT_HEADS + h))

    tab = pl.BlockSpec((grp, HEAD_DIM), lambda i, h, t: (t, 0))
    vec = pl.BlockSpec((None, 1, HEAD_DIM), lambda i, h, t: (wl, 0, 0))
    scr = lambda rows: pltpu.VMEM((rows, HEAD_DIM), F32)
    return pl.pallas_call(
        _attn_body,
        grid=(b, ATT_HEADS, s // grp),
        in_specs=[col(0), col(1), col(2), tab, tab, vec, vec],
        out_specs=pl.BlockSpec((1, grp, HEAD_DIM), lambda i, h, t: (i, t, h)),
        out_shape=jax.ShapeDtypeStruct((b, s, ATT_DIM), BF16),
        scratch_shapes=[scr(grp), scr(2 * grp), scr(2 * grp)] * 2 + [scr(grp)] * 6,
        compiler_params=pltpu.CompilerParams(
            dimension_semantics=("parallel", "parallel", "arbitrary"), vmem_limit_bytes=VMEM_LIMIT),
        name="dilated_attn",
    )(u3, u3, u3, cos, sin, qn_g, kn_g)


def _hgrn_body(qig_q, qig_i, qig_g, f_ref, lb_ref, gn_ref, o_ref, st_ref, b_ref, cum_ref, mask_ref,
               *, rows, chunk):
    levels = [chunk >> (i + 1) for i in range(chunk.bit_length() - 1)]
    small = [h for h in levels if 2 * h < SUBLANES]

    @pl.when(pl.program_id(2) == 0)
    def _():
        st_ref[...] = jnp.zeros_like(st_ref)

    ri = lax.broadcasted_iota(jnp.int32, (chunk, chunk), 0)
    ci = lax.broadcasted_iota(jnp.int32, (chunk, chunk), 1)
    cum_ref[0:chunk, :] = (ci <= ri).astype(BF16)
    for n, half in enumerate(small):
        mid_row = (ri // (2 * half)) * (2 * half) + (half - 1)
        cum_ref[(n + 1) * chunk:(n + 2) * chunk, :] = (ci <= mid_row).astype(BF16)
    for n, half in enumerate(levels):
        blk = 2 * half
        keep = ((ri // blk) == (ci // blk)) & ((ri & (blk - 1)) >= half) & ((ci & (blk - 1)) < half)
        mask_ref[n] = keep.astype(F32)

    def head_chunk(hh, rs):
        hs = slice(hh * HEAD_DIM, (hh + 1) * HEAD_DIM)
        lb = lb_ref[:, hs]
        log_lb = jnp.log(lb)
        log_1m_lb = jnp.log1p(-lb)
        qz = qig_q[0, rs, hs].astype(F32)
        fz = f_ref[0, rs, hs]
        v = qig_i[0, rs, hs]
        gz = qig_g[0, rs, hs].astype(F32)
        q = qz * _sigmoid(qz)
        sp = jnp.log(1.0 + jnp.exp(-jnp.abs(fz)))
        log_sig = jnp.minimum(fz, 0.0) - sp
        kk = (1.0 - lb) * jnp.exp(-jnp.maximum(fz, 0.0) - sp)
        t1 = jnp.broadcast_to(log_lb, fz.shape)
        t2 = log_1m_lb + log_sig
        logf = jnp.maximum(t1, t2) + jnp.log(1.0 + jnp.exp(-jnp.abs(t1 - t2)))
        hi = logf.astype(BF16)
        rem = logf - hi.astype(F32)
        mid = rem.astype(BF16)
        lo = (rem - mid.astype(F32)).astype(BF16)
        parts = jnp.dot(cum_ref[...], jnp.concatenate([hi, mid, lo], axis=1), preferred_element_type=F32)
        sums = parts[:, :HEAD_DIM] + parts[:, HEAD_DIM:2 * HEAD_DIM] + parts[:, 2 * HEAD_DIM:]
        bcum = sums[0:chunk]
        b_ref[hh] = bcum

        att = jnp.zeros((chunk, chunk), F32)
        for n, half in enumerate(levels):
            blk = 2 * half
            if half in small:
                k = small.index(half) + 1
                b_mid = sums[k * chunk:(k + 1) * chunk]
            else:
                pieces = [jnp.broadcast_to(b_ref[hh, s0 + half - 1:s0 + half, :], (blk, HEAD_DIM))
                          for s0 in range(0, chunk, blk)]
                b_mid = pieces[0] if len(pieces) == 1 else jnp.concatenate(pieces, axis=0)
            dec = jnp.exp2(jnp.abs(bcum - b_mid) * (-LOG2E))
            att = att + _dot_nt((q * dec).astype(BF16), (kk * dec).astype(BF16)) * mask_ref[n]

        st = st_ref[hh]
        o = jnp.dot(att.astype(BF16), v, preferred_element_type=F32)
        vf = v.astype(F32)
        o = o + jnp.sum(q * kk, axis=-1, keepdims=True) * vf
        o = o + _dot_nt((q * jnp.exp(bcum)).astype(BF16), st.astype(BF16))
        b_last = b_ref[hh, chunk - 1:chunk, :]
        k_dec = (kk * jnp.exp(b_last - bcum)).astype(BF16)
        st_ref[hh] = st * jnp.exp(b_last) + jnp.dot(vf.T.astype(BF16), k_dec, preferred_element_type=F32)

        on = _rms_scale(o) * gn_ref[:, hs] * (gz * _sigmoid(gz))
        o_ref[0, rs, hs] = on.astype(o_ref.dtype)

    def one_chunk(c, carry):
        rs = pl.ds(pl.multiple_of(c * chunk, chunk), chunk)
        for hh in range(HGRN_HEADS_PER_STEP):
            head_chunk(hh, rs)
        return carry

    lax.fori_loop(0, rows // chunk, one_chunk, 0)


def _hgrn(qig3, f3, lb, gn_g, wl, *, rows=1024, chunk=128):
    b, s, _ = f3.shape
    rows = min(rows, s)
    hp = HGRN_HEADS_PER_STEP
    wide = hp * HEAD_DIM
    ncol = HGRN_WIDTH // wide
    n_levels = chunk.bit_length() - 1
    n_small = sum(1 for i in range(n_levels) if (chunk >> i) < SUBLANES)

    def col(k):
        return pl.BlockSpec((1, rows, wide), lambda i, h, r: (i, r, k * ncol + h))

    return pl.pallas_call(
        functools.partial(_hgrn_body, rows=rows, chunk=chunk),
        grid=(b, ncol, s // rows),
        in_specs=[col(0), col(1), col(2), col(0),
                  pl.BlockSpec((1, wide), lambda i, h, r: (0, h)),
                  pl.BlockSpec((None, 1, wide), lambda i, h, r: (wl, 0, h))],
        out_specs=pl.BlockSpec((1, rows, wide), lambda i, h, r: (i, r, h)),
        out_shape=jax.ShapeDtypeStruct((b, s, HGRN_WIDTH), BF16),
        scratch_shapes=[pltpu.VMEM((hp, HEAD_DIM, HEAD_DIM), F32),
                        pltpu.VMEM((hp, chunk, HEAD_DIM), F32),
                        pltpu.VMEM((chunk * (1 + n_small), chunk), BF16),
                        pltpu.VMEM((n_levels, chunk, chunk), F32)],
        compiler_params=pltpu.CompilerParams(
            dimension_semantics=("parallel", "parallel", "arbitrary"), vmem_limit_bytes=VMEM_LIMIT),
        name="hgrn2",
    )(qig3, qig3, qig3, f3, lb, gn_g)


def _rope_tables(s):
    half = HEAD_DIM // 2
    inv = jnp.exp(-math.log(ROPE_THETA) * jnp.arange(half, dtype=F32) / half)
    ang = jnp.arange(s, dtype=jnp.int32).astype(F32)[:, None] * inv[None, :]
    cos, sin = jnp.cos(ang), jnp.sin(ang)
    return jnp.concatenate([cos, cos], axis=-1), jnp.concatenate([-sin, sin], axis=-1)


def _conv_attn_mixer(x2, b, s, norm_mix, layer, j, w_in, conv_w, conv_b, cn_g, cn_b, qn_g, kn_g, w_out, cos, sin):
    m = b * s
    n_in = w_in.shape[2]
    u = _norm_matmul(x2, norm_mix, w_in, layer, j, range(n_in // 1024), BF16)
    u3 = u.reshape(b, s, n_in)
    a = _conv_module(u3, conv_w, conv_b, cn_g, cn_b, j)
    o = _fused_attention(u3, cos, sin, qn_g, kn_g, j)
    return _outproj(x2, [a.reshape(m, CONV_CH), o.reshape(m, ATT_DIM)], w_out, j)


def _hgrn2_mixer(x2, b, s, norm_mix, layer, j, w_in, lb, gn_g, w_out):
    per = HGRN_WIDTH // 1024
    qig = _norm_matmul(x2, norm_mix, w_in, layer, j, list(range(per)) + list(range(2 * per, 4 * per)), BF16)
    fz = _norm_matmul(x2, norm_mix, w_in, layer, j, range(per, 2 * per), F32)
    og = _hgrn(qig.reshape(b, s, 3 * HGRN_WIDTH), fz.reshape(b, s, HGRN_WIDTH), lb, gn_g, j)
    return _outproj(x2, [og.reshape(b * s, HGRN_WIDTH)], w_out, j)


def kernel(x, norm_ffn1, ffn1_wg, ffn1_wu, ffn1_wd, norm_mix, norm_ffn2, ffn2_wg, ffn2_wu, ffn2_wd, ev_w_in, ev_conv_w, ev_conv_b, ev_cn_g, ev_cn_b, ev_qn_g, ev_kn_g, ev_w_out, od_w_in, od_lb_logits, od_gn_g, od_w_out):
    b, s, d = x.shape
    depth = norm_ffn1.shape[0]
    cos, sin = _rope_tables(s)
    p = jax.nn.softmax(od_lb_logits.astype(F32), axis=0)
    lower_bounds = jnp.cumsum(p, axis=0) - p[0:1]
    tf = 512
    wgu1, wd1 = _pack_gate_up(ffn1_wg, ffn1_wu, tf), ffn1_wd.astype(BF16)
    wgu2, wd2 = _pack_gate_up(ffn2_wg, ffn2_wu, tf), ffn2_wd.astype(BF16)
    ev_w_in_b, ev_w_out_b = ev_w_in.astype(BF16), ev_w_out.astype(BF16)
    od_w_in_b, od_w_out_b = od_w_in.astype(BF16), od_w_out.astype(BF16)
    row3 = lambda a: a[:, None, :]
    g1, gm, g2 = row3(norm_ffn1), row3(norm_mix), row3(norm_ffn2)
    x2 = x.reshape(b * s, d)
    for l in range(depth):
        j = l // 2
        x2 = _ffn(x2, g1, wgu1, wd1, l, tf)
        if l % 2 == 0:
            x2 = _conv_attn_mixer(x2, b, s, gm, l, j, ev_w_in_b, ev_conv_w, row3(ev_conv_b), row3(ev_cn_g),
                                  row3(ev_cn_b), row3(ev_qn_g), row3(ev_kn_g), ev_w_out_b, cos, sin)
        else:
            x2 = _hgrn2_mixer(x2, b, s, gm, l, j, od_w_in_b, lower_bounds[l][None], row3(od_gn_g), od_w_out_b)
        x2 = _ffn(x2, g2, wgu2, wd2, l, tf)
    return x2.reshape(b, s, d)
```

```python
import functools
import math

import jax
import jax.numpy as jnp
from jax import lax
from jax.experimental import pallas as pl
from jax.experimental.pallas import tpu as pltpu

F32 = jnp.float32
BF16 = jnp.bfloat16
EPS = 1e-6
LOG2E = 1.4426950408889634

LANES = 128
SUBLANES = 8
HEAD_DIM = 128
CONV_CH = 1024
CONV_WIDTH = 31
CONV_HALO = 32
ATT_HEADS = 8
ATT_DIM = ATT_HEADS * HEAD_DIM
DIL_PATTERNS = ((128, 1), (512, 4), (2048, 16))
Q_BLOCK = 128
ATT_GROUP = Q_BLOCK * max(d for _, d in DIL_PATTERNS)
ROPE_THETA = 10000.0
HGRN_HEADS = 16
HGRN_WIDTH = HGRN_HEADS * HEAD_DIM
HGRN_HEADS_PER_STEP = 4
ATT_UNROLL = 8

VMEM_LIMIT = 58 * 1024 * 1024


def _sigmoid(x):
    return 0.5 * jnp.tanh(0.5 * x) + 0.5


def _rms_scale(x):
    return x * lax.rsqrt(jnp.mean(x * x, axis=-1, keepdims=True) + EPS)


def _dot_nt(a, b):
    return lax.dot_general(a, b, (((1,), (1,)), ((), ())), preferred_element_type=F32)


def _ffn_body(x_ref, g_ref, wg_ref, wu_ref, wd_ref, o_ref, xn_ref):
    @pl.when(pl.program_id(1) == 0)
    def _():
        x = x_ref[...]
        xn_ref[...] = (_rms_scale(x) * g_ref[...]).astype(BF16)
        o_ref[...] = x

    xn = xn_ref[...]
    hg = jnp.dot(xn, wg_ref[...], preferred_element_type=F32)
    hu = jnp.dot(xn, wu_ref[...], preferred_element_type=F32)
    h = (hg * _sigmoid(hg)) * hu * 0.5
    o_ref[...] += jnp.dot(h.astype(BF16), wd_ref[...], preferred_element_type=F32)


def _ffn(x2, g, wg, wu, wd, layer, *, tm=1024, tf=512):
    m, d = x2.shape
    f = wg.shape[2]
    tm = min(tm, m)
    return pl.pallas_call(
        _ffn_body,
        grid=(m // tm, f // tf),
        in_specs=[
            pl.BlockSpec((tm, d), lambda i, k: (i, 0)),
            pl.BlockSpec((None, 1, d), lambda i, k: (layer, 0, 0)),
            pl.BlockSpec((None, d, tf), lambda i, k: (layer, 0, k)),
            pl.BlockSpec((None, d, tf), lambda i, k: (layer, 0, k)),
            pl.BlockSpec((None, tf, d), lambda i, k: (layer, k, 0)),
        ],
        out_specs=pl.BlockSpec((tm, d), lambda i, k: (i, 0)),
        out_shape=jax.ShapeDtypeStruct((m, d), F32),
        scratch_shapes=[pltpu.VMEM((tm, d), BF16)],
        compiler_params=pltpu.CompilerParams(
            dimension_semantics=("parallel", "arbitrary"), vmem_limit_bytes=VMEM_LIMIT),
        name="ffn",
    )(x2, g, wg, wu, wd)


def _norm_matmul_body(x_ref, g_ref, w_ref, o_ref, xn_ref):
    @pl.when(pl.program_id(1) == 0)
    def _():
        xn_ref[...] = (_rms_scale(x_ref[...]) * g_ref[...]).astype(BF16)

    o_ref[...] = jnp.dot(xn_ref[...], w_ref[...], preferred_element_type=F32).astype(o_ref.dtype)


def _norm_matmul(x2, g, w, layer, wl, out_dtype, *, tm=1024, tn=1024):
    m, d = x2.shape
    n = w.shape[2]
    tm = min(tm, m)
    return pl.pallas_call(
        _norm_matmul_body,
        grid=(m // tm, n // tn),
        in_specs=[
            pl.BlockSpec((tm, d), lambda i, k: (i, 0)),
            pl.BlockSpec((None, 1, d), lambda i, k: (layer, 0, 0)),
            pl.BlockSpec((None, d, tn), lambda i, k: (wl, 0, k)),
        ],
        out_specs=pl.BlockSpec((tm, tn), lambda i, k: (i, k)),
        out_shape=jax.ShapeDtypeStruct((m, n), out_dtype),
        scratch_shapes=[pltpu.VMEM((tm, d), BF16)],
        compiler_params=pltpu.CompilerParams(
            dimension_semantics=("parallel", "arbitrary"), vmem_limit_bytes=VMEM_LIMIT),
        name="norm_matmul",
    )(x2, g, w)


def _hgrn_inproj_body(x_ref, g_ref, w_ref, lb_ref, q_ref, kk_ref, logf_ref, v_ref, gate_ref, xn_ref, *, per):
    k = pl.program_id(1)

    @pl.when(k == 0)
    def _():
        xn_ref[...] = (_rms_scale(x_ref[...]) * g_ref[...]).astype(BF16)

    def pre():
        return jnp.dot(xn_ref[...], w_ref[...], preferred_element_type=F32)

    @pl.when(k < per)
    def _():
        z = pre()
        q_ref[...] = (z * _sigmoid(z)).astype(q_ref.dtype)

    @pl.when((k >= per) & (k < 2 * per))
    def _():
        fz = pre()
        lb = lb_ref[...]
        sp = jnp.log(1.0 + jnp.exp(-jnp.abs(fz)))
        kk_ref[...] = ((1.0 - lb) * jnp.exp(-jnp.maximum(fz, 0.0) - sp)).astype(kk_ref.dtype)
        t1 = jnp.broadcast_to(jnp.log(lb), fz.shape)
        t2 = jnp.log1p(-lb) + (jnp.minimum(fz, 0.0) - sp)
        logf_ref[...] = jnp.maximum(t1, t2) + jnp.log(1.0 + jnp.exp(-jnp.abs(t1 - t2)))

    @pl.when((k >= 2 * per) & (k < 3 * per))
    def _():
        v_ref[...] = pre().astype(v_ref.dtype)

    @pl.when(k >= 3 * per)
    def _():
        z = pre()
        gate_ref[...] = (z * _sigmoid(z)).astype(gate_ref.dtype)


def _hgrn_inproj(x2, g, w, lb, layer, wl, *, tm=1024, tn=1024):
    m, d = x2.shape
    tm = min(tm, m)
    width = w.shape[2] // 4
    per = width // tn

    def slab(n):
        return pl.BlockSpec((tm, tn), lambda i, k: (i, jnp.clip(k - n * per, 0, per - 1)))

    bf = jax.ShapeDtypeStruct((m, width), BF16)
    return pl.pallas_call(
        functools.partial(_hgrn_inproj_body, per=per),
        grid=(m // tm, 4 * per),
        in_specs=[
            pl.BlockSpec((tm, d), lambda i, k: (i, 0)),
            pl.BlockSpec((None, 1, d), lambda i, k: (layer, 0, 0)),
            pl.BlockSpec((None, d, tn), lambda i, k: (wl, 0, k)),
            pl.BlockSpec((1, tn), lambda i, k: (0, jnp.clip(k - per, 0, per - 1))),
        ],
        out_specs=[slab(0), slab(1), slab(1), slab(2), slab(3)],
        out_shape=[bf, bf, jax.ShapeDtypeStruct((m, width), F32), bf, bf],
        scratch_shapes=[pltpu.VMEM((tm, d), BF16)],
        compiler_params=pltpu.CompilerParams(
            dimension_semantics=("parallel", "arbitrary"), vmem_limit_bytes=VMEM_LIMIT),
        name="hgrn_inproj",
    )(x2, g, w, lb)


def _outproj_body(*refs, n_in):
    x_ref, o_ref = refs[0], refs[-1]
    acc = x_ref[...]
    for y_ref, w_ref in zip(refs[1:1 + n_in], refs[1 + n_in:1 + 2 * n_in]):
        acc = acc + jnp.dot(y_ref[...], w_ref[...], preferred_element_type=F32)
    o_ref[...] = acc


def _outproj(x2, ys, w, wl, *, tm=512):
    m, d = x2.shape
    tm = min(tm, m)
    in_specs = [pl.BlockSpec((tm, d), lambda i: (i, 0))]
    in_specs += [pl.BlockSpec((tm, y.shape[1]), lambda i: (i, 0)) for y in ys]
    row0 = 0
    for y in ys:
        rows = y.shape[1]
        assert row0 % rows == 0
        in_specs.append(pl.BlockSpec((None, rows, d), functools.partial(lambda i, rb: (wl, rb, 0), rb=row0 // rows)))
        row0 += rows
    return pl.pallas_call(
        functools.partial(_outproj_body, n_in=len(ys)),
        grid=(m // tm,),
        in_specs=in_specs,
        out_specs=pl.BlockSpec((tm, d), lambda i: (i, 0)),
        out_shape=jax.ShapeDtypeStruct((m, d), F32),
        compiler_params=pltpu.CompilerParams(
            dimension_semantics=("parallel",), vmem_limit_bytes=VMEM_LIMIT),
        name="outproj",
    )(x2, *ys, *([w] * len(ys)))


def _conv_body(val_ref, gate_ref, hval_ref, hgate_ref, w_ref, b_ref, lg_ref, lb_ref,
               o_ref, buf_ref, sh_ref, y_ref, *, ts):
    a = val_ref[0].astype(F32) * _sigmoid(gate_ref[0].astype(F32))
    ah = hval_ref[0].astype(F32) * _sigmoid(hgate_ref[0].astype(F32))
    ah = jnp.where(pl.program_id(1) > 0, ah, 0.0)
    buf_ref[0:CONV_HALO, :] = ah
    buf_ref[CONV_HALO:, :] = a
    span = ts + CONV_HALO - SUBLANES
    for r in range(1, SUBLANES):
        sh_ref[r - 1, :, :] = buf_ref[r:r + span, :]
    first = CONV_HALO - (CONV_WIDTH - 1)
    rb = min(ts, 128)
    for r0 in range(0, ts, rb):
        for cb in range(CONV_CH // LANES):
            cs = slice(cb * LANES, (cb + 1) * LANES)
            acc = jnp.broadcast_to(b_ref[:, cs], (rb, LANES))
            for k in range(CONV_WIDTH):
                off = first + k
                base = r0 + off - off % SUBLANES
                if off % SUBLANES == 0:
                    tap = buf_ref[base:base + rb, cs]
                else:
                    tap = sh_ref[off % SUBLANES - 1, base:base + rb, cs]
                acc = acc + w_ref[k:k + 1, cs] * tap
            y_ref[r0:r0 + rb, cs] = acc
    y = y_ref[...]
    mu = jnp.mean(y, axis=-1, keepdims=True)
    yc = y - mu
    var = jnp.mean(yc * yc, axis=-1, keepdims=True)
    z = yc * lax.rsqrt(var + EPS) * lg_ref[...] + lb_ref[...]
    o_ref[0] = (z * _sigmoid(z)).astype(o_ref.dtype)


def _conv_module(u3, conv_w, conv_b, cn_g, cn_b, wl, *, ts=256):
    b, s, _ = u3.shape
    ts = min(ts, s)
    hb = ts // CONV_HALO

    def halo_idx(col):
        return lambda i, t: (i, jnp.maximum(t * hb - 1, 0), col)

    vec = pl.BlockSpec((None, 1, CONV_CH), lambda i, t: (wl, 0, 0))
    return pl.pallas_call(
        functools.partial(_conv_body, ts=ts),
        grid=(b, s // ts),
        in_specs=[
            pl.BlockSpec((1, ts, CONV_CH), lambda i, t: (i, t, 0)),
            pl.BlockSpec((1, ts, CONV_CH), lambda i, t: (i, t, 1)),
            pl.BlockSpec((1, CONV_HALO, CONV_CH), halo_idx(0)),
            pl.BlockSpec((1, CONV_HALO, CONV_CH), halo_idx(1)),
            pl.BlockSpec((None, CONV_WIDTH, CONV_CH), lambda i, t: (wl, 0, 0)),
            vec, vec, vec,
        ],
        out_specs=pl.BlockSpec((1, ts, CONV_CH), lambda i, t: (i, t, 0)),
        out_shape=jax.ShapeDtypeStruct((b, s, CONV_CH), BF16),
        scratch_shapes=[pltpu.VMEM((ts + CONV_HALO, CONV_CH), F32),
                        pltpu.VMEM((SUBLANES - 1, ts + CONV_HALO - SUBLANES, CONV_CH), F32),
                        pltpu.VMEM((ts, CONV_CH), F32)],
        compiler_params=pltpu.CompilerParams(
            dimension_semantics=("parallel", "parallel"), vmem_limit_bytes=VMEM_LIMIT),
        name="conv_module",
    )(u3, u3, u3, u3, conv_w, conv_b, cn_g, cn_b)


def _attn_body(q_ref, k_ref, v_ref, cos_ref, sin_ref, qg_ref, kg_ref, o_ref,
               qn_ref, kn_ref, vn_ref, qc_ref, kc_ref, vc_ref,
               acc_n, m_n, l_n, acc_c, m_c, l_c):
    g = pl.program_id(2)
    grp = ATT_GROUP
    assert tuple(d for _, d in DIL_PATTERNS) == (1, 4, 16) and all(w // d == Q_BLOCK for w, d in DIL_PATTERNS)
    ncls = 4
    qcls = grp // ncls
    kcls = 2 * grp // ncls

    @pl.when(g == 0)
    def _():
        kn_ref[0:grp, :] = jnp.zeros((grp, HEAD_DIM), F32)
        vn_ref[0:grp, :] = jnp.zeros((grp, HEAD_DIM), F32)

    @pl.when(g > 0)
    def _():
        kn_ref[0:grp, :] = kn_ref[grp:2 * grp, :]
        vn_ref[0:grp, :] = vn_ref[grp:2 * grp, :]

    scale = HEAD_DIM ** -0.5
    prep_rows = 512

    def prep(i, carry):
        r0 = pl.multiple_of(i * prep_rows, prep_rows)
        rs = pl.ds(r0, prep_rows)
        cur = pl.ds(grp + r0, prep_rows)
        cos = cos_ref[rs, :]
        sin = sin_ref[rs, :]

        def rope(ref, gain_ref):
            y = _rms_scale(ref[0, rs, :].astype(F32)) * gain_ref[...]
            return y * cos + pltpu.roll(y, HEAD_DIM // 2, 1) * sin

        qn_ref[rs, :] = rope(q_ref, qg_ref) * scale
        kn_ref[cur, :] = rope(k_ref, kg_ref)
        vn_ref[cur, :] = v_ref[0, rs, :].astype(F32)
        return carry

    lax.fori_loop(0, grp // prep_rows, prep, 0)

    for c in range(ncls):
        qc_ref[c * qcls:(c + 1) * qcls, :] = qn_ref[pl.ds(c, qcls, stride=ncls), :]
        kc_ref[c * kcls:(c + 1) * kcls, :] = kn_ref[pl.ds(c, kcls, stride=ncls), :]
        vc_ref[c * kcls:(c + 1) * kcls, :] = vn_ref[pl.ds(c, kcls, stride=ncls), :]

    row = lax.broadcasted_iota(jnp.int32, (Q_BLOCK, 2 * Q_BLOCK), 0)
    col = lax.broadcasted_iota(jnp.int32, (Q_BLOCK, 2 * Q_BLOCK), 1)
    key_rank = jnp.where(col < Q_BLOCK, jnp.where(col >= row, 0, 1), jnp.where(col - Q_BLOCK <= row, -1, 1))

    def aligned(start, size):
        return pl.ds(pl.multiple_of(start, Q_BLOCK), size)

    def unit_dil1(u):
        qsl = aligned(u * Q_BLOCK, Q_BLOCK)
        return qsl, aligned(grp + (u - 1) * Q_BLOCK, 2 * Q_BLOCK), None, qsl, u > 0

    def unit_dil4(u):
        c, nb = u % ncls, u // ncls
        qsl = aligned(c * qcls + nb * Q_BLOCK, Q_BLOCK)
        ksl = aligned(c * kcls + grp // ncls + (nb - 1) * Q_BLOCK, 2 * Q_BLOCK)
        return qsl, ksl, pl.ds(nb * (Q_BLOCK * ncls) + c, Q_BLOCK, stride=ncls), qsl, nb > 0

    def unit_dil16(u):
        c, a = u % ncls, u // ncls
        qsl = pl.ds(c * qcls + a, Q_BLOCK, stride=ncls)
        return qsl, pl.ds(c * kcls + a, 2 * Q_BLOCK, stride=ncls), qsl, qsl, False

    def branch(unit, qkv, st_in, st_out, first, last):
        qb, kb, vb = qkv

        def units(it, carry):
            sl = [unit(it * ATT_UNROLL + t) for t in range(ATT_UNROLL)]
            scores = []
            for qsl, ksl, _, _, has_prev in sl:
                s = _dot_nt(qb[qsl, :].astype(BF16), kb[ksl, :].astype(BF16))
                rank_limit = jnp.where((g > 0) | has_prev, 1, 0)
                scores.append(jnp.where(key_rank < rank_limit, s, -jnp.inf))
            probs, m_news, alphas, l_news = [], [], [], []
            for (_, _, isl, _, _), s in zip(sl, scores):
                mx = jnp.max(s, axis=-1, keepdims=True)
                if first:
                    m_new = jnp.broadcast_to(mx, (Q_BLOCK, LANES))
                else:
                    m_old = st_in[1][isl, :]
                    m_new = jnp.maximum(m_old, mx)
                    alphas.append(jnp.exp(m_old - m_new))
                p = jnp.exp(s - jnp.concatenate([m_new, m_new], axis=1))
                l_news.append(jnp.broadcast_to(jnp.sum(p, axis=-1, keepdims=True), (Q_BLOCK, LANES)))
                probs.append(p.astype(BF16))
                m_news.append(m_new)
            accs = [jnp.dot(p, vb[ksl, :].astype(BF16), preferred_element_type=F32)
                    for (_, ksl, _, _, _), p in zip(sl, probs)]
            if not first:
                l_news = [a * st_in[2][isl, :] + ln for (_, _, isl, _, _), a, ln in zip(sl, alphas, l_news)]
                accs = [a * st_in[0][isl, :] + ac for (_, _, isl, _, _), a, ac in zip(sl, alphas, accs)]
            for (_, _, _, osl, _), ac, mn, ln in zip(sl, accs, m_news, l_news):
                if last:
                    st_out[0][osl, :] = ac / ln
                else:
                    st_out[0][osl, :] = ac
                    st_out[1][osl, :] = mn
                    st_out[2][osl, :] = ln
            return carry

        lax.fori_loop(0, grp // Q_BLOCK // ATT_UNROLL, units, 0)

    natural, classes = (acc_n, m_n, l_n), (acc_c, m_c, l_c)
    branch(unit_dil1, (qn_ref, kn_ref, vn_ref), None, natural, True, False)
    branch(unit_dil4, (qc_ref, kc_ref, vc_ref), natural, classes, False, False)
    branch(unit_dil16, (qc_ref, kc_ref, vc_ref), classes, classes, False, True)
    for c in range(ncls):
        acc_n[pl.ds(c, qcls, stride=ncls), :] = acc_c[c * qcls:(c + 1) * qcls, :]
    o_ref[0] = acc_n[...].astype(o_ref.dtype)


def _fused_attention(u3, cos, sin, qn_g, kn_g, wl=0):
    b, s, width = u3.shape
    grp = ATT_GROUP
    assert s % grp == 0
    q_col = (width - 3 * ATT_DIM) // HEAD_DIM

    def col(slab):
        return pl.BlockSpec((1, grp, HEAD_DIM), lambda i, h, t: (i, t, q_col + slab * ATT_HEADS + h))

    tab = pl.BlockSpec((grp, HEAD_DIM), lambda i, h, t: (t, 0))
    vec = pl.BlockSpec((None, 1, HEAD_DIM), lambda i, h, t: (wl, 0, 0))
    scr = lambda rows: pltpu.VMEM((rows, HEAD_DIM), F32)
    return pl.pallas_call(
        _attn_body,
        grid=(b, ATT_HEADS, s // grp),
        in_specs=[col(0), col(1), col(2), tab, tab, vec, vec],
        out_specs=pl.BlockSpec((1, grp, HEAD_DIM), lambda i, h, t: (i, t, h)),
        out_shape=jax.ShapeDtypeStruct((b, s, ATT_DIM), BF16),
        scratch_shapes=[scr(grp), scr(2 * grp), scr(2 * grp)] * 2 + [scr(grp)] * 6,
        compiler_params=pltpu.CompilerParams(
            dimension_semantics=("parallel", "parallel", "arbitrary"), vmem_limit_bytes=VMEM_LIMIT),
        name="dilated_attn",
    )(u3, u3, u3, cos, sin, qn_g, kn_g)


def _hgrn_body(q_ref, kk_ref, logf_ref, v_ref, gate_ref, gn_ref, o_ref, st_ref, b_ref, cum_ref, mask_ref,
               *, rows, chunk):
    levels = [chunk >> (i + 1) for i in range(chunk.bit_length() - 1)]
    small = [h for h in levels if 2 * h < SUBLANES]

    @pl.when(pl.program_id(2) == 0)
    def _():
        st_ref[...] = jnp.zeros_like(st_ref)

    ri = lax.broadcasted_iota(jnp.int32, (chunk, chunk), 0)
    ci = lax.broadcasted_iota(jnp.int32, (chunk, chunk), 1)
    cum_ref[0:chunk, :] = (ci <= ri).astype(BF16)
    for n, half in enumerate(small):
        mid_row = (ri // (2 * half)) * (2 * half) + (half - 1)
        cum_ref[(n + 1) * chunk:(n + 2) * chunk, :] = (ci <= mid_row).astype(BF16)
    for n, half in enumerate(levels):
        blk = 2 * half
        keep = ((ri // blk) == (ci // blk)) & ((ri & (blk - 1)) >= half) & ((ci & (blk - 1)) < half)
        mask_ref[n] = keep.astype(F32)

    def head_chunk(hh, rs):
        hs = slice(hh * HEAD_DIM, (hh + 1) * HEAD_DIM)
        q = q_ref[0, rs, hs].astype(F32)
        kk = kk_ref[0, rs, hs].astype(F32)
        logf = logf_ref[0, rs, hs]
        v = v_ref[0, rs, hs]
        hi = logf.astype(BF16)
        rem = logf - hi.astype(F32)
        mid = rem.astype(BF16)
        lo = (rem - mid.astype(F32)).astype(BF16)
        parts = jnp.dot(cum_ref[...], jnp.concatenate([hi, mid, lo], axis=1), preferred_element_type=F32)
        sums = parts[:, :HEAD_DIM] + parts[:, HEAD_DIM:2 * HEAD_DIM] + parts[:, 2 * HEAD_DIM:]
        bcum = sums[0:chunk]
        b_ref[hh] = bcum

        att = jnp.zeros((chunk, chunk), F32)
        for n, half in enumerate(levels):
            blk = 2 * half
            if half in small:
                k = small.index(half) + 1
                b_mid = sums[k * chunk:(k + 1) * chunk]
            else:
                pieces = [jnp.broadcast_to(b_ref[hh, s0 + half - 1:s0 + half, :], (blk, HEAD_DIM))
                          for s0 in range(0, chunk, blk)]
                b_mid = pieces[0] if len(pieces) == 1 else jnp.concatenate(pieces, axis=0)
            dec = jnp.exp2(jnp.abs(bcum - b_mid) * (-LOG2E))
            att = att + _dot_nt((q * dec).astype(BF16), (kk * dec).astype(BF16)) * mask_ref[n]

        st = st_ref[hh]
        o = jnp.dot(att.astype(BF16), v, preferred_element_type=F32)
        vf = v.astype(F32)
        o = o + jnp.sum(q * kk, axis=-1, keepdims=True) * vf
        o = o + _dot_nt((q * jnp.exp(bcum)).astype(BF16), st.astype(BF16))
        b_last = b_ref[hh, chunk - 1:chunk, :]
        k_dec = (kk * jnp.exp(b_last - bcum)).astype(BF16)
        st_ref[hh] = st * jnp.exp(b_last) + jnp.dot(vf.T.astype(BF16), k_dec, preferred_element_type=F32)

        on = _rms_scale(o) * gn_ref[:, hs] * gate_ref[0, rs, hs].astype(F32)
        o_ref[0, rs, hs] = on.astype(o_ref.dtype)

    def one_chunk(c, carry):
        rs = pl.ds(pl.multiple_of(c * chunk, chunk), chunk)
        for hh in range(HGRN_HEADS_PER_STEP):
            head_chunk(hh, rs)
        return carry

    lax.fori_loop(0, rows // chunk, one_chunk, 0)


def _hgrn(q3, kk3, logf3, v3, gate3, gn_g, wl, *, rows=1024, chunk=256):
    b, s, _ = logf3.shape
    rows = min(rows, s)
    hp = HGRN_HEADS_PER_STEP
    wide = hp * HEAD_DIM
    n_levels = chunk.bit_length() - 1
    n_small = sum(1 for i in range(n_levels) if (chunk >> i) < SUBLANES)
    blk = pl.BlockSpec((1, rows, wide), lambda i, h, r: (i, r, h))
    return pl.pallas_call(
        functools.partial(_hgrn_body, rows=rows, chunk=chunk),
        grid=(b, HGRN_WIDTH // wide, s // rows),
        in_specs=[blk, blk, blk, blk, blk,
                  pl.BlockSpec((None, 1, wide), lambda i, h, r: (wl, 0, h))],
        out_specs=blk,
        out_shape=jax.ShapeDtypeStruct((b, s, HGRN_WIDTH), BF16),
        scratch_shapes=[pltpu.VMEM((hp, HEAD_DIM, HEAD_DIM), F32),
                        pltpu.VMEM((hp, chunk, HEAD_DIM), F32),
                        pltpu.VMEM((chunk * (1 + n_small), chunk), BF16),
                        pltpu.VMEM((n_levels, chunk, chunk), F32)],
        compiler_params=pltpu.CompilerParams(
            dimension_semantics=("parallel", "parallel", "arbitrary"), vmem_limit_bytes=VMEM_LIMIT),
        name="hgrn2",
    )(q3, kk3, logf3, v3, gate3, gn_g)


def _rope_tables(s):
    half = HEAD_DIM // 2
    inv = jnp.exp(-math.log(ROPE_THETA) * jnp.arange(half, dtype=F32) / half)
    ang = jnp.arange(s, dtype=jnp.int32).astype(F32)[:, None] * inv[None, :]
    cos, sin = jnp.cos(ang), jnp.sin(ang)
    return jnp.concatenate([cos, cos], axis=-1), jnp.concatenate([-sin, sin], axis=-1)


def _conv_attn_mixer(x2, b, s, norm_mix, layer, j, w_in, conv_w, conv_b, cn_g, cn_b, qn_g, kn_g, w_out, cos, sin):
    m = b * s
    n_in = w_in.shape[2]
    u = _norm_matmul(x2, norm_mix, w_in, layer, j, BF16)
    u3 = u.reshape(b, s, n_in)
    a = _conv_module(u3, conv_w, conv_b, cn_g, cn_b, j)
    o = _fused_attention(u3, cos, sin, qn_g, kn_g, j)
    return _outproj(x2, [a.reshape(m, CONV_CH), o.reshape(m, ATT_DIM)], w_out, j)


def _hgrn2_mixer(x2, b, s, norm_mix, layer, j, w_in, lb, gn_g, w_out):
    parts = _hgrn_inproj(x2, norm_mix, w_in, lb, layer, j)
    og = _hgrn(*[p.reshape(b, s, HGRN_WIDTH) for p in parts], gn_g, j)
    return _outproj(x2, [og.reshape(b * s, HGRN_WIDTH)], w_out, j)


def kernel(x, norm_ffn1, ffn1_wg, ffn1_wu, ffn1_wd, norm_mix, norm_ffn2, ffn2_wg, ffn2_wu, ffn2_wd, ev_w_in, ev_conv_w, ev_conv_b, ev_cn_g, ev_cn_b, ev_qn_g, ev_kn_g, ev_w_out, od_w_in, od_lb_logits, od_gn_g, od_w_out):
    b, s, d = x.shape
    depth = norm_ffn1.shape[0]
    cos, sin = _rope_tables(s)
    p = jax.nn.softmax(od_lb_logits.astype(F32), axis=0)
    lower_bounds = jnp.cumsum(p, axis=0) - p[0:1]
    ffn1 = [w.astype(BF16) for w in (ffn1_wg, ffn1_wu, ffn1_wd)]
    ffn2 = [w.astype(BF16) for w in (ffn2_wg, ffn2_wu, ffn2_wd)]
    ev_w_in_b, ev_w_out_b = ev_w_in.astype(BF16), ev_w_out.astype(BF16)
    od_w_in_b, od_w_out_b = od_w_in.astype(BF16), od_w_out.astype(BF16)
    row3 = lambda a: a[:, None, :]
    g1, gm, g2 = row3(norm_ffn1), row3(norm_mix), row3(norm_ffn2)
    x2 = x.reshape(b * s, d)
    for l in range(depth):
        j = l // 2
        x2 = _ffn(x2, g1, *ffn1, l)
        if l % 2 == 0:
            x2 = _conv_attn_mixer(x2, b, s, gm, l, j, ev_w_in_b, ev_conv_w, row3(ev_conv_b), row3(ev_cn_g),
                                  row3(ev_cn_b), row3(ev_qn_g), row3(ev_kn_g), ev_w_out_b, cos, sin)
        else:
            x2 = _hgrn2_mixer(x2, b, s, gm, l, j, od_w_in_b, lower_bounds[l][None], row3(od_gn_g), od_w_out_b)
        x2 = _ffn(x2, g2, *ffn2, l)
    return x2.reshape(b, s, d)
```

```python
import functools
import math

import jax
import jax.numpy as jnp
from jax import lax
from jax.experimental import pallas as pl
from jax.experimental.pallas import tpu as pltpu

F32 = jnp.float32
BF16 = jnp.bfloat16
EPS = 1e-6
LOG2E = 1.4426950408889634
F32_TINY = 1.1754944e-38

LANES = 128
SUBLANES = 8
HEAD_DIM = 128
CONV_CH = 1024
CONV_WIDTH = 31
CONV_HALO = 32
ATT_HEADS = 8
ATT_DIM = ATT_HEADS * HEAD_DIM
DIL_PATTERNS = ((128, 1), (512, 4), (2048, 16))
Q_BLOCK = 128
ATT_GROUP = Q_BLOCK * max(d for _, d in DIL_PATTERNS)
ROPE_THETA = 10000.0
HGRN_HEADS = 16
HGRN_WIDTH = HGRN_HEADS * HEAD_DIM
HGRN_HEADS_PER_STEP = 4
ATT_UNROLL = 8

VMEM_LIMIT = 58 * 1024 * 1024


def _sigmoid(x):
    return 0.5 * jnp.tanh(0.5 * x) + 0.5


def _rms_scale(x):
    return x * lax.rsqrt(jnp.mean(x * x, axis=-1, keepdims=True) + EPS)


def _dot_nt(a, b):
    return lax.dot_general(a, b, (((1,), (1,)), ((), ())), preferred_element_type=F32)


def _ffn_body(x_ref, g_ref, wg_ref, wu_ref, wd_ref, o_ref, xn_ref):
    @pl.when(pl.program_id(1) == 0)
    def _():
        x = x_ref[...]
        xn_ref[...] = (_rms_scale(x) * g_ref[...]).astype(BF16)
        o_ref[...] = x

    xn = xn_ref[...]
    hg = jnp.dot(xn, wg_ref[...], preferred_element_type=F32)
    hu = jnp.dot(xn, wu_ref[...], preferred_element_type=F32)
    h = (hg * _sigmoid(hg)) * hu * 0.5
    o_ref[...] += jnp.dot(h.astype(BF16), wd_ref[...], preferred_element_type=F32)


def _ffn(x2, g, wg, wu, wd, layer, *, tm=1024, tf=512):
    m, d = x2.shape
    f = wg.shape[2]
    tm = min(tm, m)
    return pl.pallas_call(
        _ffn_body,
        grid=(m // tm, f // tf),
        in_specs=[
            pl.BlockSpec((tm, d), lambda i, k: (i, 0)),
            pl.BlockSpec((None, 1, d), lambda i, k: (layer, 0, 0)),
            pl.BlockSpec((None, d, tf), lambda i, k: (layer, 0, k)),
            pl.BlockSpec((None, d, tf), lambda i, k: (layer, 0, k)),
            pl.BlockSpec((None, tf, d), lambda i, k: (layer, k, 0)),
        ],
        out_specs=pl.BlockSpec((tm, d), lambda i, k: (i, 0)),
        out_shape=jax.ShapeDtypeStruct((m, d), F32),
        scratch_shapes=[pltpu.VMEM((tm, d), BF16)],
        compiler_params=pltpu.CompilerParams(
            dimension_semantics=("parallel", "arbitrary"), vmem_limit_bytes=VMEM_LIMIT),
        name="ffn",
    )(x2, g, wg, wu, wd)


def _norm_matmul_body(x_ref, g_ref, w_ref, o_ref, xn_ref):
    @pl.when(pl.program_id(1) == 0)
    def _():
        xn_ref[...] = (_rms_scale(x_ref[...]) * g_ref[...]).astype(BF16)

    o_ref[...] = jnp.dot(xn_ref[...], w_ref[...], preferred_element_type=F32).astype(o_ref.dtype)


def _norm_matmul(x2, g, w, layer, wl, out_dtype, *, tm=1024, tn=1024):
    m, d = x2.shape
    n = w.shape[2]
    tm = min(tm, m)
    return pl.pallas_call(
        _norm_matmul_body,
        grid=(m // tm, n // tn),
        in_specs=[
            pl.BlockSpec((tm, d), lambda i, k: (i, 0)),
            pl.BlockSpec((None, 1, d), lambda i, k: (layer, 0, 0)),
            pl.BlockSpec((None, d, tn), lambda i, k: (wl, 0, k)),
        ],
        out_specs=pl.BlockSpec((tm, tn), lambda i, k: (i, k)),
        out_shape=jax.ShapeDtypeStruct((m, n), out_dtype),
        scratch_shapes=[pltpu.VMEM((tm, d), BF16)],
        compiler_params=pltpu.CompilerParams(
            dimension_semantics=("parallel", "arbitrary"), vmem_limit_bytes=VMEM_LIMIT),
        name="norm_matmul",
    )(x2, g, w)


def _hgrn_inproj_body(x_ref, g_ref, w_ref, lb_ref, q_ref, kk_ref, logf_ref, v_ref, gate_ref, xn_ref, *, per):
    k = pl.program_id(1)

    @pl.when(k == 0)
    def _():
        xn_ref[...] = (_rms_scale(x_ref[...]) * g_ref[...]).astype(BF16)

    def pre():
        return jnp.dot(xn_ref[...], w_ref[...], preferred_element_type=F32)

    @pl.when(k < per)
    def _():
        z = pre()
        q_ref[...] = (z * _sigmoid(z)).astype(q_ref.dtype)

    @pl.when((k >= per) & (k < 2 * per))
    def _():
        fz = pre()
        kk = (1.0 - lb_ref[...]) * (0.5 - 0.5 * jnp.tanh(0.5 * fz))
        kk_ref[...] = kk.astype(kk_ref.dtype)
        logf_ref[...] = jnp.log(jnp.maximum(1.0 - kk, F32_TINY))

    @pl.when((k >= 2 * per) & (k < 3 * per))
    def _():
        v_ref[...] = pre().astype(v_ref.dtype)

    @pl.when(k >= 3 * per)
    def _():
        z = pre()
        gate_ref[...] = (z * _sigmoid(z)).astype(gate_ref.dtype)


def _hgrn_inproj(x2, g, w, lb, layer, wl, *, tm=1024, tn=1024):
    m, d = x2.shape
    tm = min(tm, m)
    width = w.shape[2] // 4
    per = width // tn

    def slab(n):
        return pl.BlockSpec((tm, tn), lambda i, k: (i, jnp.clip(k - n * per, 0, per - 1)))

    bf = jax.ShapeDtypeStruct((m, width), BF16)
    return pl.pallas_call(
        functools.partial(_hgrn_inproj_body, per=per),
        grid=(m // tm, 4 * per),
        in_specs=[
            pl.BlockSpec((tm, d), lambda i, k: (i, 0)),
            pl.BlockSpec((None, 1, d), lambda i, k: (layer, 0, 0)),
            pl.BlockSpec((None, d, tn), lambda i, k: (wl, 0, k)),
            pl.BlockSpec((1, tn), lambda i, k: (0, jnp.clip(k - per, 0, per - 1))),
        ],
        out_specs=[slab(0), slab(1), slab(1), slab(2), slab(3)],
        out_shape=[bf, bf, jax.ShapeDtypeStruct((m, width), F32), bf, bf],
        scratch_shapes=[pltpu.VMEM((tm, d), BF16)],
        compiler_params=pltpu.CompilerParams(
            dimension_semantics=("parallel", "arbitrary"), vmem_limit_bytes=VMEM_LIMIT),
        name="hgrn_inproj",
    )(x2, g, w, lb)


def _outproj_body(*refs, n_in):
    x_ref, o_ref = refs[0], refs[-1]
    acc = x_ref[...]
    for y_ref, w_ref in zip(refs[1:1 + n_in], refs[1 + n_in:1 + 2 * n_in]):
        acc = acc + jnp.dot(y_ref[...], w_ref[...], preferred_element_type=F32)
    o_ref[...] = acc


def _outproj(x2, ys, w, wl, *, tm=512):
    m, d = x2.shape
    tm = min(tm, m)
    in_specs = [pl.BlockSpec((tm, d), lambda i: (i, 0))]
    in_specs += [pl.BlockSpec((tm, y.shape[1]), lambda i: (i, 0)) for y in ys]
    row0 = 0
    for y in ys:
        rows = y.shape[1]
        assert row0 % rows == 0
        in_specs.append(pl.BlockSpec((None, rows, d), functools.partial(lambda i, rb: (wl, rb, 0), rb=row0 // rows)))
        row0 += rows
    return pl.pallas_call(
        functools.partial(_outproj_body, n_in=len(ys)),
        grid=(m // tm,),
        in_specs=in_specs,
        out_specs=pl.BlockSpec((tm, d), lambda i: (i, 0)),
        out_shape=jax.ShapeDtypeStruct((m, d), F32),
        compiler_params=pltpu.CompilerParams(
            dimension_semantics=("parallel",), vmem_limit_bytes=VMEM_LIMIT),
        name="outproj",
    )(x2, *ys, *([w] * len(ys)))


def _conv_body(val_ref, gate_ref, hval_ref, hgate_ref, w_ref, b_ref, lg_ref, lb_ref,
               o_ref, buf_ref, sh_ref, y_ref, *, ts):
    a = val_ref[0].astype(F32) * _sigmoid(gate_ref[0].astype(F32))
    ah = hval_ref[0].astype(F32) * _sigmoid(hgate_ref[0].astype(F32))
    ah = jnp.where(pl.program_id(1) > 0, ah, 0.0)
    buf_ref[0:CONV_HALO, :] = ah
    buf_ref[CONV_HALO:, :] = a
    span = ts + CONV_HALO - SUBLANES
    for r in range(1, SUBLANES):
        sh_ref[r - 1, :, :] = buf_ref[r:r + span, :]
    first = CONV_HALO - (CONV_WIDTH - 1)
    rb = min(ts, 128)
    for r0 in range(0, ts, rb):
        for cb in range(CONV_CH // LANES):
            cs = slice(cb * LANES, (cb + 1) * LANES)
            acc = jnp.broadcast_to(b_ref[:, cs], (rb, LANES))
            for k in range(CONV_WIDTH):
                off = first + k
                base = r0 + off - off % SUBLANES
                if off % SUBLANES == 0:
                    tap = buf_ref[base:base + rb, cs]
                else:
                    tap = sh_ref[off % SUBLANES - 1, base:base + rb, cs]
                acc = acc + w_ref[k:k + 1, cs] * tap
            y_ref[r0:r0 + rb, cs] = acc
    y = y_ref[...]
    mu = jnp.mean(y, axis=-1, keepdims=True)
    yc = y - mu
    var = jnp.mean(yc * yc, axis=-1, keepdims=True)
    z = yc * lax.rsqrt(var + EPS) * lg_ref[...] + lb_ref[...]
    o_ref[0] = (z * _sigmoid(z)).astype(o_ref.dtype)


def _conv_module(u3, conv_w, conv_b, cn_g, cn_b, wl, *, ts=256):
    b, s, _ = u3.shape
    ts = min(ts, s)
    hb = ts // CONV_HALO

    def halo_idx(col):
        return lambda i, t: (i, jnp.maximum(t * hb - 1, 0), col)

    vec = pl.BlockSpec((None, 1, CONV_CH), lambda i, t: (wl, 0, 0))
    return pl.pallas_call(
        functools.partial(_conv_body, ts=ts),
        grid=(b, s // ts),
        in_specs=[
            pl.BlockSpec((1, ts, CONV_CH), lambda i, t: (i, t, 0)),
            pl.BlockSpec((1, ts, CONV_CH), lambda i, t: (i, t, 1)),
            pl.BlockSpec((1, CONV_HALO, CONV_CH), halo_idx(0)),
            pl.BlockSpec((1, CONV_HALO, CONV_CH), halo_idx(1)),
            pl.BlockSpec((None, CONV_WIDTH, CONV_CH), lambda i, t: (wl, 0, 0)),
            vec, vec, vec,
        ],
        out_specs=pl.BlockSpec((1, ts, CONV_CH), lambda i, t: (i, t, 0)),
        out_shape=jax.ShapeDtypeStruct((b, s, CONV_CH), BF16),
        scratch_shapes=[pltpu.VMEM((ts + CONV_HALO, CONV_CH), F32),
                        pltpu.VMEM((SUBLANES - 1, ts + CONV_HALO - SUBLANES, CONV_CH), F32),
                        pltpu.VMEM((ts, CONV_CH), F32)],
        compiler_params=pltpu.CompilerParams(
            dimension_semantics=("parallel", "parallel"), vmem_limit_bytes=VMEM_LIMIT),
        name="conv_module",
    )(u3, u3, u3, u3, conv_w, conv_b, cn_g, cn_b)


def _attn_body(q_ref, k_ref, v_ref, cos_ref, sin_ref, qg_ref, kg_ref, o_ref,
               qn_ref, kn_ref, vn_ref, qc_ref, kc_ref, vc_ref,
               acc_n, m_n, l_n, acc_c, m_c, l_c):
    g = pl.program_id(2)
    grp = ATT_GROUP
    assert tuple(d for _, d in DIL_PATTERNS) == (1, 4, 16) and all(w // d == Q_BLOCK for w, d in DIL_PATTERNS)
    ncls = 4
    qcls = grp // ncls
    kcls = 2 * grp // ncls

    @pl.when(g == 0)
    def _():
        kn_ref[0:grp, :] = jnp.zeros((grp, HEAD_DIM), F32)
        vn_ref[0:grp, :] = jnp.zeros((grp, HEAD_DIM), F32)

    @pl.when(g > 0)
    def _():
        kn_ref[0:grp, :] = kn_ref[grp:2 * grp, :]
        vn_ref[0:grp, :] = vn_ref[grp:2 * grp, :]

    scale = HEAD_DIM ** -0.5
    prep_rows = 512

    def prep(i, carry):
        r0 = pl.multiple_of(i * prep_rows, prep_rows)
        rs = pl.ds(r0, prep_rows)
        cur = pl.ds(grp + r0, prep_rows)
        cos = cos_ref[rs, :]
        sin = sin_ref[rs, :]

        def rope(ref, gain_ref):
            y = _rms_scale(ref[0, rs, :].astype(F32)) * gain_ref[...]
            return y * cos + pltpu.roll(y, HEAD_DIM // 2, 1) * sin

        qn_ref[rs, :] = rope(q_ref, qg_ref) * scale
        kn_ref[cur, :] = rope(k_ref, kg_ref)
        vn_ref[cur, :] = v_ref[0, rs, :].astype(F32)
        return carry

    lax.fori_loop(0, grp // prep_rows, prep, 0)

    for c in range(ncls):
        qc_ref[c * qcls:(c + 1) * qcls, :] = qn_ref[pl.ds(c, qcls, stride=ncls), :]
        kc_ref[c * kcls:(c + 1) * kcls, :] = kn_ref[pl.ds(c, kcls, stride=ncls), :]
        vc_ref[c * kcls:(c + 1) * kcls, :] = vn_ref[pl.ds(c, kcls, stride=ncls), :]

    row = lax.broadcasted_iota(jnp.int32, (Q_BLOCK, 2 * Q_BLOCK), 0)
    col = lax.broadcasted_iota(jnp.int32, (Q_BLOCK, 2 * Q_BLOCK), 1)
    key_rank = jnp.where(col < Q_BLOCK, jnp.where(col >= row, 0, 1), jnp.where(col - Q_BLOCK <= row, -1, 1))

    def aligned(start, size):
        return pl.ds(pl.multiple_of(start, Q_BLOCK), size)

    def unit_dil1(u):
        qsl = aligned(u * Q_BLOCK, Q_BLOCK)
        return qsl, aligned(grp + (u - 1) * Q_BLOCK, 2 * Q_BLOCK), None, qsl, u > 0

    def unit_dil4(u):
        c, nb = u % ncls, u // ncls
        qsl = aligned(c * qcls + nb * Q_BLOCK, Q_BLOCK)
        ksl = aligned(c * kcls + grp // ncls + (nb - 1) * Q_BLOCK, 2 * Q_BLOCK)
        return qsl, ksl, pl.ds(nb * (Q_BLOCK * ncls) + c, Q_BLOCK, stride=ncls), qsl, nb > 0

    def unit_dil16(u):
        c, a = u % ncls, u // ncls
        qsl = pl.ds(c * qcls + a, Q_BLOCK, stride=ncls)
        return qsl, pl.ds(c * kcls + a, 2 * Q_BLOCK, stride=ncls), qsl, qsl, False

    def branch(unit, qkv, st_in, st_out, first, last):
        qb, kb, vb = qkv

        def units(it, carry):
            sl = [unit(it * ATT_UNROLL + t) for t in range(ATT_UNROLL)]
            scores = []
            for qsl, ksl, _, _, has_prev in sl:
                s = _dot_nt(qb[qsl, :].astype(BF16), kb[ksl, :].astype(BF16))
                rank_limit = jnp.where((g > 0) | has_prev, 1, 0)
                scores.append(jnp.where(key_rank < rank_limit, s, -jnp.inf))
            probs, m_news, alphas, l_news = [], [], [], []
            for (_, _, isl, _, _), s in zip(sl, scores):
                mx = jnp.max(s, axis=-1, keepdims=True)
                if first:
                    m_new = jnp.broadcast_to(mx, (Q_BLOCK, LANES))
                else:
                    m_old = st_in[1][isl, :]
                    m_new = jnp.maximum(m_old, mx)
                    alphas.append(jnp.exp(m_old - m_new))
                p = jnp.exp(s - jnp.concatenate([m_new, m_new], axis=1))
                l_news.append(jnp.broadcast_to(jnp.sum(p, axis=-1, keepdims=True), (Q_BLOCK, LANES)))
                probs.append(p.astype(BF16))
                m_news.append(m_new)
            accs = [jnp.dot(p, vb[ksl, :].astype(BF16), preferred_element_type=F32)
                    for (_, ksl, _, _, _), p in zip(sl, probs)]
            if not first:
                l_news = [a * st_in[2][isl, :] + ln for (_, _, isl, _, _), a, ln in zip(sl, alphas, l_news)]
                accs = [a * st_in[0][isl, :] + ac for (_, _, isl, _, _), a, ac in zip(sl, alphas, accs)]
            for (_, _, _, osl, _), ac, mn, ln in zip(sl, accs, m_news, l_news):
                if last:
                    st_out[0][osl, :] = ac / ln
                else:
                    st_out[0][osl, :] = ac
                    st_out[1][osl, :] = mn
                    st_out[2][osl, :] = ln
            return carry

        lax.fori_loop(0, grp // Q_BLOCK // ATT_UNROLL, units, 0)

    natural, classes = (acc_n, m_n, l_n), (acc_c, m_c, l_c)
    branch(unit_dil1, (qn_ref, kn_ref, vn_ref), None, natural, True, False)
    branch(unit_dil4, (qc_ref, kc_ref, vc_ref), natural, classes, False, False)
    branch(unit_dil16, (qc_ref, kc_ref, vc_ref), classes, classes, False, True)
    for c in range(ncls):
        acc_n[pl.ds(c, qcls, stride=ncls), :] = acc_c[c * qcls:(c + 1) * qcls, :]
    o_ref[0] = acc_n[...].astype(o_ref.dtype)


def _fused_attention(u3, cos, sin, qn_g, kn_g, wl=0):
    b, s, width = u3.shape
    grp = ATT_GROUP
    assert s % grp == 0
    q_col = (width - 3 * ATT_DIM) // HEAD_DIM

    def col(slab):
        return pl.BlockSpec((1, grp, HEAD_DIM), lambda i, h, t: (i, t, q_col + slab * ATT_HEADS + h))

    tab = pl.BlockSpec((grp, HEAD_DIM), lambda i, h, t: (t, 0))
    vec = pl.BlockSpec((None, 1, HEAD_DIM), lambda i, h, t: (wl, 0, 0))
    scr = lambda rows: pltpu.VMEM((rows, HEAD_DIM), F32)
    return pl.pallas_call(
        _attn_body,
        grid=(b, ATT_HEADS, s // grp),
        in_specs=[col(0), col(1), col(2), tab, tab, vec, vec],
        out_specs=pl.BlockSpec((1, grp, HEAD_DIM), lambda i, h, t: (i, t, h)),
        out_shape=jax.ShapeDtypeStruct((b, s, ATT_DIM), BF16),
        scratch_shapes=[scr(grp), scr(2 * grp), scr(2 * grp)] * 2 + [scr(grp)] * 6,
        compiler_params=pltpu.CompilerParams(
            dimension_semantics=("parallel", "parallel", "arbitrary"), vmem_limit_bytes=VMEM_LIMIT),
        name="dilated_attn",
    )(u3, u3, u3, cos, sin, qn_g, kn_g)


def _hgrn_body(q_ref, kk_ref, logf_ref, v_ref, gate_ref, gn_ref, o_ref, st_ref, b_ref, cum_ref, mask_ref,
               *, rows, chunk):
    sub = chunk // 2
    levels = [sub >> (i + 1) for i in range(sub.bit_length() - 1)]
    small = [h for h in levels if 2 * h < SUBLANES]

    @pl.when(pl.program_id(2) == 0)
    def _():
        st_ref[...] = jnp.zeros_like(st_ref)

    ri = lax.broadcasted_iota(jnp.int32, (chunk, chunk), 0)
    ci = lax.broadcasted_iota(jnp.int32, (chunk, chunk), 1)
    cum_ref[...] = (ci <= ri).astype(BF16)
    row = lax.broadcasted_iota(jnp.int32, (chunk, HEAD_DIM), 0)
    rs_ = lax.broadcasted_iota(jnp.int32, (sub, sub), 0)
    cs_ = lax.broadcasted_iota(jnp.int32, (sub, sub), 1)
    for n, half in enumerate(levels):
        blk = 2 * half
        keep = ((rs_ // blk) == (cs_ // blk)) & ((rs_ & (blk - 1)) >= half) & ((cs_ & (blk - 1)) < half)
        mask_ref[n] = keep.astype(F32)

    def decay(b_rows, b_mid):
        return jnp.exp2(jnp.abs(b_rows - b_mid) * (-LOG2E))

    heads = range(HGRN_HEADS_PER_STEP)

    def one_chunk(c, carry):
        rs = pl.ds(pl.multiple_of(c * chunk, chunk), chunk)
        hsl = [slice(hh * HEAD_DIM, (hh + 1) * HEAD_DIM) for hh in heads]
        q = [q_ref[0, rs, hs].astype(F32) for hs in hsl]
        kk = [kk_ref[0, rs, hs].astype(F32) for hs in hsl]
        v = [v_ref[0, rs, hs] for hs in hsl]
        bcum = []
        for hs in hsl:
            logf = logf_ref[0, rs, hs]
            hi = logf.astype(BF16)
            rem = logf - hi.astype(F32)
            mid = rem.astype(BF16)
            lo = (rem - mid.astype(F32)).astype(BF16)
            parts = jnp.dot(cum_ref[...], jnp.concatenate([hi, mid, lo], axis=1), preferred_element_type=F32)
            bcum.append(parts[:, :HEAD_DIM] + parts[:, HEAD_DIM:2 * HEAD_DIM] + parts[:, 2 * HEAD_DIM:])
        for hh in heads:
            b_ref[hh] = bcum[hh]

        att = [[jnp.zeros((sub, sub), F32), jnp.zeros((sub, sub), F32)] for _ in heads]
        for n, half in enumerate(levels):
            blk = 2 * half
            for hh in heads:
                if half in small:
                    pos = row & (blk - 1)
                    b_mid = bcum[hh]
                    for p in range(blk):
                        off = half - 1 - p
                        if off != 0:
                            b_mid = jnp.where(pos == p, pltpu.roll(bcum[hh], (-off) % chunk, 0), b_mid)
                else:
                    pieces = [jnp.broadcast_to(b_ref[hh, s0 + half - 1:s0 + half, :], (blk, HEAD_DIM))
                              for s0 in range(0, chunk, blk)]
                    b_mid = jnp.concatenate(pieces, axis=0)
                dec = decay(bcum[hh], b_mid)
                gq = (q[hh] * dec).astype(BF16)
                hk = (kk[hh] * dec).astype(BF16)
                for d in range(2):
                    blk_rows = slice(d * sub, (d + 1) * sub)
                    att[hh][d] = att[hh][d] + _dot_nt(gq[blk_rows], hk[blk_rows]) * mask_ref[n]
        cross = []
        for hh in heads:
            b_top = b_ref[hh, sub - 1:sub, :]
            cross.append(_dot_nt((q[hh][sub:] * decay(bcum[hh][sub:], b_top)).astype(BF16),
                                 (kk[hh][:sub] * decay(bcum[hh][:sub], b_top)).astype(BF16)))

        outs = []
        for hh in heads:
            st = st_ref[hh]
            o_lo = jnp.dot(att[hh][0].astype(BF16), v[hh][:sub], preferred_element_type=F32)
            o_hi = (jnp.dot(att[hh][1].astype(BF16), v[hh][sub:], preferred_element_type=F32)
                    + jnp.dot(cross[hh].astype(BF16), v[hh][:sub], preferred_element_type=F32))
            o = jnp.concatenate([o_lo, o_hi], axis=0)
            vf = v[hh].astype(F32)
            o = o + jnp.sum(q[hh] * kk[hh], axis=-1, keepdims=True) * vf
            o = o + _dot_nt((q[hh] * jnp.exp(bcum[hh])).astype(BF16), st.astype(BF16))
            b_last = b_ref[hh, chunk - 1:chunk, :]
            k_dec = (kk[hh] * jnp.exp(b_last - bcum[hh])).astype(BF16)
            st_ref[hh] = st * jnp.exp(b_last) + jnp.dot(vf.T.astype(BF16), k_dec, preferred_element_type=F32)
            outs.append(o)
        for hh, hs in zip(heads, hsl):
            on = _rms_scale(outs[hh]) * gn_ref[:, hs] * gate_ref[0, rs, hs].astype(F32)
            o_ref[0, rs, hs] = on.astype(o_ref.dtype)
        return carry

    lax.fori_loop(0, rows // chunk, one_chunk, 0)


def _hgrn(q3, kk3, logf3, v3, gate3, gn_g, wl, *, rows=1024, chunk=256):
    b, s, _ = logf3.shape
    rows = min(rows, s)
    hp = HGRN_HEADS_PER_STEP
    wide = hp * HEAD_DIM
    sub = chunk // 2
    n_levels = sub.bit_length() - 1
    blk = pl.BlockSpec((1, rows, wide), lambda i, h, r: (i, r, h))
    return pl.pallas_call(
        functools.partial(_hgrn_body, rows=rows, chunk=chunk),
        grid=(b, HGRN_WIDTH // wide, s // rows),
        in_specs=[blk, blk, blk, blk, blk,
                  pl.BlockSpec((None, 1, wide), lambda i, h, r: (wl, 0, h))],
        out_specs=blk,
        out_shape=jax.ShapeDtypeStruct((b, s, HGRN_WIDTH), BF16),
        scratch_shapes=[pltpu.VMEM((hp, HEAD_DIM, HEAD_DIM), F32),
                        pltpu.VMEM((hp, chunk, HEAD_DIM), F32),
                        pltpu.VMEM((chunk, chunk), BF16),
                        pltpu.VMEM((n_levels, sub, sub), F32)],
        compiler_params=pltpu.CompilerParams(
            dimension_semantics=("parallel", "parallel", "arbitrary"), vmem_limit_bytes=VMEM_LIMIT),
        name="hgrn2",
    )(q3, kk3, logf3, v3, gate3, gn_g)


def _rope_tables(s):
    half = HEAD_DIM // 2
    inv = jnp.exp(-math.log(ROPE_THETA) * jnp.arange(half, dtype=F32) / half)
    ang = jnp.arange(s, dtype=jnp.int32).astype(F32)[:, None] * inv[None, :]
    cos, sin = jnp.cos(ang), jnp.sin(ang)
    return jnp.concatenate([cos, cos], axis=-1), jnp.concatenate([-sin, sin], axis=-1)


def _conv_attn_mixer(x2, b, s, norm_mix, layer, j, w_in, conv_w, conv_b, cn_g, cn_b, qn_g, kn_g, w_out, cos, sin):
    m = b * s
    n_in = w_in.shape[2]
    u = _norm_matmul(x2, norm_mix, w_in, layer, j, BF16)
    u3 = u.reshape(b, s, n_in)
    a = _conv_module(u3, conv_w, conv_b, cn_g, cn_b, j)
    o = _fused_attention(u3, cos, sin, qn_g, kn_g, j)
    return _outproj(x2, [a.reshape(m, CONV_CH), o.reshape(m, ATT_DIM)], w_out, j)


def _hgrn2_mixer(x2, b, s, norm_mix, layer, j, w_in, lb, gn_g, w_out):
    parts = _hgrn_inproj(x2, norm_mix, w_in, lb, layer, j)
    og = _hgrn(*[p.reshape(b, s, HGRN_WIDTH) for p in parts], gn_g, j)
    return _outproj(x2, [og.reshape(b * s, HGRN_WIDTH)], w_out, j)


def kernel(x, norm_ffn1, ffn1_wg, ffn1_wu, ffn1_wd, norm_mix, norm_ffn2, ffn2_wg, ffn2_wu, ffn2_wd, ev_w_in, ev_conv_w, ev_conv_b, ev_cn_g, ev_cn_b, ev_qn_g, ev_kn_g, ev_w_out, od_w_in, od_lb_logits, od_gn_g, od_w_out):
    b, s, d = x.shape
    depth = norm_ffn1.shape[0]
    cos, sin = _rope_tables(s)
    p = jax.nn.softmax(od_lb_logits.astype(F32), axis=0)
    lower_bounds = jnp.cumsum(p, axis=0) - p[0:1]
    ffn1 = [w.astype(BF16) for w in (ffn1_wg, ffn1_wu, ffn1_wd)]
    ffn2 = [w.astype(BF16) for w in (ffn2_wg, ffn2_wu, ffn2_wd)]
    ev_w_in_b, ev_w_out_b = ev_w_in.astype(BF16), ev_w_out.astype(BF16)
    od_w_in_b, od_w_out_b = od_w_in.astype(BF16), od_w_out.astype(BF16)
    row3 = lambda a: a[:, None, :]
    g1, gm, g2 = row3(norm_ffn1), row3(norm_mix), row3(norm_ffn2)
    x2 = x.reshape(b * s, d)
    for l in range(depth):
        j = l // 2
        x2 = _ffn(x2, g1, *ffn1, l)
        if l % 2 == 0:
            x2 = _conv_attn_mixer(x2, b, s, gm, l, j, ev_w_in_b, ev_conv_w, row3(ev_conv_b), row3(ev_cn_g),
                                  row3(ev_cn_b), row3(ev_qn_g), row3(ev_kn_g), ev_w_out_b, cos, sin)
        else:
            x2 = _hgrn2_mixer(x2, b, s, gm, l, j, od_w_in_b, lower_bounds[l][None], row3(od_gn_g), od_w_out_b)
        x2 = _ffn(x2, g2, *ffn2, l)
    return x2.reshape(b, s, d)
```

```python
import functools
import math

import jax
import jax.numpy as jnp
from jax import lax
from jax.experimental import pallas as pl
from jax.experimental.pallas import tpu as pltpu

F32 = jnp.float32
BF16 = jnp.bfloat16
EPS = 1e-6
LOG2E = 1.4426950408889634
F32_TINY = 1.1754944e-38

LANES = 128
SUBLANES = 8
HEAD_DIM = 128
CONV_CH = 1024
CONV_WIDTH = 31
CONV_HALO = 32
ATT_HEADS = 8
ATT_DIM = ATT_HEADS * HEAD_DIM
DIL_PATTERNS = ((128, 1), (512, 4), (2048, 16))
Q_BLOCK = 128
ATT_GROUP = Q_BLOCK * max(d for _, d in DIL_PATTERNS)
ROPE_THETA = 10000.0
HGRN_HEADS = 16
HGRN_WIDTH = HGRN_HEADS * HEAD_DIM
HGRN_HEADS_PER_STEP = 4
FFN_ROW_BLOCKS = 4
ATT_UNROLL = 8

VMEM_LIMIT = 60 * 1024 * 1024


def _sigmoid(x):
    return 0.5 * jnp.tanh(0.5 * x) + 0.5


def _rms_scale(x):
    return x * lax.rsqrt(jnp.mean(x * x, axis=-1, keepdims=True) + EPS)


def _dot_nt(a, b):
    return lax.dot_general(a, b, (((1,), (1,)), ((), ())), preferred_element_type=F32)


def _ffn_body(x_ref, g_ref, wg_ref, wu_ref, wd_ref, o_ref, xn_ref):
    def hidden(xn):
        hg = jnp.dot(xn, wg_ref[...], preferred_element_type=F32)
        hu = jnp.dot(xn, wu_ref[...], preferred_element_type=F32)
        return ((hg * _sigmoid(hg)) * hu * 0.5).astype(BF16)

    @pl.when(pl.program_id(1) == 0)
    def _():
        wd = wd_ref[...].astype(BF16)
        tm = x_ref.shape[0]
        rb = tm // FFN_ROW_BLOCKS
        for r0 in range(0, tm, rb):
            x = x_ref[r0:r0 + rb, :]
            xn = (_rms_scale(x) * g_ref[...]).astype(BF16)
            xn_ref[r0:r0 + rb, :] = xn
            o_ref[r0:r0 + rb, :] = x + jnp.dot(hidden(xn), wd, preferred_element_type=F32)

    @pl.when(pl.program_id(1) > 0)
    def _():
        o_ref[...] += jnp.dot(hidden(xn_ref[...]), wd_ref[...].astype(BF16), preferred_element_type=F32)


def _ffn(x2, g, wg, wu, wd, layer, *, tm=1024, tf=512):
    m, d = x2.shape
    f = wg.shape[2]
    tm = min(tm, m)
    return pl.pallas_call(
        _ffn_body,
        grid=(m // tm, f // tf),
        in_specs=[
            pl.BlockSpec((tm, d), lambda i, k: (i, 0)),
            pl.BlockSpec((None, 1, d), lambda i, k: (layer, 0, 0)),
            pl.BlockSpec((None, d, tf), lambda i, k: (layer, 0, k)),
            pl.BlockSpec((None, d, tf), lambda i, k: (layer, 0, k)),
            pl.BlockSpec((None, tf, d), lambda i, k: (layer, k, 0)),
        ],
        out_specs=pl.BlockSpec((tm, d), lambda i, k: (i, 0)),
        out_shape=jax.ShapeDtypeStruct((m, d), F32),
        scratch_shapes=[pltpu.VMEM((tm, d), BF16)],
        compiler_params=pltpu.CompilerParams(
            dimension_semantics=("parallel", "arbitrary"), vmem_limit_bytes=VMEM_LIMIT),
        name="ffn",
    )(x2, g, wg, wu, wd)


def _norm_matmul_body(x_ref, g_ref, w_ref, o_ref, xn_ref):
    @pl.when(pl.program_id(1) == 0)
    def _():
        xn_ref[...] = (_rms_scale(x_ref[...]) * g_ref[...]).astype(BF16)

    o_ref[...] = jnp.dot(xn_ref[...], w_ref[...], preferred_element_type=F32).astype(o_ref.dtype)


def _norm_matmul(x2, g, w, layer, wl, out_dtype, *, tm=1024, tn=1024):
    m, d = x2.shape
    n = w.shape[2]
    tm = min(tm, m)
    return pl.pallas_call(
        _norm_matmul_body,
        grid=(m // tm, n // tn),
        in_specs=[
            pl.BlockSpec((tm, d), lambda i, k: (i, 0)),
            pl.BlockSpec((None, 1, d), lambda i, k: (layer, 0, 0)),
            pl.BlockSpec((None, d, tn), lambda i, k: (wl, 0, k)),
        ],
        out_specs=pl.BlockSpec((tm, tn), lambda i, k: (i, k)),
        out_shape=jax.ShapeDtypeStruct((m, n), out_dtype),
        scratch_shapes=[pltpu.VMEM((tm, d), BF16)],
        compiler_params=pltpu.CompilerParams(
            dimension_semantics=("parallel", "arbitrary"), vmem_limit_bytes=VMEM_LIMIT),
        name="norm_matmul",
    )(x2, g, w)


def _hgrn_inproj_body(x_ref, g_ref, w_ref, lb_ref, q_ref, kk_ref, logf_ref, v_ref, gate_ref, xn_ref, *, per):
    k = pl.program_id(1)

    @pl.when(k == 0)
    def _():
        xn_ref[...] = (_rms_scale(x_ref[...]) * g_ref[...]).astype(BF16)

    def pre():
        return jnp.dot(xn_ref[...], w_ref[...], preferred_element_type=F32)

    @pl.when(k < per)
    def _():
        z = pre()
        q_ref[...] = (z * _sigmoid(z)).astype(q_ref.dtype)

    @pl.when((k >= per) & (k < 2 * per))
    def _():
        fz = pre()
        kk = (1.0 - lb_ref[...]) * (0.5 - 0.5 * jnp.tanh(0.5 * fz))
        kk_ref[...] = kk.astype(kk_ref.dtype)
        logf_ref[...] = jnp.log(jnp.maximum(1.0 - kk, F32_TINY))

    @pl.when((k >= 2 * per) & (k < 3 * per))
    def _():
        v_ref[...] = pre().astype(v_ref.dtype)

    @pl.when(k >= 3 * per)
    def _():
        z = pre()
        gate_ref[...] = (z * _sigmoid(z)).astype(gate_ref.dtype)


def _hgrn_inproj(x2, g, w, lb, layer, wl, *, tm=1024, tn=1024):
    m, d = x2.shape
    tm = min(tm, m)
    width = w.shape[2] // 4
    per = width // tn

    def slab(n):
        return pl.BlockSpec((tm, tn), lambda i, k: (i, jnp.clip(k - n * per, 0, per - 1)))

    bf = jax.ShapeDtypeStruct((m, width), BF16)
    return pl.pallas_call(
        functools.partial(_hgrn_inproj_body, per=per),
        grid=(m // tm, 4 * per),
        in_specs=[
            pl.BlockSpec((tm, d), lambda i, k: (i, 0)),
            pl.BlockSpec((None, 1, d), lambda i, k: (layer, 0, 0)),
            pl.BlockSpec((None, d, tn), lambda i, k: (wl, 0, k)),
            pl.BlockSpec((1, tn), lambda i, k: (0, jnp.clip(k - per, 0, per - 1))),
        ],
        out_specs=[slab(0), slab(1), slab(1), slab(2), slab(3)],
        out_shape=[bf, bf, jax.ShapeDtypeStruct((m, width), F32), bf, bf],
        scratch_shapes=[pltpu.VMEM((tm, d), BF16)],
        compiler_params=pltpu.CompilerParams(
            dimension_semantics=("parallel", "arbitrary"), vmem_limit_bytes=VMEM_LIMIT),
        name="hgrn_inproj",
    )(x2, g, w, lb)


def _outproj_body(*refs, n_in):
    x_ref, o_ref = refs[0], refs[-1]
    acc = x_ref[...]
    for y_ref, w_ref in zip(refs[1:1 + n_in], refs[1 + n_in:1 + 2 * n_in]):
        acc = acc + jnp.dot(y_ref[...], w_ref[...].astype(BF16), preferred_element_type=F32)
    o_ref[...] = acc


def _outproj(x2, ys, w, wl, *, tm=512):
    m, d = x2.shape
    tm = min(tm, m)
    in_specs = [pl.BlockSpec((tm, d), lambda i: (i, 0))]
    in_specs += [pl.BlockSpec((tm, y.shape[1]), lambda i: (i, 0)) for y in ys]
    row0 = 0
    for y in ys:
        rows = y.shape[1]
        assert row0 % rows == 0
        in_specs.append(pl.BlockSpec((None, rows, d), functools.partial(lambda i, rb: (wl, rb, 0), rb=row0 // rows)))
        row0 += rows
    return pl.pallas_call(
        functools.partial(_outproj_body, n_in=len(ys)),
        grid=(m // tm,),
        in_specs=in_specs,
        out_specs=pl.BlockSpec((tm, d), lambda i: (i, 0)),
        out_shape=jax.ShapeDtypeStruct((m, d), F32),
        compiler_params=pltpu.CompilerParams(
            dimension_semantics=("parallel",), vmem_limit_bytes=VMEM_LIMIT),
        name="outproj",
    )(x2, *ys, *([w] * len(ys)))


def _conv_body(val_ref, gate_ref, hval_ref, hgate_ref, w_ref, b_ref, lg_ref, lb_ref,
               o_ref, buf_ref, sh_ref, y_ref, *, ts):
    ah = hval_ref[0].astype(F32) * _sigmoid(hgate_ref[0].astype(F32))
    buf_ref[0:CONV_HALO, :] = jnp.where(pl.program_id(1) > 0, ah, 0.0)
    buf_ref[CONV_HALO:, :] = val_ref[0].astype(F32) * _sigmoid(gate_ref[0].astype(F32))
    span = ts + CONV_HALO - SUBLANES
    for r in range(1, SUBLANES):
        sh_ref[r - 1, :, :] = buf_ref[r:r + span, :]
    first = CONV_HALO - (CONV_WIDTH - 1)
    rb = min(ts, 16)

    def taps(i, carry):
        r0 = pl.multiple_of(i * rb, rb)
        for cb in range(CONV_CH // LANES):
            cs = slice(cb * LANES, (cb + 1) * LANES)
            acc = jnp.broadcast_to(b_ref[:, cs], (rb, LANES))
            for k in range(CONV_WIDTH):
                off = first + k
                rows = pl.ds(r0 + (off - off % SUBLANES), rb)
                if off % SUBLANES == 0:
                    tap = buf_ref[rows, cs]
                else:
                    tap = sh_ref[off % SUBLANES - 1, rows, cs]
                acc = acc + w_ref[k:k + 1, cs] * tap
            y_ref[pl.ds(r0, rb), cs] = acc
        return carry

    lax.fori_loop(0, ts // rb, taps, 0)

    nb = min(ts, 128)

    def norm(i, carry):
        rs = pl.ds(pl.multiple_of(i * nb, nb), nb)
        y = y_ref[rs, :]
        mu = jnp.mean(y, axis=-1, keepdims=True)
        yc = y - mu
        var = jnp.mean(yc * yc, axis=-1, keepdims=True)
        z = yc * lax.rsqrt(var + EPS) * lg_ref[...] + lb_ref[...]
        o_ref[0, rs, :] = (z * _sigmoid(z)).astype(o_ref.dtype)
        return carry

    lax.fori_loop(0, ts // nb, norm, 0)


def _conv_module(u3, conv_w, conv_b, cn_g, cn_b, wl, *, ts=256):
    b, s, _ = u3.shape
    ts = min(ts, s)
    hb = ts // CONV_HALO

    def halo_idx(col):
        return lambda i, t: (i, jnp.maximum(t * hb - 1, 0), col)

    vec = pl.BlockSpec((None, 1, CONV_CH), lambda i, t: (wl, 0, 0))
    return pl.pallas_call(
        functools.partial(_conv_body, ts=ts),
        grid=(b, s // ts),
        in_specs=[
            pl.BlockSpec((1, ts, CONV_CH), lambda i, t: (i, t, 0)),
            pl.BlockSpec((1, ts, CONV_CH), lambda i, t: (i, t, 1)),
            pl.BlockSpec((1, CONV_HALO, CONV_CH), halo_idx(0)),
            pl.BlockSpec((1, CONV_HALO, CONV_CH), halo_idx(1)),
            pl.BlockSpec((None, CONV_WIDTH, CONV_CH), lambda i, t: (wl, 0, 0)),
            vec, vec, vec,
        ],
        out_specs=pl.BlockSpec((1, ts, CONV_CH), lambda i, t: (i, t, 0)),
        out_shape=jax.ShapeDtypeStruct((b, s, CONV_CH), BF16),
        scratch_shapes=[pltpu.VMEM((ts + CONV_HALO, CONV_CH), F32),
                        pltpu.VMEM((SUBLANES - 1, ts + CONV_HALO - SUBLANES, CONV_CH), F32),
                        pltpu.VMEM((ts, CONV_CH), F32)],
        compiler_params=pltpu.CompilerParams(
            dimension_semantics=("parallel", "parallel"), vmem_limit_bytes=VMEM_LIMIT),
        name="conv_module",
    )(u3, u3, u3, u3, conv_w, conv_b, cn_g, cn_b)


def _attn_body(q_ref, k_ref, v_ref, cos_ref, sin_ref, qg_ref, kg_ref, o_ref,
               qn_ref, kn_ref, vn_ref, qc_ref, kc_ref, vc_ref,
               acc_n, m_n, l_n, acc_c, m_c, l_c):
    g = pl.program_id(2)
    grp = ATT_GROUP
    assert tuple(d for _, d in DIL_PATTERNS) == (1, 4, 16) and all(w // d == Q_BLOCK for w, d in DIL_PATTERNS)
    ncls = 4
    qcls = grp // ncls
    kcls = 2 * grp // ncls

    @pl.when(g == 0)
    def _():
        kn_ref[0:grp, :] = jnp.zeros((grp, HEAD_DIM), F32)
        vn_ref[0:grp, :] = jnp.zeros((grp, HEAD_DIM), F32)

    @pl.when(g > 0)
    def _():
        kn_ref[0:grp, :] = kn_ref[grp:2 * grp, :]
        vn_ref[0:grp, :] = vn_ref[grp:2 * grp, :]

    scale = HEAD_DIM ** -0.5
    prep_rows = 512

    def prep(i, carry):
        r0 = pl.multiple_of(i * prep_rows, prep_rows)
        rs = pl.ds(r0, prep_rows)
        cur = pl.ds(grp + r0, prep_rows)
        cos = cos_ref[rs, :]
        sin = sin_ref[rs, :]

        def rope(ref, gain_ref):
            y = _rms_scale(ref[0, rs, :].astype(F32)) * gain_ref[...]
            return y * cos + pltpu.roll(y, HEAD_DIM // 2, 1) * sin

        qn_ref[rs, :] = rope(q_ref, qg_ref) * (scale * LOG2E)
        kn_ref[cur, :] = rope(k_ref, kg_ref)
        vn_ref[cur, :] = v_ref[0, rs, :].astype(F32)
        return carry

    lax.fori_loop(0, grp // prep_rows, prep, 0)

    for c in range(ncls):
        qc_ref[c * qcls:(c + 1) * qcls, :] = qn_ref[pl.ds(c, qcls, stride=ncls), :]
        kc_ref[c * kcls:(c + 1) * kcls, :] = kn_ref[pl.ds(c, kcls, stride=ncls), :]
        vc_ref[c * kcls:(c + 1) * kcls, :] = vn_ref[pl.ds(c, kcls, stride=ncls), :]

    row = lax.broadcasted_iota(jnp.int32, (Q_BLOCK, 2 * Q_BLOCK), 0)
    col = lax.broadcasted_iota(jnp.int32, (Q_BLOCK, 2 * Q_BLOCK), 1)
    key_rank = jnp.where(col < Q_BLOCK, jnp.where(col >= row, 0, 1), jnp.where(col - Q_BLOCK <= row, -1, 1))

    def aligned(start, size):
        return pl.ds(pl.multiple_of(start, Q_BLOCK), size)

    def unit_dil1(u):
        qsl = aligned(u * Q_BLOCK, Q_BLOCK)
        return qsl, aligned(grp + (u - 1) * Q_BLOCK, 2 * Q_BLOCK), None, qsl, u > 0

    def unit_dil4(u):
        c, nb = u % ncls, u // ncls
        qsl = aligned(c * qcls + nb * Q_BLOCK, Q_BLOCK)
        ksl = aligned(c * kcls + grp // ncls + (nb - 1) * Q_BLOCK, 2 * Q_BLOCK)
        return qsl, ksl, pl.ds(nb * (Q_BLOCK * ncls) + c, Q_BLOCK, stride=ncls), qsl, nb > 0

    def unit_dil16(u):
        c, a = u % ncls, u // ncls
        qsl = pl.ds(c * qcls + a, Q_BLOCK, stride=ncls)
        return qsl, pl.ds(c * kcls + a, 2 * Q_BLOCK, stride=ncls), qsl, qsl, False

    def branch(unit, qkv, st_in, st_out, first, last):
        qb, kb, vb = qkv

        def units(it, carry):
            sl = [unit(it * ATT_UNROLL + t) for t in range(ATT_UNROLL)]
            scores = []
            for qsl, ksl, _, _, has_prev in sl:
                s = _dot_nt(qb[qsl, :].astype(BF16), kb[ksl, :].astype(BF16))
                rank_limit = jnp.where((g > 0) | has_prev, 1, 0)
                scores.append(jnp.where(key_rank < rank_limit, s, -jnp.inf))
            probs, m_news, alphas, l_news = [], [], [], []
            for (_, _, isl, _, _), s in zip(sl, scores):
                mx = jnp.max(s, axis=-1, keepdims=True)
                if first:
                    m_new = jnp.broadcast_to(mx, (Q_BLOCK, LANES))
                else:
                    m_old = st_in[1][isl, :]
                    m_new = jnp.maximum(m_old, mx)
                    alphas.append(jnp.exp2(m_old - m_new))
                p = jnp.exp2(s - jnp.concatenate([m_new, m_new], axis=1))
                l_news.append(jnp.broadcast_to(jnp.sum(p, axis=-1, keepdims=True), (Q_BLOCK, LANES)))
                probs.append(p.astype(BF16))
                m_news.append(m_new)
            accs = [jnp.dot(p, vb[ksl, :].astype(BF16), preferred_element_type=F32)
                    for (_, ksl, _, _, _), p in zip(sl, probs)]
            if not first:
                l_news = [a * st_in[2][isl, :] + ln for (_, _, isl, _, _), a, ln in zip(sl, alphas, l_news)]
                accs = [a * st_in[0][isl, :] + ac for (_, _, isl, _, _), a, ac in zip(sl, alphas, accs)]
            for (_, _, _, osl, _), ac, mn, ln in zip(sl, accs, m_news, l_news):
                if last:
                    st_out[0][osl, :] = ac / ln
                else:
                    st_out[0][osl, :] = ac
                    st_out[1][osl, :] = mn
                    st_out[2][osl, :] = ln
            return carry

        lax.fori_loop(0, grp // Q_BLOCK // ATT_UNROLL, units, 0)

    natural, classes = (acc_n, m_n, l_n), (acc_c, m_c, l_c)
    branch(unit_dil1, (qn_ref, kn_ref, vn_ref), None, natural, True, False)
    branch(unit_dil4, (qc_ref, kc_ref, vc_ref), natural, classes, False, False)
    branch(unit_dil16, (qc_ref, kc_ref, vc_ref), classes, classes, False, True)
    for c in range(ncls):
        acc_n[pl.ds(c, qcls, stride=ncls), :] = acc_c[c * qcls:(c + 1) * qcls, :]
    o_ref[0] = acc_n[...].astype(o_ref.dtype)


def _fused_attention(u3, cos, sin, qn_g, kn_g, wl=0):
    b, s, width = u3.shape
    grp = ATT_GROUP
    assert s % grp == 0
    q_col = (width - 3 * ATT_DIM) // HEAD_DIM

    def col(slab):
        return pl.BlockSpec((1, grp, HEAD_DIM), lambda i, h, t: (i, t, q_col + slab * ATT_HEADS + h))

    tab = pl.BlockSpec((grp, HEAD_DIM), lambda i, h, t: (t, 0))
    vec = pl.BlockSpec((None, 1, HEAD_DIM), lambda i, h, t: (wl, 0, 0))
    scr = lambda rows: pltpu.VMEM((rows, HEAD_DIM), F32)
    return pl.pallas_call(
        _attn_body,
        grid=(b, ATT_HEADS, s // grp),
        in_specs=[col(0), col(1), col(2), tab, tab, vec, vec],
        out_specs=pl.BlockSpec((1, grp, HEAD_DIM), lambda i, h, t: (i, t, h)),
        out_shape=jax.ShapeDtypeStruct((b, s, ATT_DIM), BF16),
        scratch_shapes=[scr(grp), scr(2 * grp), scr(2 * grp)] * 2 + [scr(grp)] * 6,
        compiler_params=pltpu.CompilerParams(
            dimension_semantics=("parallel", "parallel", "arbitrary"), vmem_limit_bytes=VMEM_LIMIT),
        name="dilated_attn",
    )(u3, u3, u3, cos, sin, qn_g, kn_g)


def _hgrn_body(q_ref, kk_ref, logf_ref, v_ref, gate_ref, gn_ref, o_ref, st_ref, b_ref, cum_ref, mask_ref,
               sgn_ref, *, rows, chunk):
    sub = chunk // 2
    levels = [sub >> (i + 1) for i in range(sub.bit_length() - 1)]
    small = [h for h in levels if 2 * h < SUBLANES]

    @pl.when(pl.program_id(2) == 0)
    def _():
        st_ref[...] = jnp.zeros_like(st_ref)

    ri = lax.broadcasted_iota(jnp.int32, (chunk, chunk), 0)
    ci = lax.broadcasted_iota(jnp.int32, (chunk, chunk), 1)
    cum_ref[...] = (ci <= ri).astype(BF16)
    row = lax.broadcasted_iota(jnp.int32, (chunk, HEAD_DIM), 0)
    rs_ = lax.broadcasted_iota(jnp.int32, (sub, sub), 0)
    cs_ = lax.broadcasted_iota(jnp.int32, (sub, sub), 1)
    for n, half in enumerate(levels):
        blk = 2 * half
        keep = ((rs_ // blk) == (cs_ // blk)) & ((rs_ & (blk - 1)) >= half) & ((cs_ & (blk - 1)) < half)
        mask_ref[n] = keep.astype(F32)
        sgn_ref[n] = jnp.where((row & (blk - 1)) >= half, 1.0, -1.0)

    heads = range(HGRN_HEADS_PER_STEP)

    def one_chunk(c, carry):
        rs = pl.ds(pl.multiple_of(c * chunk, chunk), chunk)
        hsl = [slice(hh * HEAD_DIM, (hh + 1) * HEAD_DIM) for hh in heads]
        q = [q_ref[0, rs, hs].astype(F32) for hs in hsl]
        kk = [kk_ref[0, rs, hs].astype(F32) for hs in hsl]
        v = [v_ref[0, rs, hs] for hs in hsl]
        bcum = []
        for hs in hsl:
            logf = logf_ref[0, rs, hs]
            hi = logf.astype(BF16)
            rem = logf - hi.astype(F32)
            mid = rem.astype(BF16)
            lo = (rem - mid.astype(F32)).astype(BF16)
            parts = jnp.dot(cum_ref[...], jnp.concatenate([hi, mid, lo], axis=1), preferred_element_type=F32)
            bcum.append((parts[:, :HEAD_DIM] + parts[:, HEAD_DIM:2 * HEAD_DIM] + parts[:, 2 * HEAD_DIM:]) * LOG2E)
        for hh in heads:
            b_ref[hh] = bcum[hh]

        att = [[jnp.zeros((sub, sub), F32), jnp.zeros((sub, sub), F32)] for _ in heads]
        for n, half in enumerate(levels):
            blk = 2 * half
            for hh in heads:
                if half in small:
                    pos = row & (blk - 1)
                    b_mid = bcum[hh]
                    for p in range(blk):
                        off = half - 1 - p
                        if off != 0:
                            b_mid = jnp.where(pos == p, pltpu.roll(bcum[hh], (-off) % chunk, 0), b_mid)
                else:
                    pieces = [jnp.broadcast_to(b_ref[hh, s0 + half - 1:s0 + half, :], (blk, HEAD_DIM))
                              for s0 in range(0, chunk, blk)]
                    b_mid = jnp.concatenate(pieces, axis=0)
                dec = jnp.exp2((bcum[hh] - b_mid) * sgn_ref[n])
                gq = (q[hh] * dec).astype(BF16)
                hk = (kk[hh] * dec).astype(BF16)
                for d in range(2):
                    blk_rows = slice(d * sub, (d + 1) * sub)
                    att[hh][d] = att[hh][d] + _dot_nt(gq[blk_rows], hk[blk_rows]) * mask_ref[n]
        cross = []
        for hh in heads:
            b_top = b_ref[hh, sub - 1:sub, :]
            cross.append(_dot_nt((q[hh][sub:] * jnp.exp2(bcum[hh][sub:] - b_top)).astype(BF16),
                                 (kk[hh][:sub] * jnp.exp2(b_top - bcum[hh][:sub])).astype(BF16)))

        outs = []
        for hh in heads:
            st = st_ref[hh]
            o_lo = jnp.dot(att[hh][0].astype(BF16), v[hh][:sub], preferred_element_type=F32)
            o_hi = (jnp.dot(att[hh][1].astype(BF16), v[hh][sub:], preferred_element_type=F32)
                    + jnp.dot(cross[hh].astype(BF16), v[hh][:sub], preferred_element_type=F32))
            o = jnp.concatenate([o_lo, o_hi], axis=0)
            vf = v[hh].astype(F32)
            o = o + jnp.sum(q[hh] * kk[hh], axis=-1, keepdims=True) * vf
            o = o + _dot_nt((q[hh] * jnp.exp2(bcum[hh])).astype(BF16), st.astype(BF16))
            b_last = b_ref[hh, chunk - 1:chunk, :]
            k_dec = (kk[hh] * jnp.exp2(b_last - bcum[hh])).astype(BF16)
            st_ref[hh] = st * jnp.exp2(b_last) + jnp.dot(vf.T.astype(BF16), k_dec, preferred_element_type=F32)
            outs.append(o)
        for hh, hs in zip(heads, hsl):
            on = _rms_scale(outs[hh]) * gn_ref[:, hs] * gate_ref[0, rs, hs].astype(F32)
            o_ref[0, rs, hs] = on.astype(o_ref.dtype)
        return carry

    lax.fori_loop(0, rows // chunk, one_chunk, 0)


def _hgrn(q3, kk3, logf3, v3, gate3, gn_g, wl, *, rows=1024, chunk=256):
    b, s, _ = logf3.shape
    rows = min(rows, s)
    hp = HGRN_HEADS_PER_STEP
    wide = hp * HEAD_DIM
    sub = chunk // 2
    n_levels = sub.bit_length() - 1
    blk = pl.BlockSpec((1, rows, wide), lambda i, h, r: (i, r, h))
    return pl.pallas_call(
        functools.partial(_hgrn_body, rows=rows, chunk=chunk),
        grid=(b, HGRN_WIDTH // wide, s // rows),
        in_specs=[blk, blk, blk, blk, blk,
                  pl.BlockSpec((None, 1, wide), lambda i, h, r: (wl, 0, h))],
        out_specs=blk,
        out_shape=jax.ShapeDtypeStruct((b, s, HGRN_WIDTH), BF16),
        scratch_shapes=[pltpu.VMEM((hp, HEAD_DIM, HEAD_DIM), F32),
                        pltpu.VMEM((hp, chunk, HEAD_DIM), F32),
                        pltpu.VMEM((chunk, chunk), BF16),
                        pltpu.VMEM((n_levels, sub, sub), F32),
                        pltpu.VMEM((n_levels, chunk, HEAD_DIM), F32)],
        compiler_params=pltpu.CompilerParams(
            dimension_semantics=("parallel", "parallel", "arbitrary"), vmem_limit_bytes=VMEM_LIMIT),
        name="hgrn2",
    )(q3, kk3, logf3, v3, gate3, gn_g)


def _rope_tables(s):
    half = HEAD_DIM // 2
    inv = jnp.exp(-math.log(ROPE_THETA) * jnp.arange(half, dtype=F32) / half)
    ang = jnp.arange(s, dtype=jnp.int32).astype(F32)[:, None] * inv[None, :]
    cos, sin = jnp.cos(ang), jnp.sin(ang)
    return jnp.concatenate([cos, cos], axis=-1), jnp.concatenate([-sin, sin], axis=-1)


def _conv_attn_mixer(x2, b, s, norm_mix, layer, j, w_in, conv_w, conv_b, cn_g, cn_b, qn_g, kn_g, w_out, cos, sin):
    m = b * s
    n_in = w_in.shape[2]
    u = _norm_matmul(x2, norm_mix, w_in, layer, j, BF16)
    u3 = u.reshape(b, s, n_in)
    a = _conv_module(u3, conv_w, conv_b, cn_g, cn_b, j)
    o = _fused_attention(u3, cos, sin, qn_g, kn_g, j)
    return _outproj(x2, [a.reshape(m, CONV_CH), o.reshape(m, ATT_DIM)], w_out, j)


def _hgrn2_mixer(x2, b, s, norm_mix, layer, j, w_in, lb, gn_g, w_out):
    parts = _hgrn_inproj(x2, norm_mix, w_in, lb, layer, j)
    og = _hgrn(*[p.reshape(b, s, HGRN_WIDTH) for p in parts], gn_g, j)
    return _outproj(x2, [og.reshape(b * s, HGRN_WIDTH)], w_out, j)


def kernel(x, norm_ffn1, ffn1_wg, ffn1_wu, ffn1_wd, norm_mix, norm_ffn2, ffn2_wg, ffn2_wu, ffn2_wd, ev_w_in, ev_conv_w, ev_conv_b, ev_cn_g, ev_cn_b, ev_qn_g, ev_kn_g, ev_w_out, od_w_in, od_lb_logits, od_gn_g, od_w_out):
    b, s, d = x.shape
    depth = norm_ffn1.shape[0]
    cos, sin = _rope_tables(s)
    p = jax.nn.softmax(od_lb_logits.astype(F32), axis=0)
    lower_bounds = jnp.cumsum(p, axis=0) - p[0:1]
    ffn1 = [ffn1_wg.astype(BF16), ffn1_wu.astype(BF16), ffn1_wd]
    ffn2 = [ffn2_wg.astype(BF16), ffn2_wu.astype(BF16), ffn2_wd]
    row3 = lambda a: a[:, None, :]
    g1, gm, g2 = row3(norm_ffn1), row3(norm_mix), row3(norm_ffn2)
    x2 = x.reshape(b * s, d)
    for l in range(depth):
        j = l // 2
        x2 = _ffn(x2, g1, *ffn1, l)
        if l % 2 == 0:
            x2 = _conv_attn_mixer(x2, b, s, gm, l, j, ev_w_in.astype(BF16), ev_conv_w, row3(ev_conv_b), row3(ev_cn_g),
                                  row3(ev_cn_b), row3(ev_qn_g), row3(ev_kn_g), ev_w_out, cos, sin)
        else:
            x2 = _hgrn2_mixer(x2, b, s, gm, l, j, od_w_in.astype(BF16), lower_bounds[l][None], row3(od_gn_g),
                              od_w_out)
        x2 = _ffn(x2, g2, *ffn2, l)
    return x2.reshape(b, s, d)
```

```python
import functools
import math

import jax
import jax.numpy as jnp
from jax import lax
from jax.experimental import pallas as pl
from jax.experimental.pallas import tpu as pltpu

F32 = jnp.float32
BF16 = jnp.bfloat16
EPS = 1e-6
LOG2E = 1.4426950408889634
F32_TINY = 1.1754944e-38

LANES = 128
SUBLANES = 8
HEAD_DIM = 128
CONV_CH = 1024
CONV_WIDTH = 31
CONV_HALO = 32
ATT_HEADS = 8
ATT_DIM = ATT_HEADS * HEAD_DIM
DIL_PATTERNS = ((128, 1), (512, 4), (2048, 16))
Q_BLOCK = 128
ATT_GROUP = Q_BLOCK * max(d for _, d in DIL_PATTERNS)
ROPE_THETA = 10000.0
HGRN_HEADS = 16
HGRN_WIDTH = HGRN_HEADS * HEAD_DIM
HGRN_HEADS_PER_STEP = 4
FFN_ROW_BLOCKS = 4
ATT_UNROLL = 8

VMEM_LIMIT = 60 * 1024 * 1024


def _sigmoid(x):
    return 0.5 * jnp.tanh(0.5 * x) + 0.5


def _rms_scale(x):
    return x * lax.rsqrt(jnp.mean(x * x, axis=-1, keepdims=True) + EPS)


def _dot_nt(a, b):
    return lax.dot_general(a, b, (((1,), (1,)), ((), ())), preferred_element_type=F32)


def _ffn_body(x_ref, g_ref, wg_ref, wu_ref, wd_ref, o_ref, xn_ref):
    def hidden(xn):
        hg = jnp.dot(xn, wg_ref[...], preferred_element_type=F32)
        hu = jnp.dot(xn, wu_ref[...], preferred_element_type=F32)
        return ((hg * _sigmoid(hg)) * hu * 0.5).astype(BF16)

    @pl.when(pl.program_id(1) == 0)
    def _():
        wd = wd_ref[...].astype(BF16)
        tm = x_ref.shape[0]
        rb = tm // FFN_ROW_BLOCKS
        for r0 in range(0, tm, rb):
            x = x_ref[r0:r0 + rb, :]
            xn = (_rms_scale(x) * g_ref[...]).astype(BF16)
            xn_ref[r0:r0 + rb, :] = xn
            o_ref[r0:r0 + rb, :] = x + jnp.dot(hidden(xn), wd, preferred_element_type=F32)

    @pl.when(pl.program_id(1) > 0)
    def _():
        o_ref[...] += jnp.dot(hidden(xn_ref[...]), wd_ref[...].astype(BF16), preferred_element_type=F32)


def _ffn(x2, g, wg, wu, wd, layer, gu_layer, *, tm=1024, tf=512):
    m, d = x2.shape
    f = wg.shape[2]
    tm = min(tm, m)
    return pl.pallas_call(
        _ffn_body,
        grid=(m // tm, f // tf),
        in_specs=[
            pl.BlockSpec((tm, d), lambda i, k: (i, 0)),
            pl.BlockSpec((None, 1, d), lambda i, k: (layer, 0, 0)),
            pl.BlockSpec((None, d, tf), lambda i, k: (gu_layer, 0, k)),
            pl.BlockSpec((None, d, tf), lambda i, k: (gu_layer, 0, k)),
            pl.BlockSpec((None, tf, d), lambda i, k: (layer, k, 0)),
        ],
        out_specs=pl.BlockSpec((tm, d), lambda i, k: (i, 0)),
        out_shape=jax.ShapeDtypeStruct((m, d), F32),
        scratch_shapes=[pltpu.VMEM((tm, d), BF16)],
        compiler_params=pltpu.CompilerParams(
            dimension_semantics=("parallel", "arbitrary"), vmem_limit_bytes=VMEM_LIMIT),
        name="ffn",
    )(x2, g, wg, wu, wd)


def _norm_matmul_body(x_ref, g_ref, w_ref, o_ref, xn_ref):
    @pl.when(pl.program_id(1) == 0)
    def _():
        xn_ref[...] = (_rms_scale(x_ref[...]) * g_ref[...]).astype(BF16)

    o_ref[...] = jnp.dot(xn_ref[...], w_ref[...], preferred_element_type=F32).astype(o_ref.dtype)


def _norm_matmul(x2, g, w, layer, wl, out_dtype, *, tm=1024, tn=1024):
    m, d = x2.shape
    n = w.shape[2]
    tm = min(tm, m)
    return pl.pallas_call(
        _norm_matmul_body,
        grid=(m // tm, n // tn),
        in_specs=[
            pl.BlockSpec((tm, d), lambda i, k: (i, 0)),
            pl.BlockSpec((None, 1, d), lambda i, k: (layer, 0, 0)),
            pl.BlockSpec((None, d, tn), lambda i, k: (wl, 0, k)),
        ],
        out_specs=pl.BlockSpec((tm, tn), lambda i, k: (i, k)),
        out_shape=jax.ShapeDtypeStruct((m, n), out_dtype),
        scratch_shapes=[pltpu.VMEM((tm, d), BF16)],
        compiler_params=pltpu.CompilerParams(
            dimension_semantics=("parallel", "arbitrary"), vmem_limit_bytes=VMEM_LIMIT),
        name="norm_matmul",
    )(x2, g, w)


def _hgrn_inproj_body(x_ref, g_ref, w_ref, lb_ref, q_ref, kk_ref, logf_ref, v_ref, gate_ref, xn_ref, *, per):
    k = pl.program_id(1)

    @pl.when(k == 0)
    def _():
        xn_ref[...] = (_rms_scale(x_ref[...]) * g_ref[...]).astype(BF16)

    def pre():
        return jnp.dot(xn_ref[...], w_ref[...], preferred_element_type=F32)

    @pl.when(k < per)
    def _():
        z = pre()
        q_ref[...] = (z * _sigmoid(z)).astype(q_ref.dtype)

    @pl.when((k >= per) & (k < 2 * per))
    def _():
        fz = pre()
        kk = (1.0 - lb_ref[...]) * (0.5 - 0.5 * jnp.tanh(0.5 * fz))
        kk_ref[...] = kk.astype(kk_ref.dtype)
        logf_ref[...] = jnp.log(jnp.maximum(1.0 - kk, F32_TINY))

    @pl.when((k >= 2 * per) & (k < 3 * per))
    def _():
        v_ref[...] = pre().astype(v_ref.dtype)

    @pl.when(k >= 3 * per)
    def _():
        z = pre()
        gate_ref[...] = (z * _sigmoid(z)).astype(gate_ref.dtype)


def _hgrn_inproj(x2, g, w, lb, layer, wl, *, tm=1024, tn=1024):
    m, d = x2.shape
    tm = min(tm, m)
    width = w.shape[2] // 4
    per = width // tn

    def slab(n):
        return pl.BlockSpec((tm, tn), lambda i, k: (i, jnp.clip(k - n * per, 0, per - 1)))

    bf = jax.ShapeDtypeStruct((m, width), BF16)
    return pl.pallas_call(
        functools.partial(_hgrn_inproj_body, per=per),
        grid=(m // tm, 4 * per),
        in_specs=[
            pl.BlockSpec((tm, d), lambda i, k: (i, 0)),
            pl.BlockSpec((None, 1, d), lambda i, k: (layer, 0, 0)),
            pl.BlockSpec((None, d, tn), lambda i, k: (wl, 0, k)),
            pl.BlockSpec((1, tn), lambda i, k: (0, jnp.clip(k - per, 0, per - 1))),
        ],
        out_specs=[slab(0), slab(1), slab(1), slab(2), slab(3)],
        out_shape=[bf, bf, jax.ShapeDtypeStruct((m, width), F32), bf, bf],
        scratch_shapes=[pltpu.VMEM((tm, d), BF16)],
        compiler_params=pltpu.CompilerParams(
            dimension_semantics=("parallel", "arbitrary"), vmem_limit_bytes=VMEM_LIMIT),
        name="hgrn_inproj",
    )(x2, g, w, lb)


def _outproj_body(*refs, n_in):
    x_ref, o_ref = refs[0], refs[-1]
    acc = x_ref[...]
    for y_ref, w_ref in zip(refs[1:1 + n_in], refs[1 + n_in:1 + 2 * n_in]):
        acc = acc + jnp.dot(y_ref[...], w_ref[...].astype(BF16), preferred_element_type=F32)
    o_ref[...] = acc


def _outproj(x2, ys, w, wl, *, tm=512):
    m, d = x2.shape
    tm = min(tm, m)
    in_specs = [pl.BlockSpec((tm, d), lambda i: (i, 0))]
    in_specs += [pl.BlockSpec((tm, y.shape[1]), lambda i: (i, 0)) for y in ys]
    row0 = 0
    for y in ys:
        rows = y.shape[1]
        assert row0 % rows == 0
        in_specs.append(pl.BlockSpec((None, rows, d), functools.partial(lambda i, rb: (wl, rb, 0), rb=row0 // rows)))
        row0 += rows
    return pl.pallas_call(
        functools.partial(_outproj_body, n_in=len(ys)),
        grid=(m // tm,),
        in_specs=in_specs,
        out_specs=pl.BlockSpec((tm, d), lambda i: (i, 0)),
        out_shape=jax.ShapeDtypeStruct((m, d), F32),
        compiler_params=pltpu.CompilerParams(
            dimension_semantics=("parallel",), vmem_limit_bytes=VMEM_LIMIT),
        name="outproj",
    )(x2, *ys, *([w] * len(ys)))


def _cast_job_specs(jobs, n_steps, step_index):
    in_specs, out_specs, out_shapes, args = [], [], [], []
    for arr, row0, nrows in jobs:
        rows, cols = nrows // n_steps, arr.shape[1]
        assert nrows % n_steps == 0 and rows % 16 == 0 and row0 % rows == 0
        in_specs.append(pl.BlockSpec((rows, cols), functools.partial(
            lambda *g, first: (step_index(*g) + first, 0), first=row0 // rows)))
        out_specs.append(pl.BlockSpec((rows, cols), lambda *g: (step_index(*g), 0)))
        out_shapes.append(jax.ShapeDtypeStruct((nrows, cols), BF16))
        args.append(arr)
    return in_specs, out_specs, out_shapes, args


def _run_cast_jobs(cast_in, cast_out):
    for src, dst in zip(cast_in, cast_out):
        dst[...] = src[...].astype(dst.dtype)


def _conv_body(*refs, ts, n_cast):
    val_ref, gate_ref, hval_ref, hgate_ref, w_ref, b_ref, lg_ref, lb_ref = refs[:8]
    cast_in, refs = refs[8:8 + n_cast], refs[8 + n_cast:]
    o_ref, cast_out = refs[0], refs[1:1 + n_cast]
    buf_ref, sh_ref, y_ref = refs[1 + n_cast:]
    _run_cast_jobs(cast_in, cast_out)
    ah = hval_ref[0].astype(F32) * _sigmoid(hgate_ref[0].astype(F32))
    buf_ref[0:CONV_HALO, :] = jnp.where(pl.program_id(1) > 0, ah, 0.0)
    buf_ref[CONV_HALO:, :] = val_ref[0].astype(F32) * _sigmoid(gate_ref[0].astype(F32))
    span = ts + CONV_HALO - SUBLANES
    for r in range(1, SUBLANES):
        sh_ref[r - 1, :, :] = buf_ref[r:r + span, :]
    first = CONV_HALO - (CONV_WIDTH - 1)
    rb = min(ts, 16)

    def taps(i, carry):
        r0 = pl.multiple_of(i * rb, rb)
        for cb in range(CONV_CH // LANES):
            cs = slice(cb * LANES, (cb + 1) * LANES)
            acc = jnp.broadcast_to(b_ref[:, cs], (rb, LANES))
            for k in range(CONV_WIDTH):
                off = first + k
                rows = pl.ds(r0 + (off - off % SUBLANES), rb)
                if off % SUBLANES == 0:
                    tap = buf_ref[rows, cs]
                else:
                    tap = sh_ref[off % SUBLANES - 1, rows, cs]
                acc = acc + w_ref[k:k + 1, cs] * tap
            y_ref[pl.ds(r0, rb), cs] = acc
        return carry

    lax.fori_loop(0, ts // rb, taps, 0)

    nb = min(ts, 128)

    def norm(i, carry):
        rs = pl.ds(pl.multiple_of(i * nb, nb), nb)
        y = y_ref[rs, :]
        mu = jnp.mean(y, axis=-1, keepdims=True)
        yc = y - mu
        var = jnp.mean(yc * yc, axis=-1, keepdims=True)
        z = yc * lax.rsqrt(var + EPS) * lg_ref[...] + lb_ref[...]
        o_ref[0, rs, :] = (z * _sigmoid(z)).astype(o_ref.dtype)
        return carry

    lax.fori_loop(0, ts // nb, norm, 0)


def _conv_module(u3, conv_w, conv_b, cn_g, cn_b, wl, cast_jobs=(), *, ts=256):
    b, s, _ = u3.shape
    ts = min(ts, s)
    hb = ts // CONV_HALO
    nt = s // ts
    c_in, c_out, c_shapes, c_args = _cast_job_specs(cast_jobs, b * nt, lambda i, t: i * nt + t)

    def halo_idx(col):
        return lambda i, t: (i, jnp.maximum(t * hb - 1, 0), col)

    vec = pl.BlockSpec((None, 1, CONV_CH), lambda i, t: (wl, 0, 0))
    outs = pl.pallas_call(
        functools.partial(_conv_body, ts=ts, n_cast=len(c_args)),
        grid=(b, nt),
        in_specs=[
            pl.BlockSpec((1, ts, CONV_CH), lambda i, t: (i, t, 0)),
            pl.BlockSpec((1, ts, CONV_CH), lambda i, t: (i, t, 1)),
            pl.BlockSpec((1, CONV_HALO, CONV_CH), halo_idx(0)),
            pl.BlockSpec((1, CONV_HALO, CONV_CH), halo_idx(1)),
            pl.BlockSpec((None, CONV_WIDTH, CONV_CH), lambda i, t: (wl, 0, 0)),
            vec, vec, vec,
        ] + c_in,
        out_specs=[pl.BlockSpec((1, ts, CONV_CH), lambda i, t: (i, t, 0))] + c_out,
        out_shape=[jax.ShapeDtypeStruct((b, s, CONV_CH), BF16)] + c_shapes,
        scratch_shapes=[pltpu.VMEM((ts + CONV_HALO, CONV_CH), F32),
                        pltpu.VMEM((SUBLANES - 1, ts + CONV_HALO - SUBLANES, CONV_CH), F32),
                        pltpu.VMEM((ts, CONV_CH), F32)],
        compiler_params=pltpu.CompilerParams(
            dimension_semantics=("parallel", "parallel"), vmem_limit_bytes=VMEM_LIMIT),
        name="conv_module",
    )(u3, u3, u3, u3, conv_w, conv_b, cn_g, cn_b, *c_args)
    return outs[0], list(outs[1:])


def _attn_body(*refs, n_cast):
    q_ref, k_ref, v_ref, cos_ref, sin_ref, qg_ref, kg_ref = refs[:7]
    cast_in, refs = refs[7:7 + n_cast], refs[7 + n_cast:]
    o_ref, cast_out = refs[0], refs[1:1 + n_cast]
    qn_ref, kn_ref, vn_ref, qc_ref, kc_ref, vc_ref, acc_n, m_n, l_n, acc_c, m_c, l_c = refs[1 + n_cast:]
    _run_cast_jobs(cast_in, cast_out)
    g = pl.program_id(2)
    grp = ATT_GROUP
    assert tuple(d for _, d in DIL_PATTERNS) == (1, 4, 16) and all(w // d == Q_BLOCK for w, d in DIL_PATTERNS)
    ncls = 4
    qcls = grp // ncls
    kcls = 2 * grp // ncls

    @pl.when(g == 0)
    def _():
        kn_ref[0:grp, :] = jnp.zeros((grp, HEAD_DIM), F32)
        vn_ref[0:grp, :] = jnp.zeros((grp, HEAD_DIM), F32)

    @pl.when(g > 0)
    def _():
        kn_ref[0:grp, :] = kn_ref[grp:2 * grp, :]
        vn_ref[0:grp, :] = vn_ref[grp:2 * grp, :]

    scale = HEAD_DIM ** -0.5
    prep_rows = 512

    def prep(i, carry):
        r0 = pl.multiple_of(i * prep_rows, prep_rows)
        rs = pl.ds(r0, prep_rows)
        cur = pl.ds(grp + r0, prep_rows)
        cos = cos_ref[rs, :]
        sin = sin_ref[rs, :]

        def rope(ref, gain_ref):
            y = _rms_scale(ref[0, rs, :].astype(F32)) * gain_ref[...]
            return y * cos + pltpu.roll(y, HEAD_DIM // 2, 1) * sin

        qn_ref[rs, :] = rope(q_ref, qg_ref) * (scale * LOG2E)
        kn_ref[cur, :] = rope(k_ref, kg_ref)
        vn_ref[cur, :] = v_ref[0, rs, :].astype(F32)
        return carry

    lax.fori_loop(0, grp // prep_rows, prep, 0)

    for c in range(ncls):
        qc_ref[c * qcls:(c + 1) * qcls, :] = qn_ref[pl.ds(c, qcls, stride=ncls), :]
        kc_ref[c * kcls:(c + 1) * kcls, :] = kn_ref[pl.ds(c, kcls, stride=ncls), :]
        vc_ref[c * kcls:(c + 1) * kcls, :] = vn_ref[pl.ds(c, kcls, stride=ncls), :]

    row = lax.broadcasted_iota(jnp.int32, (Q_BLOCK, 2 * Q_BLOCK), 0)
    col = lax.broadcasted_iota(jnp.int32, (Q_BLOCK, 2 * Q_BLOCK), 1)
    key_rank = jnp.where(col < Q_BLOCK, jnp.where(col >= row, 0, 1), jnp.where(col - Q_BLOCK <= row, -1, 1))

    def aligned(start, size):
        return pl.ds(pl.multiple_of(start, Q_BLOCK), size)

    def unit_dil1(u):
        qsl = aligned(u * Q_BLOCK, Q_BLOCK)
        return qsl, aligned(grp + (u - 1) * Q_BLOCK, 2 * Q_BLOCK), None, qsl, u > 0

    def unit_dil4(u):
        c, nb = u % ncls, u // ncls
        qsl = aligned(c * qcls + nb * Q_BLOCK, Q_BLOCK)
        ksl = aligned(c * kcls + grp // ncls + (nb - 1) * Q_BLOCK, 2 * Q_BLOCK)
        return qsl, ksl, pl.ds(nb * (Q_BLOCK * ncls) + c, Q_BLOCK, stride=ncls), qsl, nb > 0

    def unit_dil16(u):
        c, a = u % ncls, u // ncls
        qsl = pl.ds(c * qcls + a, Q_BLOCK, stride=ncls)
        return qsl, pl.ds(c * kcls + a, 2 * Q_BLOCK, stride=ncls), qsl, qsl, False

    def branch(unit, qkv, st_in, st_out, first, last):
        qb, kb, vb = qkv

        def units(it, carry):
            sl = [unit(it * ATT_UNROLL + t) for t in range(ATT_UNROLL)]
            scores = []
            for qsl, ksl, _, _, has_prev in sl:
                s = _dot_nt(qb[qsl, :].astype(BF16), kb[ksl, :].astype(BF16))
                rank_limit = jnp.where((g > 0) | has_prev, 1, 0)
                scores.append(jnp.where(key_rank < rank_limit, s, -jnp.inf))
            probs, m_news, alphas, l_news = [], [], [], []
            for (_, _, isl, _, _), s in zip(sl, scores):
                mx = jnp.max(s, axis=-1, keepdims=True)
                if first:
                    m_new = jnp.broadcast_to(mx, (Q_BLOCK, LANES))
                else:
                    m_old = st_in[1][isl, :]
                    m_new = jnp.maximum(m_old, mx)
                    alphas.append(jnp.exp2(m_old - m_new))
                p = jnp.exp2(s - jnp.concatenate([m_new, m_new], axis=1))
                l_news.append(jnp.broadcast_to(jnp.sum(p, axis=-1, keepdims=True), (Q_BLOCK, LANES)))
                probs.append(p.astype(BF16))
                m_news.append(m_new)
            accs = [jnp.dot(p, vb[ksl, :].astype(BF16), preferred_element_type=F32)
                    for (_, ksl, _, _, _), p in zip(sl, probs)]
            if not first:
                l_news = [a * st_in[2][isl, :] + ln for (_, _, isl, _, _), a, ln in zip(sl, alphas, l_news)]
                accs = [a * st_in[0][isl, :] + ac for (_, _, isl, _, _), a, ac in zip(sl, alphas, accs)]
            for (_, _, _, osl, _), ac, mn, ln in zip(sl, accs, m_news, l_news):
                if last:
                    st_out[0][osl, :] = ac / ln
                else:
                    st_out[0][osl, :] = ac
                    st_out[1][osl, :] = mn
                    st_out[2][osl, :] = ln
            return carry

        lax.fori_loop(0, grp // Q_BLOCK // ATT_UNROLL, units, 0)

    natural, classes = (acc_n, m_n, l_n), (acc_c, m_c, l_c)
    branch(unit_dil1, (qn_ref, kn_ref, vn_ref), None, natural, True, False)
    branch(unit_dil4, (qc_ref, kc_ref, vc_ref), natural, classes, False, False)
    branch(unit_dil16, (qc_ref, kc_ref, vc_ref), classes, classes, False, True)
    for c in range(ncls):
        acc_n[pl.ds(c, qcls, stride=ncls), :] = acc_c[c * qcls:(c + 1) * qcls, :]
    o_ref[0] = acc_n[...].astype(o_ref.dtype)


def _fused_attention(u3, cos, sin, qn_g, kn_g, wl=0, cast_jobs=()):
    b, s, width = u3.shape
    grp = ATT_GROUP
    assert s % grp == 0
    ng = s // grp
    c_in, c_out, c_shapes, c_args = _cast_job_specs(
        cast_jobs, b * ATT_HEADS * ng, lambda i, h, t: (i * ATT_HEADS + h) * ng + t)
    q_col = (width - 3 * ATT_DIM) // HEAD_DIM

    def col(slab):
        return pl.BlockSpec((1, grp, HEAD_DIM), lambda i, h, t: (i, t, q_col + slab * ATT_HEADS + h))

    tab = pl.BlockSpec((grp, HEAD_DIM), lambda i, h, t: (t, 0))
    vec = pl.BlockSpec((None, 1, HEAD_DIM), lambda i, h, t: (wl, 0, 0))
    scr = lambda rows: pltpu.VMEM((rows, HEAD_DIM), F32)
    outs = pl.pallas_call(
        functools.partial(_attn_body, n_cast=len(c_args)),
        grid=(b, ATT_HEADS, ng),
        in_specs=[col(0), col(1), col(2), tab, tab, vec, vec] + c_in,
        out_specs=[pl.BlockSpec((1, grp, HEAD_DIM), lambda i, h, t: (i, t, h))] + c_out,
        out_shape=[jax.ShapeDtypeStruct((b, s, ATT_DIM), BF16)] + c_shapes,
        scratch_shapes=[scr(grp), scr(2 * grp), scr(2 * grp)] * 2 + [scr(grp)] * 6,
        compiler_params=pltpu.CompilerParams(
            dimension_semantics=("parallel", "parallel", "arbitrary"), vmem_limit_bytes=VMEM_LIMIT),
        name="dilated_attn",
    )(u3, u3, u3, cos, sin, qn_g, kn_g, *c_args)
    return outs[0], list(outs[1:])


def _hgrn_body(q_ref, kk_ref, logf_ref, v_ref, gate_ref, gn_ref, o_ref, st_ref, b_ref, cum_ref, mask_ref,
               sgn_ref, *, rows, chunk):
    sub = chunk // 2
    levels = [sub >> (i + 1) for i in range(sub.bit_length() - 1)]
    small = [h for h in levels if 2 * h < SUBLANES]

    @pl.when(pl.program_id(2) == 0)
    def _():
        st_ref[...] = jnp.zeros_like(st_ref)

    ri = lax.broadcasted_iota(jnp.int32, (chunk, chunk), 0)
    ci = lax.broadcasted_iota(jnp.int32, (chunk, chunk), 1)
    cum_ref[...] = (ci <= ri).astype(BF16)
    row = lax.broadcasted_iota(jnp.int32, (chunk, HEAD_DIM), 0)
    rs_ = lax.broadcasted_iota(jnp.int32, (sub, sub), 0)
    cs_ = lax.broadcasted_iota(jnp.int32, (sub, sub), 1)
    for n, half in enumerate(levels):
        blk = 2 * half
        keep = ((rs_ // blk) == (cs_ // blk)) & ((rs_ & (blk - 1)) >= half) & ((cs_ & (blk - 1)) < half)
        mask_ref[n] = keep.astype(F32)
        sgn_ref[n] = jnp.where((row & (blk - 1)) >= half, 1.0, -1.0)

    heads = range(HGRN_HEADS_PER_STEP)

    def one_chunk(c, carry):
        rs = pl.ds(pl.multiple_of(c * chunk, chunk), chunk)
        hsl = [slice(hh * HEAD_DIM, (hh + 1) * HEAD_DIM) for hh in heads]
        q = [q_ref[0, rs, hs].astype(F32) for hs in hsl]
        kk = [kk_ref[0, rs, hs].astype(F32) for hs in hsl]
        v = [v_ref[0, rs, hs] for hs in hsl]
        bcum = []
        for hs in hsl:
            logf = logf_ref[0, rs, hs]
            hi = logf.astype(BF16)
            rem = logf - hi.astype(F32)
            mid = rem.astype(BF16)
            lo = (rem - mid.astype(F32)).astype(BF16)
            parts = jnp.dot(cum_ref[...], jnp.concatenate([hi, mid, lo], axis=1), preferred_element_type=F32)
            bcum.append((parts[:, :HEAD_DIM] + parts[:, HEAD_DIM:2 * HEAD_DIM] + parts[:, 2 * HEAD_DIM:]) * LOG2E)
        for hh in heads:
            b_ref[hh] = bcum[hh]

        att = [[jnp.zeros((sub, sub), F32), jnp.zeros((sub, sub), F32)] for _ in heads]
        for n, half in enumerate(levels):
            blk = 2 * half
            for hh in heads:
                if half in small:
                    pos = row & (blk - 1)
                    b_mid = bcum[hh]
                    for p in range(blk):
                        off = half - 1 - p
                        if off != 0:
                            b_mid = jnp.where(pos == p, pltpu.roll(bcum[hh], (-off) % chunk, 0), b_mid)
                else:
                    pieces = [jnp.broadcast_to(b_ref[hh, s0 + half - 1:s0 + half, :], (blk, HEAD_DIM))
                              for s0 in range(0, chunk, blk)]
                    b_mid = jnp.concatenate(pieces, axis=0)
                dec = jnp.exp2((bcum[hh] - b_mid) * sgn_ref[n])
                gq = (q[hh] * dec).astype(BF16)
                hk = (kk[hh] * dec).astype(BF16)
                for d in range(2):
                    blk_rows = slice(d * sub, (d + 1) * sub)
                    att[hh][d] = att[hh][d] + _dot_nt(gq[blk_rows], hk[blk_rows]) * mask_ref[n]
        cross = []
        for hh in heads:
            b_top = b_ref[hh, sub - 1:sub, :]
            cross.append(_dot_nt((q[hh][sub:] * jnp.exp2(bcum[hh][sub:] - b_top)).astype(BF16),
                                 (kk[hh][:sub] * jnp.exp2(b_top - bcum[hh][:sub])).astype(BF16)))

        outs = []
        for hh in heads:
            st = st_ref[hh]
            o_lo = jnp.dot(att[hh][0].astype(BF16), v[hh][:sub], preferred_element_type=F32)
            o_hi = (jnp.dot(att[hh][1].astype(BF16), v[hh][sub:], preferred_element_type=F32)
                    + jnp.dot(cross[hh].astype(BF16), v[hh][:sub], preferred_element_type=F32))
            o = jnp.concatenate([o_lo, o_hi], axis=0)
            vf = v[hh].astype(F32)
            o = o + jnp.sum(q[hh] * kk[hh], axis=-1, keepdims=True) * vf
            o = o + _dot_nt((q[hh] * jnp.exp2(bcum[hh])).astype(BF16), st.astype(BF16))
            b_last = b_ref[hh, chunk - 1:chunk, :]
            k_dec = (kk[hh] * jnp.exp2(b_last - bcum[hh])).astype(BF16)
            st_ref[hh] = st * jnp.exp2(b_last) + jnp.dot(vf.T.astype(BF16), k_dec, preferred_element_type=F32)
            outs.append(o)
        for hh, hs in zip(heads, hsl):
            on = _rms_scale(outs[hh]) * gn_ref[:, hs] * gate_ref[0, rs, hs].astype(F32)
            o_ref[0, rs, hs] = on.astype(o_ref.dtype)
        return carry

    lax.fori_loop(0, rows // chunk, one_chunk, 0)


def _hgrn(q3, kk3, logf3, v3, gate3, gn_g, wl, *, rows=1024, chunk=256):
    b, s, _ = logf3.shape
    rows = min(rows, s)
    hp = HGRN_HEADS_PER_STEP
    wide = hp * HEAD_DIM
    sub = chunk // 2
    n_levels = sub.bit_length() - 1
    blk = pl.BlockSpec((1, rows, wide), lambda i, h, r: (i, r, h))
    return pl.pallas_call(
        functools.partial(_hgrn_body, rows=rows, chunk=chunk),
        grid=(b, HGRN_WIDTH // wide, s // rows),
        in_specs=[blk, blk, blk, blk, blk,
                  pl.BlockSpec((None, 1, wide), lambda i, h, r: (wl, 0, h))],
        out_specs=blk,
        out_shape=jax.ShapeDtypeStruct((b, s, HGRN_WIDTH), BF16),
        scratch_shapes=[pltpu.VMEM((hp, HEAD_DIM, HEAD_DIM), F32),
                        pltpu.VMEM((hp, chunk, HEAD_DIM), F32),
                        pltpu.VMEM((chunk, chunk), BF16),
                        pltpu.VMEM((n_levels, sub, sub), F32),
                        pltpu.VMEM((n_levels, chunk, HEAD_DIM), F32)],
        compiler_params=pltpu.CompilerParams(
            dimension_semantics=("parallel", "parallel", "arbitrary"), vmem_limit_bytes=VMEM_LIMIT),
        name="hgrn2",
    )(q3, kk3, logf3, v3, gate3, gn_g)


def _rope_tables(s):
    half = HEAD_DIM // 2
    inv = jnp.exp(-math.log(ROPE_THETA) * jnp.arange(half, dtype=F32) / half)
    ang = jnp.arange(s, dtype=jnp.int32).astype(F32)[:, None] * inv[None, :]
    cos, sin = jnp.cos(ang), jnp.sin(ang)
    return jnp.concatenate([cos, cos], axis=-1), jnp.concatenate([-sin, sin], axis=-1)


def _conv_attn_mixer(x2, b, s, norm_mix, layer, j, w_in, conv_w, conv_b, cn_g, cn_b, qn_g, kn_g, w_out, cos, sin,
                     conv_casts=(), attn_casts=()):
    m = b * s
    n_in = w_in.shape[2]
    u = _norm_matmul(x2, norm_mix, w_in, layer, j, BF16)
    u3 = u.reshape(b, s, n_in)
    a, conv_cast = _conv_module(u3, conv_w, conv_b, cn_g, cn_b, j, conv_casts)
    o, attn_cast = _fused_attention(u3, cos, sin, qn_g, kn_g, j, attn_casts)
    x2 = _outproj(x2, [a.reshape(m, CONV_CH), o.reshape(m, ATT_DIM)], w_out, j)
    return x2, conv_cast, attn_cast


def _hgrn2_mixer(x2, b, s, norm_mix, layer, j, w_in, lb, gn_g, w_out):
    parts = _hgrn_inproj(x2, norm_mix, w_in, lb, layer, j)
    og = _hgrn(*[p.reshape(b, s, HGRN_WIDTH) for p in parts], gn_g, j)
    return _outproj(x2, [og.reshape(b * s, HGRN_WIDTH)], w_out, j)


def kernel(x, norm_ffn1, ffn1_wg, ffn1_wu, ffn1_wd, norm_mix, norm_ffn2, ffn2_wg, ffn2_wu, ffn2_wd, ev_w_in, ev_conv_w, ev_conv_b, ev_cn_g, ev_cn_b, ev_qn_g, ev_kn_g, ev_w_out, od_w_in, od_lb_logits, od_gn_g, od_w_out):
    b, s, d = x.shape
    depth = norm_ffn1.shape[0]
    cos, sin = _rope_tables(s)
    p = jax.nn.softmax(od_lb_logits.astype(F32), axis=0)
    lower_bounds = jnp.cumsum(p, axis=0) - p[0:1]
    row3 = lambda a: a[:, None, :]
    rows2 = lambda w: w.reshape(-1, w.shape[-1])
    g1, gm, g2 = row3(norm_ffn1), row3(norm_mix), row3(norm_ffn2)
    f = ffn1_wg.shape[2]
    gu1 = {0: (ffn1_wg[0:1].astype(BF16), ffn1_wu[0:1].astype(BF16), 0)}
    later = (depth - 1) * d
    conv_casts = [(rows2(ffn1_wg), d, later), (rows2(ffn1_wu), d, later),
                  (rows2(od_w_in), 0, od_w_in.shape[0] * d)] if depth > 1 else []
    attn_casts = [(rows2(ffn2_wg), 0, depth * d), (rows2(ffn2_wu), 0, depth * d)]
    gu2 = od_in = None
    x2 = x.reshape(b * s, d)
    for l in range(depth):
        j = l // 2
        wg1, wu1, idx1 = gu1[l]
        x2 = _ffn(x2, g1, wg1, wu1, ffn1_wd, l, idx1)
        if l % 2 == 0:
            first = l == 0
            x2, conv_cast, attn_cast = _conv_attn_mixer(
                x2, b, s, gm, l, j, ev_w_in.astype(BF16), ev_conv_w, row3(ev_conv_b), row3(ev_cn_g), row3(ev_cn_b),
                row3(ev_qn_g), row3(ev_kn_g), ev_w_out, cos, sin,
                conv_casts if first else (), attn_casts if first else ())
            if first:
                gu2 = [w.reshape(depth, d, f) for w in attn_cast]
                if conv_cast:
                    wg_l, wu_l, od_in = conv_cast
                    for ll in range(1, depth):
                        gu1[ll] = (wg_l.reshape(depth - 1, d, f), wu_l.reshape(depth - 1, d, f), ll - 1)
                    od_in = od_in.reshape(od_w_in.shape)
        else:
            x2 = _hgrn2_mixer(x2, b, s, gm, l, j, od_in, lower_bounds[l][None], row3(od_gn_g), od_w_out)
        x2 = _ffn(x2, g2, gu2[0], gu2[1], ffn2_wd, l, l)
    return x2.reshape(b, s, d)
```

```python
import functools
import math

import jax
import jax.numpy as jnp
from jax import lax
from jax.experimental import pallas as pl
from jax.experimental.pallas import tpu as pltpu

F32 = jnp.float32
BF16 = jnp.bfloat16
EPS = 1e-6
LOG2E = 1.4426950408889634
F32_TINY = 1.1754944e-38

LANES = 128
SUBLANES = 8
HEAD_DIM = 128
CONV_CH = 1024
CONV_WIDTH = 31
CONV_HALO = 32
ATT_HEADS = 8
ATT_DIM = ATT_HEADS * HEAD_DIM
DIL_PATTERNS = ((128, 1), (512, 4), (2048, 16))
Q_BLOCK = 128
ATT_GROUP = Q_BLOCK * max(d for _, d in DIL_PATTERNS)
ROPE_THETA = 10000.0
HGRN_HEADS = 16
HGRN_WIDTH = HGRN_HEADS * HEAD_DIM
HGRN_HEADS_PER_STEP = 4
FFN_ROW_BLOCKS = 2
ATT_UNROLL = 8

VMEM_LIMIT = 60 * 1024 * 1024


def _sigmoid(x):
    return 0.5 * jnp.tanh(0.5 * x) + 0.5


def _rms_scale(x):
    return x * lax.rsqrt(jnp.mean(x * x, axis=-1, keepdims=True) + EPS)


def _dot_nt(a, b):
    return lax.dot_general(a, b, (((1,), (1,)), ((), ())), preferred_element_type=F32)


def _ffn_body(x_ref, g_ref, wg_ref, wu_ref, wd_ref, o_ref, xn_ref):
    def hidden(xn):
        hg = jnp.dot(xn, wg_ref[...], preferred_element_type=F32)
        hu = jnp.dot(xn, wu_ref[...], preferred_element_type=F32)
        return ((hg * _sigmoid(hg)) * hu * 0.5).astype(BF16)

    @pl.when(pl.program_id(1) == 0)
    def _():
        wd = wd_ref[...].astype(BF16)
        tm = x_ref.shape[0]
        rb = tm // FFN_ROW_BLOCKS
        for r0 in range(0, tm, rb):
            x = x_ref[r0:r0 + rb, :]
            xn = (_rms_scale(x) * g_ref[...]).astype(BF16)
            xn_ref[r0:r0 + rb, :] = xn
            o_ref[r0:r0 + rb, :] = x + jnp.dot(hidden(xn), wd, preferred_element_type=F32)

    @pl.when(pl.program_id(1) > 0)
    def _():
        o_ref[...] += jnp.dot(hidden(xn_ref[...]), wd_ref[...].astype(BF16), preferred_element_type=F32)


def _ffn(x2, g, wg, wu, wd, layer, gu_layer, *, tm=1024, tf=512):
    m, d = x2.shape
    f = wg.shape[2]
    tm = min(tm, m)
    return pl.pallas_call(
        _ffn_body,
        grid=(m // tm, f // tf),
        in_specs=[
            pl.BlockSpec((tm, d), lambda i, k: (i, 0)),
            pl.BlockSpec((None, 1, d), lambda i, k: (layer, 0, 0)),
            pl.BlockSpec((None, d, tf), lambda i, k: (gu_layer, 0, k)),
            pl.BlockSpec((None, d, tf), lambda i, k: (gu_layer, 0, k)),
            pl.BlockSpec((None, tf, d), lambda i, k: (layer, k, 0)),
        ],
        out_specs=pl.BlockSpec((tm, d), lambda i, k: (i, 0)),
        out_shape=jax.ShapeDtypeStruct((m, d), F32),
        scratch_shapes=[pltpu.VMEM((tm, d), BF16)],
        compiler_params=pltpu.CompilerParams(
            dimension_semantics=("parallel", "arbitrary"), vmem_limit_bytes=VMEM_LIMIT),
        name="ffn",
    )(x2, g, wg, wu, wd)


def _norm_matmul_body(x_ref, g_ref, w_ref, o_ref, xn_ref):
    @pl.when(pl.program_id(1) == 0)
    def _():
        xn_ref[...] = (_rms_scale(x_ref[...]) * g_ref[...]).astype(BF16)

    o_ref[...] = jnp.dot(xn_ref[...], w_ref[...], preferred_element_type=F32).astype(o_ref.dtype)


def _norm_matmul(x2, g, w, layer, wl, out_dtype, *, tm=1024, tn=1024):
    m, d = x2.shape
    n = w.shape[2]
    tm = min(tm, m)
    return pl.pallas_call(
        _norm_matmul_body,
        grid=(m // tm, n // tn),
        in_specs=[
            pl.BlockSpec((tm, d), lambda i, k: (i, 0)),
            pl.BlockSpec((None, 1, d), lambda i, k: (layer, 0, 0)),
            pl.BlockSpec((None, d, tn), lambda i, k: (wl, 0, k)),
        ],
        out_specs=pl.BlockSpec((tm, tn), lambda i, k: (i, k)),
        out_shape=jax.ShapeDtypeStruct((m, n), out_dtype),
        scratch_shapes=[pltpu.VMEM((tm, d), BF16)],
        compiler_params=pltpu.CompilerParams(
            dimension_semantics=("parallel", "arbitrary"), vmem_limit_bytes=VMEM_LIMIT),
        name="norm_matmul",
    )(x2, g, w)


def _hgrn_inproj_body(x_ref, g_ref, w_ref, lb_ref, q_ref, kk_ref, logf_ref, v_ref, gate_ref, xn_ref, *, per):
    k = pl.program_id(1)

    @pl.when(k == 0)
    def _():
        xn_ref[...] = (_rms_scale(x_ref[...]) * g_ref[...]).astype(BF16)

    def pre():
        return jnp.dot(xn_ref[...], w_ref[...], preferred_element_type=F32)

    @pl.when(k < per)
    def _():
        z = pre()
        q_ref[...] = (z * _sigmoid(z)).astype(q_ref.dtype)

    @pl.when((k >= per) & (k < 2 * per))
    def _():
        fz = pre()
        kk = (1.0 - lb_ref[...]) * (0.5 - 0.5 * jnp.tanh(0.5 * fz))
        kk_ref[...] = kk.astype(kk_ref.dtype)
        logf_ref[...] = jnp.log(jnp.maximum(1.0 - kk, F32_TINY))

    @pl.when((k >= 2 * per) & (k < 3 * per))
    def _():
        v_ref[...] = pre().astype(v_ref.dtype)

    @pl.when(k >= 3 * per)
    def _():
        z = pre()
        gate_ref[...] = (z * _sigmoid(z)).astype(gate_ref.dtype)


def _hgrn_inproj(x2, g, w, lb, layer, wl, *, tm=1024, tn=1024):
    m, d = x2.shape
    tm = min(tm, m)
    width = w.shape[2] // 4
    per = width // tn

    def slab(n):
        return pl.BlockSpec((tm, tn), lambda i, k: (i, jnp.clip(k - n * per, 0, per - 1)))

    bf = jax.ShapeDtypeStruct((m, width), BF16)
    return pl.pallas_call(
        functools.partial(_hgrn_inproj_body, per=per),
        grid=(m // tm, 4 * per),
        in_specs=[
            pl.BlockSpec((tm, d), lambda i, k: (i, 0)),
            pl.BlockSpec((None, 1, d), lambda i, k: (layer, 0, 0)),
            pl.BlockSpec((None, d, tn), lambda i, k: (wl, 0, k)),
            pl.BlockSpec((1, tn), lambda i, k: (0, jnp.clip(k - per, 0, per - 1))),
        ],
        out_specs=[slab(0), slab(1), slab(1), slab(2), slab(3)],
        out_shape=[bf, bf, jax.ShapeDtypeStruct((m, width), F32), bf, bf],
        scratch_shapes=[pltpu.VMEM((tm, d), BF16)],
        compiler_params=pltpu.CompilerParams(
            dimension_semantics=("parallel", "arbitrary"), vmem_limit_bytes=VMEM_LIMIT),
        name="hgrn_inproj",
    )(x2, g, w, lb)


def _outproj_body(*refs, n_in):
    x_ref, o_ref = refs[0], refs[-1]
    acc = x_ref[...]
    for y_ref, w_ref in zip(refs[1:1 + n_in], refs[1 + n_in:1 + 2 * n_in]):
        acc = acc + jnp.dot(y_ref[...], w_ref[...].astype(BF16), preferred_element_type=F32)
    o_ref[...] = acc


def _outproj(x2, ys, w, wl, *, tm=512):
    m, d = x2.shape
    tm = min(tm, m)
    in_specs = [pl.BlockSpec((tm, d), lambda i: (i, 0))]
    in_specs += [pl.BlockSpec((tm, y.shape[1]), lambda i: (i, 0)) for y in ys]
    row0 = 0
    for y in ys:
        rows = y.shape[1]
        assert row0 % rows == 0
        in_specs.append(pl.BlockSpec((None, rows, d), functools.partial(lambda i, rb: (wl, rb, 0), rb=row0 // rows)))
        row0 += rows
    return pl.pallas_call(
        functools.partial(_outproj_body, n_in=len(ys)),
        grid=(m // tm,),
        in_specs=in_specs,
        out_specs=pl.BlockSpec((tm, d), lambda i: (i, 0)),
        out_shape=jax.ShapeDtypeStruct((m, d), F32),
        compiler_params=pltpu.CompilerParams(
            dimension_semantics=("parallel",), vmem_limit_bytes=VMEM_LIMIT),
        name="outproj",
    )(x2, *ys, *([w] * len(ys)))


def _cast_job_specs(jobs, n_steps, step_index):
    in_specs, out_specs, out_shapes, args = [], [], [], []
    for arr, row0, nrows in jobs:
        rows, cols = nrows // n_steps, arr.shape[1]
        assert nrows % n_steps == 0 and rows % 16 == 0 and row0 % rows == 0
        in_specs.append(pl.BlockSpec((rows, cols), functools.partial(
            lambda *g, first: (step_index(*g) + first, 0), first=row0 // rows)))
        out_specs.append(pl.BlockSpec((rows, cols), lambda *g: (step_index(*g), 0)))
        out_shapes.append(jax.ShapeDtypeStruct((nrows, cols), BF16))
        args.append(arr)
    return in_specs, out_specs, out_shapes, args


def _run_cast_jobs(cast_in, cast_out):
    for src, dst in zip(cast_in, cast_out):
        dst[...] = src[...].astype(dst.dtype)


def _conv_body(*refs, ts, n_cast):
    val_ref, gate_ref, hval_ref, hgate_ref, w_ref, b_ref, lg_ref, lb_ref = refs[:8]
    cast_in, refs = refs[8:8 + n_cast], refs[8 + n_cast:]
    o_ref, cast_out = refs[0], refs[1:1 + n_cast]
    buf_ref, sh_ref, y_ref = refs[1 + n_cast:]
    _run_cast_jobs(cast_in, cast_out)
    ah = hval_ref[0].astype(F32) * _sigmoid(hgate_ref[0].astype(F32))
    buf_ref[0:CONV_HALO, :] = jnp.where(pl.program_id(1) > 0, ah, 0.0)
    buf_ref[CONV_HALO:, :] = val_ref[0].astype(F32) * _sigmoid(gate_ref[0].astype(F32))
    span = ts + CONV_HALO - SUBLANES
    for r in range(1, SUBLANES):
        sh_ref[r - 1, :, :] = buf_ref[r:r + span, :]
    first = CONV_HALO - (CONV_WIDTH - 1)
    rb = min(ts, 16)

    def taps(i, carry):
        r0 = pl.multiple_of(i * rb, rb)
        for cb in range(CONV_CH // LANES):
            cs = slice(cb * LANES, (cb + 1) * LANES)
            acc = jnp.broadcast_to(b_ref[:, cs], (rb, LANES))
            for k in range(CONV_WIDTH):
                off = first + k
                rows = pl.ds(r0 + (off - off % SUBLANES), rb)
                if off % SUBLANES == 0:
                    tap = buf_ref[rows, cs]
                else:
                    tap = sh_ref[off % SUBLANES - 1, rows, cs]
                acc = acc + w_ref[k:k + 1, cs] * tap
            y_ref[pl.ds(r0, rb), cs] = acc
        return carry

    lax.fori_loop(0, ts // rb, taps, 0)

    nb = min(ts, 128)

    def norm(i, carry):
        rs = pl.ds(pl.multiple_of(i * nb, nb), nb)
        y = y_ref[rs, :]
        mu = jnp.mean(y, axis=-1, keepdims=True)
        yc = y - mu
        var = jnp.mean(yc * yc, axis=-1, keepdims=True)
        z = yc * lax.rsqrt(var + EPS) * lg_ref[...] + lb_ref[...]
        o_ref[0, rs, :] = (z * _sigmoid(z)).astype(o_ref.dtype)
        return carry

    lax.fori_loop(0, ts // nb, norm, 0)


def _conv_module(u3, conv_w, conv_b, cn_g, cn_b, wl, cast_jobs=(), *, ts=512):
    b, s, _ = u3.shape
    ts = min(ts, s)
    hb = ts // CONV_HALO
    nt = s // ts
    c_in, c_out, c_shapes, c_args = _cast_job_specs(cast_jobs, b * nt, lambda i, t: i * nt + t)

    def halo_idx(col):
        return lambda i, t: (i, jnp.maximum(t * hb - 1, 0), col)

    vec = pl.BlockSpec((None, 1, CONV_CH), lambda i, t: (wl, 0, 0))
    outs = pl.pallas_call(
        functools.partial(_conv_body, ts=ts, n_cast=len(c_args)),
        grid=(b, nt),
        in_specs=[
            pl.BlockSpec((1, ts, CONV_CH), lambda i, t: (i, t, 0)),
            pl.BlockSpec((1, ts, CONV_CH), lambda i, t: (i, t, 1)),
            pl.BlockSpec((1, CONV_HALO, CONV_CH), halo_idx(0)),
            pl.BlockSpec((1, CONV_HALO, CONV_CH), halo_idx(1)),
            pl.BlockSpec((None, CONV_WIDTH, CONV_CH), lambda i, t: (wl, 0, 0)),
            vec, vec, vec,
        ] + c_in,
        out_specs=[pl.BlockSpec((1, ts, CONV_CH), lambda i, t: (i, t, 0))] + c_out,
        out_shape=[jax.ShapeDtypeStruct((b, s, CONV_CH), BF16)] + c_shapes,
        scratch_shapes=[pltpu.VMEM((ts + CONV_HALO, CONV_CH), F32),
                        pltpu.VMEM((SUBLANES - 1, ts + CONV_HALO - SUBLANES, CONV_CH), F32),
                        pltpu.VMEM((ts, CONV_CH), F32)],
        compiler_params=pltpu.CompilerParams(
            dimension_semantics=("parallel", "parallel"), vmem_limit_bytes=VMEM_LIMIT),
        name="conv_module",
    )(u3, u3, u3, u3, conv_w, conv_b, cn_g, cn_b, *c_args)
    return outs[0], list(outs[1:])


def _attn_body(*refs, n_cast):
    q_ref, k_ref, v_ref, cos_ref, sin_ref, qg_ref, kg_ref = refs[:7]
    cast_in, refs = refs[7:7 + n_cast], refs[7 + n_cast:]
    o_ref, cast_out = refs[0], refs[1:1 + n_cast]
    qn_ref, kn_ref, vn_ref, qc_ref, kc_ref, vc_ref, acc_n, m_n, l_n, acc_c, m_c, l_c = refs[1 + n_cast:]
    _run_cast_jobs(cast_in, cast_out)
    g = pl.program_id(2)
    grp = ATT_GROUP
    assert tuple(d for _, d in DIL_PATTERNS) == (1, 4, 16) and all(w // d == Q_BLOCK for w, d in DIL_PATTERNS)
    ncls = 4
    qcls = grp // ncls
    kcls = 2 * grp // ncls

    @pl.when(g == 0)
    def _():
        kn_ref[0:grp, :] = jnp.zeros((grp, HEAD_DIM), F32)
        vn_ref[0:grp, :] = jnp.zeros((grp, HEAD_DIM), F32)

    @pl.when(g > 0)
    def _():
        kn_ref[0:grp, :] = kn_ref[grp:2 * grp, :]
        vn_ref[0:grp, :] = vn_ref[grp:2 * grp, :]

    scale = HEAD_DIM ** -0.5
    prep_rows = 512

    def prep(i, carry):
        r0 = pl.multiple_of(i * prep_rows, prep_rows)
        rs = pl.ds(r0, prep_rows)
        cur = pl.ds(grp + r0, prep_rows)
        cos = cos_ref[rs, :]
        sin = sin_ref[rs, :]

        def rope(ref, gain_ref):
            y = _rms_scale(ref[0, rs, :].astype(F32)) * gain_ref[...]
            return y * cos + pltpu.roll(y, HEAD_DIM // 2, 1) * sin

        qn_ref[rs, :] = rope(q_ref, qg_ref) * (scale * LOG2E)
        kn_ref[cur, :] = rope(k_ref, kg_ref)
        vn_ref[cur, :] = v_ref[0, rs, :].astype(F32)
        return carry

    lax.fori_loop(0, grp // prep_rows, prep, 0)

    for c in range(ncls):
        qc_ref[c * qcls:(c + 1) * qcls, :] = qn_ref[pl.ds(c, qcls, stride=ncls), :]
        kc_ref[c * kcls:(c + 1) * kcls, :] = kn_ref[pl.ds(c, kcls, stride=ncls), :]
        vc_ref[c * kcls:(c + 1) * kcls, :] = vn_ref[pl.ds(c, kcls, stride=ncls), :]

    row = lax.broadcasted_iota(jnp.int32, (Q_BLOCK, 2 * Q_BLOCK), 0)
    col = lax.broadcasted_iota(jnp.int32, (Q_BLOCK, 2 * Q_BLOCK), 1)
    key_rank = jnp.where(col < Q_BLOCK, jnp.where(col >= row, 0, 1), jnp.where(col - Q_BLOCK <= row, -1, 1))

    def aligned(start, size):
        return pl.ds(pl.multiple_of(start, Q_BLOCK), size)

    def unit_dil1(u):
        qsl = aligned(u * Q_BLOCK, Q_BLOCK)
        return qsl, aligned(grp + (u - 1) * Q_BLOCK, 2 * Q_BLOCK), None, qsl, u > 0

    def unit_dil4(u):
        c, nb = u % ncls, u // ncls
        qsl = aligned(c * qcls + nb * Q_BLOCK, Q_BLOCK)
        ksl = aligned(c * kcls + grp // ncls + (nb - 1) * Q_BLOCK, 2 * Q_BLOCK)
        return qsl, ksl, pl.ds(nb * (Q_BLOCK * ncls) + c, Q_BLOCK, stride=ncls), qsl, nb > 0

    def unit_dil16(u):
        c, a = u % ncls, u // ncls
        qsl = pl.ds(c * qcls + a, Q_BLOCK, stride=ncls)
        return qsl, pl.ds(c * kcls + a, 2 * Q_BLOCK, stride=ncls), qsl, qsl, False

    def branch(unit, qkv, st_in, st_out, first, last):
        qb, kb, vb = qkv

        def units(it, carry):
            sl = [unit(it * ATT_UNROLL + t) for t in range(ATT_UNROLL)]
            scores = []
            for qsl, ksl, _, _, has_prev in sl:
                s = _dot_nt(qb[qsl, :].astype(BF16), kb[ksl, :].astype(BF16))
                rank_limit = jnp.where((g > 0) | has_prev, 1, 0)
                scores.append(jnp.where(key_rank < rank_limit, s, -jnp.inf))
            probs, m_news, alphas, l_news = [], [], [], []
            for (_, _, isl, _, _), s in zip(sl, scores):
                mx = jnp.max(s, axis=-1, keepdims=True)
                if first:
                    m_new = jnp.broadcast_to(mx, (Q_BLOCK, LANES))
                else:
                    m_old = st_in[1][isl, :]
                    m_new = jnp.maximum(m_old, mx)
                    alphas.append(jnp.exp2(m_old - m_new))
                p = jnp.exp2(s - jnp.concatenate([m_new, m_new], axis=1))
                l_news.append(jnp.broadcast_to(jnp.sum(p, axis=-1, keepdims=True), (Q_BLOCK, LANES)))
                probs.append(p.astype(BF16))
                m_news.append(m_new)
            accs = [jnp.dot(p, vb[ksl, :].astype(BF16), preferred_element_type=F32)
                    for (_, ksl, _, _, _), p in zip(sl, probs)]
            if not first:
                l_news = [a * st_in[2][isl, :] + ln for (_, _, isl, _, _), a, ln in zip(sl, alphas, l_news)]
                accs = [a * st_in[0][isl, :] + ac for (_, _, isl, _, _), a, ac in zip(sl, alphas, accs)]
            for (_, _, _, osl, _), ac, mn, ln in zip(sl, accs, m_news, l_news):
                if last:
                    st_out[0][osl, :] = ac / ln
                else:
                    st_out[0][osl, :] = ac
                    st_out[1][osl, :] = mn
                    st_out[2][osl, :] = ln
            return carry

        lax.fori_loop(0, grp // Q_BLOCK // ATT_UNROLL, units, 0)

    natural, classes = (acc_n, m_n, l_n), (acc_c, m_c, l_c)
    branch(unit_dil1, (qn_ref, kn_ref, vn_ref), None, natural, True, False)
    branch(unit_dil4, (qc_ref, kc_ref, vc_ref), natural, classes, False, False)
    branch(unit_dil16, (qc_ref, kc_ref, vc_ref), classes, classes, False, True)
    for c in range(ncls):
        acc_n[pl.ds(c, qcls, stride=ncls), :] = acc_c[c * qcls:(c + 1) * qcls, :]
    o_ref[0] = acc_n[...].astype(o_ref.dtype)


def _fused_attention(u3, cos, sin, qn_g, kn_g, wl=0, cast_jobs=()):
    b, s, width = u3.shape
    grp = ATT_GROUP
    assert s % grp == 0
    ng = s // grp
    c_in, c_out, c_shapes, c_args = _cast_job_specs(
        cast_jobs, b * ATT_HEADS * ng, lambda i, h, t: (i * ATT_HEADS + h) * ng + t)
    q_col = (width - 3 * ATT_DIM) // HEAD_DIM

    def col(slab):
        return pl.BlockSpec((1, grp, HEAD_DIM), lambda i, h, t: (i, t, q_col + slab * ATT_HEADS + h))

    tab = pl.BlockSpec((grp, HEAD_DIM), lambda i, h, t: (t, 0))
    vec = pl.BlockSpec((None, 1, HEAD_DIM), lambda i, h, t: (wl, 0, 0))
    scr = lambda rows: pltpu.VMEM((rows, HEAD_DIM), F32)
    outs = pl.pallas_call(
        functools.partial(_attn_body, n_cast=len(c_args)),
        grid=(b, ATT_HEADS, ng),
        in_specs=[col(0), col(1), col(2), tab, tab, vec, vec] + c_in,
        out_specs=[pl.BlockSpec((1, grp, HEAD_DIM), lambda i, h, t: (i, t, h))] + c_out,
        out_shape=[jax.ShapeDtypeStruct((b, s, ATT_DIM), BF16)] + c_shapes,
        scratch_shapes=[scr(grp), scr(2 * grp), scr(2 * grp)] * 2 + [scr(grp)] * 6,
        compiler_params=pltpu.CompilerParams(
            dimension_semantics=("parallel", "parallel", "arbitrary"), vmem_limit_bytes=VMEM_LIMIT),
        name="dilated_attn",
    )(u3, u3, u3, cos, sin, qn_g, kn_g, *c_args)
    return outs[0], list(outs[1:])


def _hgrn_body(q_ref, kk_ref, logf_ref, v_ref, gate_ref, gn_ref, o_ref, st_ref, b_ref, cum_ref, mask_ref,
               sgn_ref, *, rows, chunk):
    sub = chunk // 2
    levels = [sub >> (i + 1) for i in range(sub.bit_length() - 1)]
    small = [h for h in levels if 2 * h < SUBLANES]

    @pl.when(pl.program_id(2) == 0)
    def _():
        st_ref[...] = jnp.zeros_like(st_ref)

    ri = lax.broadcasted_iota(jnp.int32, (chunk, chunk), 0)
    ci = lax.broadcasted_iota(jnp.int32, (chunk, chunk), 1)
    cum_ref[...] = (ci <= ri).astype(BF16)
    row = lax.broadcasted_iota(jnp.int32, (chunk, HEAD_DIM), 0)
    rs_ = lax.broadcasted_iota(jnp.int32, (sub, sub), 0)
    cs_ = lax.broadcasted_iota(jnp.int32, (sub, sub), 1)
    for n, half in enumerate(levels):
        blk = 2 * half
        keep = ((rs_ // blk) == (cs_ // blk)) & ((rs_ & (blk - 1)) >= half) & ((cs_ & (blk - 1)) < half)
        mask_ref[n] = keep.astype(F32)
        sgn_ref[n] = jnp.where((row & (blk - 1)) >= half, 1.0, -1.0)

    heads = range(HGRN_HEADS_PER_STEP)

    def one_chunk(c, carry):
        rs = pl.ds(pl.multiple_of(c * chunk, chunk), chunk)
        hsl = [slice(hh * HEAD_DIM, (hh + 1) * HEAD_DIM) for hh in heads]
        q = [q_ref[0, rs, hs].astype(F32) for hs in hsl]
        kk = [kk_ref[0, rs, hs].astype(F32) for hs in hsl]
        v = [v_ref[0, rs, hs] for hs in hsl]
        bcum = []
        for hs in hsl:
            logf = logf_ref[0, rs, hs]
            hi = logf.astype(BF16)
            rem = logf - hi.astype(F32)
            mid = rem.astype(BF16)
            lo = (rem - mid.astype(F32)).astype(BF16)
            parts = jnp.dot(cum_ref[...], jnp.concatenate([hi, mid, lo], axis=1), preferred_element_type=F32)
            bcum.append((parts[:, :HEAD_DIM] + parts[:, HEAD_DIM:2 * HEAD_DIM] + parts[:, 2 * HEAD_DIM:]) * LOG2E)
        for hh in heads:
            b_ref[hh] = bcum[hh]

        att = [[jnp.zeros((sub, sub), F32), jnp.zeros((sub, sub), F32)] for _ in heads]
        for n, half in enumerate(levels):
            blk = 2 * half
            for hh in heads:
                if half in small:
                    pos = row & (blk - 1)
                    b_mid = bcum[hh]
                    for p in range(blk):
                        off = half - 1 - p
                        if off != 0:
                            b_mid = jnp.where(pos == p, pltpu.roll(bcum[hh], (-off) % chunk, 0), b_mid)
                else:
                    pieces = [jnp.broadcast_to(b_ref[hh, s0 + half - 1:s0 + half, :], (blk, HEAD_DIM))
                              for s0 in range(0, chunk, blk)]
                    b_mid = jnp.concatenate(pieces, axis=0)
                dec = jnp.exp2((bcum[hh] - b_mid) * sgn_ref[n])
                gq = (q[hh] * dec).astype(BF16)
                hk = (kk[hh] * dec).astype(BF16)
                for d in range(2):
                    blk_rows = slice(d * sub, (d + 1) * sub)
                    att[hh][d] = att[hh][d] + _dot_nt(gq[blk_rows], hk[blk_rows]) * mask_ref[n]
        cross = []
        for hh in heads:
            b_top = b_ref[hh, sub - 1:sub, :]
            cross.append(_dot_nt((q[hh][sub:] * jnp.exp2(bcum[hh][sub:] - b_top)).astype(BF16),
                                 (kk[hh][:sub] * jnp.exp2(b_top - bcum[hh][:sub])).astype(BF16)))

        outs = []
        for hh in heads:
            st = st_ref[hh]
            o_lo = jnp.dot(att[hh][0].astype(BF16), v[hh][:sub], preferred_element_type=F32)
            o_hi = (jnp.dot(att[hh][1].astype(BF16), v[hh][sub:], preferred_element_type=F32)
                    + jnp.dot(cross[hh].astype(BF16), v[hh][:sub], preferred_element_type=F32))
            o = jnp.concatenate([o_lo, o_hi], axis=0)
            vf = v[hh].astype(F32)
            o = o + jnp.sum(q[hh] * kk[hh], axis=-1, keepdims=True) * vf
            o = o + _dot_nt((q[hh] * jnp.exp2(bcum[hh])).astype(BF16), st.astype(BF16))
            b_last = b_ref[hh, chunk - 1:chunk, :]
            k_dec = (kk[hh] * jnp.exp2(b_last - bcum[hh])).astype(BF16)
            st_ref[hh] = st * jnp.exp2(b_last) + jnp.dot(vf.T.astype(BF16), k_dec, preferred_element_type=F32)
            outs.append(o)
        for hh, hs in zip(heads, hsl):
            on = _rms_scale(outs[hh]) * gn_ref[:, hs] * gate_ref[0, rs, hs].astype(F32)
            o_ref[0, rs, hs] = on.astype(o_ref.dtype)
        return carry

    lax.fori_loop(0, rows // chunk, one_chunk, 0)


def _hgrn(q3, kk3, logf3, v3, gate3, gn_g, wl, *, rows=2048, chunk=256):
    b, s, _ = logf3.shape
    rows = min(rows, s)
    hp = HGRN_HEADS_PER_STEP
    wide = hp * HEAD_DIM
    sub = chunk // 2
    n_levels = sub.bit_length() - 1
    blk = pl.BlockSpec((1, rows, wide), lambda i, h, r: (i, r, h))
    return pl.pallas_call(
        functools.partial(_hgrn_body, rows=rows, chunk=chunk),
        grid=(b, HGRN_WIDTH // wide, s // rows),
        in_specs=[blk, blk, blk, blk, blk,
                  pl.BlockSpec((None, 1, wide), lambda i, h, r: (wl, 0, h))],
        out_specs=blk,
        out_shape=jax.ShapeDtypeStruct((b, s, HGRN_WIDTH), BF16),
        scratch_shapes=[pltpu.VMEM((hp, HEAD_DIM, HEAD_DIM), F32),
                        pltpu.VMEM((hp, chunk, HEAD_DIM), F32),
                        pltpu.VMEM((chunk, chunk), BF16),
                        pltpu.VMEM((n_levels, sub, sub), F32),
                        pltpu.VMEM((n_levels, chunk, HEAD_DIM), F32)],
        compiler_params=pltpu.CompilerParams(
            dimension_semantics=("parallel", "parallel", "arbitrary"), vmem_limit_bytes=VMEM_LIMIT),
        name="hgrn2",
    )(q3, kk3, logf3, v3, gate3, gn_g)


def _rope_tables(s):
    half = HEAD_DIM // 2
    inv = jnp.exp(-math.log(ROPE_THETA) * jnp.arange(half, dtype=F32) / half)
    ang = jnp.arange(s, dtype=jnp.int32).astype(F32)[:, None] * inv[None, :]
    cos, sin = jnp.cos(ang), jnp.sin(ang)
    return jnp.concatenate([cos, cos], axis=-1), jnp.concatenate([-sin, sin], axis=-1)


def _conv_attn_mixer(x2, b, s, norm_mix, layer, j, w_in, conv_w, conv_b, cn_g, cn_b, qn_g, kn_g, w_out, cos, sin,
                     conv_casts=(), attn_casts=()):
    m = b * s
    n_in = w_in.shape[2]
    u = _norm_matmul(x2, norm_mix, w_in, layer, j, BF16)
    u3 = u.reshape(b, s, n_in)
    a, conv_cast = _conv_module(u3, conv_w, conv_b, cn_g, cn_b, j, conv_casts)
    o, attn_cast = _fused_attention(u3, cos, sin, qn_g, kn_g, j, attn_casts)
    x2 = _outproj(x2, [a.reshape(m, CONV_CH), o.reshape(m, ATT_DIM)], w_out, j)
    return x2, conv_cast, attn_cast


def _hgrn2_mixer(x2, b, s, norm_mix, layer, j, w_in, lb, gn_g, w_out):
    parts = _hgrn_inproj(x2, norm_mix, w_in, lb, layer, j)
    og = _hgrn(*[p.reshape(b, s, HGRN_WIDTH) for p in parts], gn_g, j)
    return _outproj(x2, [og.reshape(b * s, HGRN_WIDTH)], w_out, j)


def kernel(x, norm_ffn1, ffn1_wg, ffn1_wu, ffn1_wd, norm_mix, norm_ffn2, ffn2_wg, ffn2_wu, ffn2_wd, ev_w_in, ev_conv_w, ev_conv_b, ev_cn_g, ev_cn_b, ev_qn_g, ev_kn_g, ev_w_out, od_w_in, od_lb_logits, od_gn_g, od_w_out):
    b, s, d = x.shape
    depth = norm_ffn1.shape[0]
    cos, sin = _rope_tables(s)
    p = jax.nn.softmax(od_lb_logits.astype(F32), axis=0)
    lower_bounds = jnp.cumsum(p, axis=0) - p[0:1]
    row3 = lambda a: a[:, None, :]
    rows2 = lambda w: w.reshape(-1, w.shape[-1])
    g1, gm, g2 = row3(norm_ffn1), row3(norm_mix), row3(norm_ffn2)
    f = ffn1_wg.shape[2]
    gu1 = {0: (ffn1_wg[0:1].astype(BF16), ffn1_wu[0:1].astype(BF16), 0)}
    later = (depth - 1) * d
    conv_casts = [(rows2(ffn1_wg), d, later), (rows2(ffn1_wu), d, later),
                  (rows2(od_w_in), 0, od_w_in.shape[0] * d)] if depth > 1 else []
    attn_casts = [(rows2(ffn2_wg), 0, depth * d), (rows2(ffn2_wu), 0, depth * d)]
    gu2 = od_in = None
    x2 = x.reshape(b * s, d)
    for l in range(depth):
        j = l // 2
        wg1, wu1, idx1 = gu1[l]
        x2 = _ffn(x2, g1, wg1, wu1, ffn1_wd, l, idx1)
        if l % 2 == 0:
            first = l == 0
            x2, conv_cast, attn_cast = _conv_attn_mixer(
                x2, b, s, gm, l, j, ev_w_in.astype(BF16), ev_conv_w, row3(ev_conv_b), row3(ev_cn_g), row3(ev_cn_b),
                row3(ev_qn_g), row3(ev_kn_g), ev_w_out, cos, sin,
                conv_casts if first else (), attn_casts if first else ())
            if first:
                gu2 = [w.reshape(depth, d, f) for w in attn_cast]
                if conv_cast:
                    wg_l, wu_l, od_in = conv_cast
                    for ll in range(1, depth):
                        gu1[ll] = (wg_l.reshape(depth - 1, d, f), wu_l.reshape(depth - 1, d, f), ll - 1)
                    od_in = od_in.reshape(od_w_in.shape)
        else:
            x2 = _hgrn2_mixer(x2, b, s, gm, l, j, od_in, lower_bounds[l][None], row3(od_gn_g), od_w_out)
        x2 = _ffn(x2, g2, gu2[0], gu2[1], ffn2_wd, l, l)
    return x2.reshape(b, s, d)
```

```python
import functools
import math

import jax
import jax.numpy as jnp
from jax import lax
from jax.experimental import pallas as pl
from jax.experimental.pallas import tpu as pltpu

F32 = jnp.float32
BF16 = jnp.bfloat16
EPS = 1e-6
LOG2E = 1.4426950408889634
F32_TINY = 1.1754944e-38

LANES = 128
SUBLANES = 8
HEAD_DIM = 128
CONV_CH = 1024
CONV_WIDTH = 31
CONV_HALO = 32
ATT_HEADS = 8
ATT_DIM = ATT_HEADS * HEAD_DIM
DIL_PATTERNS = ((128, 1), (512, 4), (2048, 16))
Q_BLOCK = 128
ATT_GROUP = Q_BLOCK * max(d for _, d in DIL_PATTERNS)
ROPE_THETA = 10000.0
HGRN_HEADS = 16
HGRN_WIDTH = HGRN_HEADS * HEAD_DIM
HGRN_HEADS_PER_STEP = 4
ATT_UNROLL = 8

VMEM_LIMIT = 60 * 1024 * 1024


def _sigmoid(x):
    return 0.5 * jnp.tanh(0.5 * x) + 0.5


def _rms_scale(x):
    return x * lax.rsqrt(jnp.mean(x * x, axis=-1, keepdims=True) + EPS)


def _dot_nt(a, b):
    return lax.dot_general(a, b, (((1,), (1,)), ((), ())), preferred_element_type=F32)


def _ffn_body(x_ref, g_ref, wg_ref, wu_ref, wd_ref, o_ref, xn_ref):
    def hidden(xn):
        hg = jnp.dot(xn, wg_ref[...], preferred_element_type=F32)
        hu = jnp.dot(xn, wu_ref[...], preferred_element_type=F32)
        return ((hg * _sigmoid(hg)) * hu * 0.5).astype(BF16)

    @pl.when(pl.program_id(1) == 0)
    def _():
        x = x_ref[...]
        xn = (_rms_scale(x) * g_ref[...]).astype(BF16)
        xn_ref[...] = xn
        o_ref[...] = x + jnp.dot(hidden(xn), wd_ref[...].astype(BF16), preferred_element_type=F32)

    @pl.when(pl.program_id(1) > 0)
    def _():
        o_ref[...] += jnp.dot(hidden(xn_ref[...]), wd_ref[...].astype(BF16), preferred_element_type=F32)


def _ffn(x2, g, wg, wu, wd, layer, gu_layer, *, tm=1024, tf=512):
    m, d = x2.shape
    f = wg.shape[2]
    tm = min(tm, m)
    return pl.pallas_call(
        _ffn_body,
        grid=(m // tm, f // tf),
        in_specs=[
            pl.BlockSpec((tm, d), lambda i, k: (i, 0)),
            pl.BlockSpec((None, 1, d), lambda i, k: (layer, 0, 0)),
            pl.BlockSpec((None, d, tf), lambda i, k: (gu_layer, 0, k)),
            pl.BlockSpec((None, d, tf), lambda i, k: (gu_layer, 0, k)),
            pl.BlockSpec((None, tf, d), lambda i, k: (layer, k, 0)),
        ],
        out_specs=pl.BlockSpec((tm, d), lambda i, k: (i, 0)),
        out_shape=jax.ShapeDtypeStruct((m, d), F32),
        scratch_shapes=[pltpu.VMEM((tm, d), BF16)],
        compiler_params=pltpu.CompilerParams(
            dimension_semantics=("parallel", "arbitrary"), vmem_limit_bytes=VMEM_LIMIT),
        name="ffn",
    )(x2, g, wg, wu, wd)


def _norm_matmul_body(x_ref, g_ref, w_ref, o_ref, xn_ref):
    @pl.when(pl.program_id(1) == 0)
    def _():
        xn_ref[...] = (_rms_scale(x_ref[...]) * g_ref[...]).astype(BF16)

    o_ref[...] = jnp.dot(xn_ref[...], w_ref[...], preferred_element_type=F32).astype(o_ref.dtype)


def _norm_matmul(x2, g, w, layer, wl, out_dtype, *, tm=1024, tn=1024):
    m, d = x2.shape
    n = w.shape[2]
    tm = min(tm, m)
    return pl.pallas_call(
        _norm_matmul_body,
        grid=(m // tm, n // tn),
        in_specs=[
            pl.BlockSpec((tm, d), lambda i, k: (i, 0)),
            pl.BlockSpec((None, 1, d), lambda i, k: (layer, 0, 0)),
            pl.BlockSpec((None, d, tn), lambda i, k: (wl, 0, k)),
        ],
        out_specs=pl.BlockSpec((tm, tn), lambda i, k: (i, k)),
        out_shape=jax.ShapeDtypeStruct((m, n), out_dtype),
        scratch_shapes=[pltpu.VMEM((tm, d), BF16)],
        compiler_params=pltpu.CompilerParams(
            dimension_semantics=("parallel", "arbitrary"), vmem_limit_bytes=VMEM_LIMIT),
        name="norm_matmul",
    )(x2, g, w)


def _hgrn_inproj_body(x_ref, g_ref, w_ref, lb_ref, q_ref, kk_ref, logf_ref, v_ref, gate_ref, xn_ref, *, per):
    k = pl.program_id(1)

    @pl.when(k == 0)
    def _():
        xn_ref[...] = (_rms_scale(x_ref[...]) * g_ref[...]).astype(BF16)

    def pre():
        return jnp.dot(xn_ref[...], w_ref[...], preferred_element_type=F32)

    @pl.when(k < per)
    def _():
        z = pre()
        q_ref[...] = (z * _sigmoid(z)).astype(q_ref.dtype)

    @pl.when((k >= per) & (k < 2 * per))
    def _():
        fz = pre()
        kk = (1.0 - lb_ref[...]) * (0.5 - 0.5 * jnp.tanh(0.5 * fz))
        kk_ref[...] = kk.astype(kk_ref.dtype)
        logf_ref[...] = jnp.log(jnp.maximum(1.0 - kk, F32_TINY))

    @pl.when((k >= 2 * per) & (k < 3 * per))
    def _():
        v_ref[...] = pre().astype(v_ref.dtype)

    @pl.when(k >= 3 * per)
    def _():
        z = pre()
        gate_ref[...] = (z * _sigmoid(z)).astype(gate_ref.dtype)


def _hgrn_inproj(x2, g, w, lb, layer, wl, *, tm=1024, tn=1024):
    m, d = x2.shape
    tm = min(tm, m)
    width = w.shape[2] // 4
    per = width // tn

    def slab(n):
        return pl.BlockSpec((tm, tn), lambda i, k: (i, jnp.clip(k - n * per, 0, per - 1)))

    bf = jax.ShapeDtypeStruct((m, width), BF16)
    return pl.pallas_call(
        functools.partial(_hgrn_inproj_body, per=per),
        grid=(m // tm, 4 * per),
        in_specs=[
            pl.BlockSpec((tm, d), lambda i, k: (i, 0)),
            pl.BlockSpec((None, 1, d), lambda i, k: (layer, 0, 0)),
            pl.BlockSpec((None, d, tn), lambda i, k: (wl, 0, k)),
            pl.BlockSpec((1, tn), lambda i, k: (0, jnp.clip(k - per, 0, per - 1))),
        ],
        out_specs=[slab(0), slab(1), slab(1), slab(2), slab(3)],
        out_shape=[bf, bf, jax.ShapeDtypeStruct((m, width), F32), bf, bf],
        scratch_shapes=[pltpu.VMEM((tm, d), BF16)],
        compiler_params=pltpu.CompilerParams(
            dimension_semantics=("parallel", "arbitrary"), vmem_limit_bytes=VMEM_LIMIT),
        name="hgrn_inproj",
    )(x2, g, w, lb)


def _outproj_body(*refs, n_in):
    x_ref, o_ref = refs[0], refs[-1]
    acc = x_ref[...]
    for y_ref, w_ref in zip(refs[1:1 + n_in], refs[1 + n_in:1 + 2 * n_in]):
        acc = acc + jnp.dot(y_ref[...], w_ref[...].astype(BF16), preferred_element_type=F32)
    o_ref[...] = acc


def _outproj(x2, ys, w, wl, *, tm=512):
    m, d = x2.shape
    tm = min(tm, m)
    in_specs = [pl.BlockSpec((tm, d), lambda i: (i, 0))]
    in_specs += [pl.BlockSpec((tm, y.shape[1]), lambda i: (i, 0)) for y in ys]
    row0 = 0
    for y in ys:
        rows = y.shape[1]
        assert row0 % rows == 0
        in_specs.append(pl.BlockSpec((None, rows, d), functools.partial(lambda i, rb: (wl, rb, 0), rb=row0 // rows)))
        row0 += rows
    return pl.pallas_call(
        functools.partial(_outproj_body, n_in=len(ys)),
        grid=(m // tm,),
        in_specs=in_specs,
        out_specs=pl.BlockSpec((tm, d), lambda i: (i, 0)),
        out_shape=jax.ShapeDtypeStruct((m, d), F32),
        compiler_params=pltpu.CompilerParams(
            dimension_semantics=("parallel",), vmem_limit_bytes=VMEM_LIMIT),
        name="outproj",
    )(x2, *ys, *([w] * len(ys)))


def _cast_job_specs(jobs, n_steps, step_index):
    in_specs, out_specs, out_shapes, args = [], [], [], []
    for arr, row0, nrows in jobs:
        rows, cols = nrows // n_steps, arr.shape[1]
        assert nrows % n_steps == 0 and rows % 16 == 0 and row0 % rows == 0
        in_specs.append(pl.BlockSpec((rows, cols), functools.partial(
            lambda *g, first: (step_index(*g) + first, 0), first=row0 // rows)))
        out_specs.append(pl.BlockSpec((rows, cols), lambda *g: (step_index(*g), 0)))
        out_shapes.append(jax.ShapeDtypeStruct((nrows, cols), BF16))
        args.append(arr)
    return in_specs, out_specs, out_shapes, args


def _run_cast_jobs(cast_in, cast_out):
    for src, dst in zip(cast_in, cast_out):
        dst[...] = src[...].astype(dst.dtype)


def _conv_body(*refs, ts, n_cast):
    val_ref, gate_ref, hval_ref, hgate_ref, w_ref, b_ref, lg_ref, lb_ref = refs[:8]
    cast_in, refs = refs[8:8 + n_cast], refs[8 + n_cast:]
    o_ref, cast_out = refs[0], refs[1:1 + n_cast]
    buf_ref, sh_ref, y_ref = refs[1 + n_cast:]
    _run_cast_jobs(cast_in, cast_out)
    ah = hval_ref[0].astype(F32) * _sigmoid(hgate_ref[0].astype(F32))
    buf_ref[0:CONV_HALO, :] = jnp.where(pl.program_id(1) > 0, ah, 0.0)
    buf_ref[CONV_HALO:, :] = val_ref[0].astype(F32) * _sigmoid(gate_ref[0].astype(F32))
    span = ts + CONV_HALO - SUBLANES
    for r in range(1, SUBLANES):
        sh_ref[r - 1, :, :] = buf_ref[r:r + span, :]
    first = CONV_HALO - (CONV_WIDTH - 1)
    rb = min(ts, 16)

    def taps(i, carry):
        r0 = pl.multiple_of(i * rb, rb)
        for cb in range(CONV_CH // LANES):
            cs = slice(cb * LANES, (cb + 1) * LANES)
            acc = jnp.broadcast_to(b_ref[:, cs], (rb, LANES))
            for k in range(CONV_WIDTH):
                off = first + k
                rows = pl.ds(r0 + (off - off % SUBLANES), rb)
                if off % SUBLANES == 0:
                    tap = buf_ref[rows, cs]
                else:
                    tap = sh_ref[off % SUBLANES - 1, rows, cs]
                acc = acc + w_ref[k:k + 1, cs] * tap
            y_ref[pl.ds(r0, rb), cs] = acc
        return carry

    lax.fori_loop(0, ts // rb, taps, 0)

    nb = min(ts, 128)

    def norm(i, carry):
        rs = pl.ds(pl.multiple_of(i * nb, nb), nb)
        y = y_ref[rs, :]
        mu = jnp.mean(y, axis=-1, keepdims=True)
        yc = y - mu
        var = jnp.mean(yc * yc, axis=-1, keepdims=True)
        z = yc * lax.rsqrt(var + EPS) * lg_ref[...] + lb_ref[...]
        o_ref[0, rs, :] = (z * _sigmoid(z)).astype(o_ref.dtype)
        return carry

    lax.fori_loop(0, ts // nb, norm, 0)


def _conv_module(u3, conv_w, conv_b, cn_g, cn_b, wl, cast_jobs=(), *, ts=512):
    b, s, _ = u3.shape
    ts = min(ts, s)
    hb = ts // CONV_HALO
    nt = s // ts
    c_in, c_out, c_shapes, c_args = _cast_job_specs(cast_jobs, b * nt, lambda i, t: i * nt + t)

    def halo_idx(col):
        return lambda i, t: (i, jnp.maximum(t * hb - 1, 0), col)

    vec = pl.BlockSpec((None, 1, CONV_CH), lambda i, t: (wl, 0, 0))
    outs = pl.pallas_call(
        functools.partial(_conv_body, ts=ts, n_cast=len(c_args)),
        grid=(b, nt),
        in_specs=[
            pl.BlockSpec((1, ts, CONV_CH), lambda i, t: (i, t, 0)),
            pl.BlockSpec((1, ts, CONV_CH), lambda i, t: (i, t, 1)),
            pl.BlockSpec((1, CONV_HALO, CONV_CH), halo_idx(0)),
            pl.BlockSpec((1, CONV_HALO, CONV_CH), halo_idx(1)),
            pl.BlockSpec((None, CONV_WIDTH, CONV_CH), lambda i, t: (wl, 0, 0)),
            vec, vec, vec,
        ] + c_in,
        out_specs=[pl.BlockSpec((1, ts, CONV_CH), lambda i, t: (i, t, 0))] + c_out,
        out_shape=[jax.ShapeDtypeStruct((b, s, CONV_CH), BF16)] + c_shapes,
        scratch_shapes=[pltpu.VMEM((ts + CONV_HALO, CONV_CH), F32),
                        pltpu.VMEM((SUBLANES - 1, ts + CONV_HALO - SUBLANES, CONV_CH), F32),
                        pltpu.VMEM((ts, CONV_CH), F32)],
        compiler_params=pltpu.CompilerParams(
            dimension_semantics=("parallel", "parallel"), vmem_limit_bytes=VMEM_LIMIT),
        name="conv_module",
    )(u3, u3, u3, u3, conv_w, conv_b, cn_g, cn_b, *c_args)
    return outs[0], list(outs[1:])


def _attn_body(*refs, n_cast):
    q_ref, k_ref, v_ref, cos_ref, sin_ref, qg_ref, kg_ref = refs[:7]
    cast_in, refs = refs[7:7 + n_cast], refs[7 + n_cast:]
    o_ref, cast_out = refs[0], refs[1:1 + n_cast]
    qn_ref, kn_ref, vn_ref, qc_ref, kc_ref, vc_ref, acc_n, m_n, l_n, acc_c, m_c, l_c = refs[1 + n_cast:]
    _run_cast_jobs(cast_in, cast_out)
    g = pl.program_id(2)
    grp = ATT_GROUP
    assert tuple(d for _, d in DIL_PATTERNS) == (1, 4, 16) and all(w // d == Q_BLOCK for w, d in DIL_PATTERNS)
    ncls = 4
    qcls = grp // ncls
    kcls = 2 * grp // ncls

    @pl.when(g == 0)
    def _():
        kn_ref[0:grp, :] = jnp.zeros((grp, HEAD_DIM), F32)
        vn_ref[0:grp, :] = jnp.zeros((grp, HEAD_DIM), F32)

    @pl.when(g > 0)
    def _():
        kn_ref[0:grp, :] = kn_ref[grp:2 * grp, :]
        vn_ref[0:grp, :] = vn_ref[grp:2 * grp, :]

    scale = HEAD_DIM ** -0.5
    prep_rows = 512

    def prep(i, carry):
        r0 = pl.multiple_of(i * prep_rows, prep_rows)
        rs = pl.ds(r0, prep_rows)
        cur = pl.ds(grp + r0, prep_rows)
        cos = cos_ref[rs, :]
        sin = sin_ref[rs, :]

        def rope(ref, gain_ref):
            y = _rms_scale(ref[0, rs, :].astype(F32)) * gain_ref[...]
            return y * cos + pltpu.roll(y, HEAD_DIM // 2, 1) * sin

        qn_ref[rs, :] = rope(q_ref, qg_ref) * (scale * LOG2E)
        kn_ref[cur, :] = rope(k_ref, kg_ref)
        vn_ref[cur, :] = v_ref[0, rs, :].astype(F32)
        return carry

    lax.fori_loop(0, grp // prep_rows, prep, 0)

    for c in range(ncls):
        qc_ref[c * qcls:(c + 1) * qcls, :] = qn_ref[pl.ds(c, qcls, stride=ncls), :]
        kc_ref[c * kcls:(c + 1) * kcls, :] = kn_ref[pl.ds(c, kcls, stride=ncls), :]
        vc_ref[c * kcls:(c + 1) * kcls, :] = vn_ref[pl.ds(c, kcls, stride=ncls), :]

    row = lax.broadcasted_iota(jnp.int32, (Q_BLOCK, 2 * Q_BLOCK), 0)
    col = lax.broadcasted_iota(jnp.int32, (Q_BLOCK, 2 * Q_BLOCK), 1)
    key_rank = jnp.where(col < Q_BLOCK, jnp.where(col >= row, 0, 1), jnp.where(col - Q_BLOCK <= row, -1, 1))

    def aligned(start, size):
        return pl.ds(pl.multiple_of(start, Q_BLOCK), size)

    def unit_dil1(u):
        qsl = aligned(u * Q_BLOCK, Q_BLOCK)
        return qsl, aligned(grp + (u - 1) * Q_BLOCK, 2 * Q_BLOCK), None, qsl, u > 0

    def unit_dil4(u):
        c, nb = u % ncls, u // ncls
        qsl = aligned(c * qcls + nb * Q_BLOCK, Q_BLOCK)
        ksl = aligned(c * kcls + grp // ncls + (nb - 1) * Q_BLOCK, 2 * Q_BLOCK)
        return qsl, ksl, pl.ds(nb * (Q_BLOCK * ncls) + c, Q_BLOCK, stride=ncls), qsl, nb > 0

    def unit_dil16(u):
        c, a = u % ncls, u // ncls
        qsl = pl.ds(c * qcls + a, Q_BLOCK, stride=ncls)
        return qsl, pl.ds(c * kcls + a, 2 * Q_BLOCK, stride=ncls), qsl, qsl, False

    def branch(unit, qkv, st_in, st_out, first, last):
        qb, kb, vb = qkv

        def units(it, carry):
            sl = [unit(it * ATT_UNROLL + t) for t in range(ATT_UNROLL)]
            scores = []
            for qsl, ksl, _, _, has_prev in sl:
                s = _dot_nt(qb[qsl, :].astype(BF16), kb[ksl, :].astype(BF16))
                rank_limit = jnp.where((g > 0) | has_prev, 1, 0)
                scores.append(jnp.where(key_rank < rank_limit, s, -jnp.inf))
            probs, m_news, alphas, l_news = [], [], [], []
            for (_, _, isl, _, _), s in zip(sl, scores):
                mx = jnp.max(s, axis=-1, keepdims=True)
                if first:
                    m_new = jnp.broadcast_to(mx, (Q_BLOCK, LANES))
                else:
                    m_old = st_in[1][isl, :]
                    m_new = jnp.maximum(m_old, mx)
                    alphas.append(jnp.exp2(m_old - m_new))
                p = jnp.exp2(s - jnp.concatenate([m_new, m_new], axis=1))
                l_news.append(jnp.broadcast_to(jnp.sum(p, axis=-1, keepdims=True), (Q_BLOCK, LANES)))
                probs.append(p.astype(BF16))
                m_news.append(m_new)
            accs = [jnp.dot(p, vb[ksl, :].astype(BF16), preferred_element_type=F32)
                    for (_, ksl, _, _, _), p in zip(sl, probs)]
            if not first:
                l_news = [a * st_in[2][isl, :] + ln for (_, _, isl, _, _), a, ln in zip(sl, alphas, l_news)]
                accs = [a * st_in[0][isl, :] + ac for (_, _, isl, _, _), a, ac in zip(sl, alphas, accs)]
            for (_, _, _, osl, _), ac, mn, ln in zip(sl, accs, m_news, l_news):
                if last:
                    st_out[0][osl, :] = ac / ln
                else:
                    st_out[0][osl, :] = ac
                    st_out[1][osl, :] = mn
                    st_out[2][osl, :] = ln
            return carry

        lax.fori_loop(0, grp // Q_BLOCK // ATT_UNROLL, units, 0)

    natural, classes = (acc_n, m_n, l_n), (acc_c, m_c, l_c)
    branch(unit_dil1, (qn_ref, kn_ref, vn_ref), None, natural, True, False)
    branch(unit_dil4, (qc_ref, kc_ref, vc_ref), natural, classes, False, False)
    branch(unit_dil16, (qc_ref, kc_ref, vc_ref), classes, classes, False, True)
    for c in range(ncls):
        acc_n[pl.ds(c, qcls, stride=ncls), :] = acc_c[c * qcls:(c + 1) * qcls, :]
    o_ref[0] = acc_n[...].astype(o_ref.dtype)


def _fused_attention(u3, cos, sin, qn_g, kn_g, wl=0, cast_jobs=()):
    b, s, width = u3.shape
    grp = ATT_GROUP
    assert s % grp == 0
    ng = s // grp
    c_in, c_out, c_shapes, c_args = _cast_job_specs(
        cast_jobs, b * ATT_HEADS * ng, lambda i, h, t: (i * ATT_HEADS + h) * ng + t)
    q_col = (width - 3 * ATT_DIM) // HEAD_DIM

    def col(slab):
        return pl.BlockSpec((1, grp, HEAD_DIM), lambda i, h, t: (i, t, q_col + slab * ATT_HEADS + h))

    tab = pl.BlockSpec((grp, HEAD_DIM), lambda i, h, t: (t, 0))
    vec = pl.BlockSpec((None, 1, HEAD_DIM), lambda i, h, t: (wl, 0, 0))
    scr = lambda rows: pltpu.VMEM((rows, HEAD_DIM), F32)
    outs = pl.pallas_call(
        functools.partial(_attn_body, n_cast=len(c_args)),
        grid=(b, ATT_HEADS, ng),
        in_specs=[col(0), col(1), col(2), tab, tab, vec, vec] + c_in,
        out_specs=[pl.BlockSpec((1, grp, HEAD_DIM), lambda i, h, t: (i, t, h))] + c_out,
        out_shape=[jax.ShapeDtypeStruct((b, s, ATT_DIM), BF16)] + c_shapes,
        scratch_shapes=[scr(grp), scr(2 * grp), scr(2 * grp)] * 2 + [scr(grp)] * 6,
        compiler_params=pltpu.CompilerParams(
            dimension_semantics=("parallel", "parallel", "arbitrary"), vmem_limit_bytes=VMEM_LIMIT),
        name="dilated_attn",
    )(u3, u3, u3, cos, sin, qn_g, kn_g, *c_args)
    return outs[0], list(outs[1:])


def _hgrn_body(q_ref, kk_ref, logf_ref, v_ref, gate_ref, gn_ref, o_ref, st_ref, b_ref, cum_ref, mask_ref,
               sgn_ref, *, rows, chunk):
    sub = chunk // 2
    levels = [sub >> (i + 1) for i in range(sub.bit_length() - 1)]
    small = [h for h in levels if 2 * h < SUBLANES]

    @pl.when(pl.program_id(2) == 0)
    def _():
        st_ref[...] = jnp.zeros_like(st_ref)

    ri = lax.broadcasted_iota(jnp.int32, (chunk, chunk), 0)
    ci = lax.broadcasted_iota(jnp.int32, (chunk, chunk), 1)
    cum_ref[...] = (ci <= ri).astype(BF16)
    row = lax.broadcasted_iota(jnp.int32, (chunk, HEAD_DIM), 0)
    rs_ = lax.broadcasted_iota(jnp.int32, (sub, sub), 0)
    cs_ = lax.broadcasted_iota(jnp.int32, (sub, sub), 1)
    for n, half in enumerate(levels):
        blk = 2 * half
        keep = ((rs_ // blk) == (cs_ // blk)) & ((rs_ & (blk - 1)) >= half) & ((cs_ & (blk - 1)) < half)
        mask_ref[n] = keep.astype(F32)
        sgn_ref[n] = jnp.where((row & (blk - 1)) >= half, 1.0, -1.0)

    heads = range(HGRN_HEADS_PER_STEP)

    def one_chunk(c, carry):
        rs = pl.ds(pl.multiple_of(c * chunk, chunk), chunk)
        hsl = [slice(hh * HEAD_DIM, (hh + 1) * HEAD_DIM) for hh in heads]
        q = [q_ref[0, rs, hs].astype(F32) for hs in hsl]
        kk = [kk_ref[0, rs, hs].astype(F32) for hs in hsl]
        v = [v_ref[0, rs, hs] for hs in hsl]
        bcum = []
        for hs in hsl:
            logf = logf_ref[0, rs, hs]
            hi = logf.astype(BF16)
            rem = logf - hi.astype(F32)
            mid = rem.astype(BF16)
            lo = (rem - mid.astype(F32)).astype(BF16)
            parts = jnp.dot(cum_ref[...], jnp.concatenate([hi, mid, lo], axis=1), preferred_element_type=F32)
            bcum.append((parts[:, :HEAD_DIM] + parts[:, HEAD_DIM:2 * HEAD_DIM] + parts[:, 2 * HEAD_DIM:]) * LOG2E)
        for hh in heads:
            b_ref[hh] = bcum[hh]

        att = [[jnp.zeros((sub, sub), F32), jnp.zeros((sub, sub), F32)] for _ in heads]
        for n, half in enumerate(levels):
            blk = 2 * half
            for hh in heads:
                if half in small:
                    pos = row & (blk - 1)
                    b_mid = bcum[hh]
                    for p in range(blk):
                        off = half - 1 - p
                        if off != 0:
                            b_mid = jnp.where(pos == p, pltpu.roll(bcum[hh], (-off) % chunk, 0), b_mid)
                else:
                    pieces = [jnp.broadcast_to(b_ref[hh, s0 + half - 1:s0 + half, :], (blk, HEAD_DIM))
                              for s0 in range(0, chunk, blk)]
                    b_mid = jnp.concatenate(pieces, axis=0)
                dec = jnp.exp2((bcum[hh] - b_mid) * sgn_ref[n])
                gq = (q[hh] * dec).astype(BF16)
                hk = (kk[hh] * dec).astype(BF16)
                for d in range(2):
                    blk_rows = slice(d * sub, (d + 1) * sub)
                    att[hh][d] = att[hh][d] + _dot_nt(gq[blk_rows], hk[blk_rows]) * mask_ref[n]
        cross = []
        for hh in heads:
            b_top = b_ref[hh, sub - 1:sub, :]
            cross.append(_dot_nt((q[hh][sub:] * jnp.exp2(bcum[hh][sub:] - b_top)).astype(BF16),
                                 (kk[hh][:sub] * jnp.exp2(b_top - bcum[hh][:sub])).astype(BF16)))

        outs = []
        for hh in heads:
            st = st_ref[hh]
            o_lo = jnp.dot(att[hh][0].astype(BF16), v[hh][:sub], preferred_element_type=F32)
            o_hi = (jnp.dot(att[hh][1].astype(BF16), v[hh][sub:], preferred_element_type=F32)
                    + jnp.dot(cross[hh].astype(BF16), v[hh][:sub], preferred_element_type=F32))
            o = jnp.concatenate([o_lo, o_hi], axis=0)
            vf = v[hh].astype(F32)
            o = o + jnp.sum(q[hh] * kk[hh], axis=-1, keepdims=True) * vf
            o = o + _dot_nt((q[hh] * jnp.exp2(bcum[hh])).astype(BF16), st.astype(BF16))
            b_last = b_ref[hh, chunk - 1:chunk, :]
            k_dec = (kk[hh] * jnp.exp2(b_last - bcum[hh])).astype(BF16)
            st_ref[hh] = st * jnp.exp2(b_last) + jnp.dot(vf.T.astype(BF16), k_dec, preferred_element_type=F32)
            outs.append(o)
        for hh, hs in zip(heads, hsl):
            on = _rms_scale(outs[hh]) * gn_ref[:, hs] * gate_ref[0, rs, hs].astype(F32)
            o_ref[0, rs, hs] = on.astype(o_ref.dtype)
        return carry

    lax.fori_loop(0, rows // chunk, one_chunk, 0)


def _hgrn(q3, kk3, logf3, v3, gate3, gn_g, wl, *, rows=2048, chunk=256):
    b, s, _ = logf3.shape
    rows = min(rows, s)
    hp = HGRN_HEADS_PER_STEP
    wide = hp * HEAD_DIM
    sub = chunk // 2
    n_levels = sub.bit_length() - 1
    blk = pl.BlockSpec((1, rows, wide), lambda i, h, r: (i, r, h))
    return pl.pallas_call(
        functools.partial(_hgrn_body, rows=rows, chunk=chunk),
        grid=(b, HGRN_WIDTH // wide, s // rows),
        in_specs=[blk, blk, blk, blk, blk,
                  pl.BlockSpec((None, 1, wide), lambda i, h, r: (wl, 0, h))],
        out_specs=blk,
        out_shape=jax.ShapeDtypeStruct((b, s, HGRN_WIDTH), BF16),
        scratch_shapes=[pltpu.VMEM((hp, HEAD_DIM, HEAD_DIM), F32),
                        pltpu.VMEM((hp, chunk, HEAD_DIM), F32),
                        pltpu.VMEM((chunk, chunk), BF16),
                        pltpu.VMEM((n_levels, sub, sub), F32),
                        pltpu.VMEM((n_levels, chunk, HEAD_DIM), F32)],
        compiler_params=pltpu.CompilerParams(
            dimension_semantics=("parallel", "parallel", "arbitrary"), vmem_limit_bytes=VMEM_LIMIT),
        name="hgrn2",
    )(q3, kk3, logf3, v3, gate3, gn_g)


def _rope_tables(s):
    half = HEAD_DIM // 2
    inv = jnp.exp(-math.log(ROPE_THETA) * jnp.arange(half, dtype=F32) / half)
    ang = jnp.arange(s, dtype=jnp.int32).astype(F32)[:, None] * inv[None, :]
    cos, sin = jnp.cos(ang), jnp.sin(ang)
    return jnp.concatenate([cos, cos], axis=-1), jnp.concatenate([-sin, sin], axis=-1)


def _conv_attn_mixer(x2, b, s, norm_mix, layer, j, w_in, conv_w, conv_b, cn_g, cn_b, qn_g, kn_g, w_out, cos, sin,
                     conv_casts=(), attn_casts=()):
    m = b * s
    n_in = w_in.shape[2]
    u = _norm_matmul(x2, norm_mix, w_in, layer, j, BF16)
    u3 = u.reshape(b, s, n_in)
    a, conv_cast = _conv_module(u3, conv_w, conv_b, cn_g, cn_b, j, conv_casts)
    o, attn_cast = _fused_attention(u3, cos, sin, qn_g, kn_g, j, attn_casts)
    x2 = _outproj(x2, [a.reshape(m, CONV_CH), o.reshape(m, ATT_DIM)], w_out, j)
    return x2, conv_cast, attn_cast


def _hgrn2_mixer(x2, b, s, norm_mix, layer, j, w_in, lb, gn_g, w_out):
    parts = _hgrn_inproj(x2, norm_mix, w_in, lb, layer, j)
    og = _hgrn(*[p.reshape(b, s, HGRN_WIDTH) for p in parts], gn_g, j)
    return _outproj(x2, [og.reshape(b * s, HGRN_WIDTH)], w_out, j)


def kernel(x, norm_ffn1, ffn1_wg, ffn1_wu, ffn1_wd, norm_mix, norm_ffn2, ffn2_wg, ffn2_wu, ffn2_wd, ev_w_in, ev_conv_w, ev_conv_b, ev_cn_g, ev_cn_b, ev_qn_g, ev_kn_g, ev_w_out, od_w_in, od_lb_logits, od_gn_g, od_w_out):
    b, s, d = x.shape
    depth = norm_ffn1.shape[0]
    cos, sin = _rope_tables(s)
    p = jax.nn.softmax(od_lb_logits.astype(F32), axis=0)
    lower_bounds = jnp.cumsum(p, axis=0) - p[0:1]
    row3 = lambda a: a[:, None, :]
    rows2 = lambda w: w.reshape(-1, w.shape[-1])
    g1, gm, g2 = row3(norm_ffn1), row3(norm_mix), row3(norm_ffn2)
    f = ffn1_wg.shape[2]
    gu1 = {0: (ffn1_wg[0:1].astype(BF16), ffn1_wu[0:1].astype(BF16), 0)}
    later = (depth - 1) * d
    conv_casts = [(rows2(ffn1_wg), d, later), (rows2(ffn1_wu), d, later),
                  (rows2(od_w_in), 0, od_w_in.shape[0] * d)] if depth > 1 else []
    attn_casts = [(rows2(ffn2_wg), 0, depth * d), (rows2(ffn2_wu), 0, depth * d)]
    gu2 = od_in = None
    x2 = x.reshape(b * s, d)
    for l in range(depth):
        j = l // 2
        wg1, wu1, idx1 = gu1[l]
        x2 = _ffn(x2, g1, wg1, wu1, ffn1_wd, l, idx1)
        if l % 2 == 0:
            first = l == 0
            x2, conv_cast, attn_cast = _conv_attn_mixer(
                x2, b, s, gm, l, j, ev_w_in.astype(BF16), ev_conv_w, row3(ev_conv_b), row3(ev_cn_g), row3(ev_cn_b),
                row3(ev_qn_g), row3(ev_kn_g), ev_w_out, cos, sin,
                conv_casts if first else (), attn_casts if first else ())
            if first:
                gu2 = [w.reshape(depth, d, f) for w in attn_cast]
                if conv_cast:
                    wg_l, wu_l, od_in = conv_cast
                    for ll in range(1, depth):
                        gu1[ll] = (wg_l.reshape(depth - 1, d, f), wu_l.reshape(depth - 1, d, f), ll - 1)
                    od_in = od_in.reshape(od_w_in.shape)
        else:
            x2 = _hgrn2_mixer(x2, b, s, gm, l, j, od_in, lower_bounds[l][None], row3(od_gn_g), od_w_out)
        x2 = _ffn(x2, g2, gu2[0], gu2[1], ffn2_wd, l, l)
    return x2.reshape(b, s, d)
```

```python
import functools
import math

import jax
import jax.numpy as jnp
from jax import lax
from jax.experimental import pallas as pl
from jax.experimental.pallas import tpu as pltpu

F32 = jnp.float32
BF16 = jnp.bfloat16
EPS = 1e-6
LOG2E = 1.4426950408889634
F32_TINY = 1.1754944e-38

LANES = 128
SUBLANES = 8
HEAD_DIM = 128
CONV_CH = 1024
CONV_WIDTH = 31
CONV_HALO = 32
ATT_HEADS = 8
ATT_DIM = ATT_HEADS * HEAD_DIM
DIL_PATTERNS = ((128, 1), (512, 4), (2048, 16))
Q_BLOCK = 128
ATT_GROUP = Q_BLOCK * max(d for _, d in DIL_PATTERNS)
ROPE_THETA = 10000.0
HGRN_HEADS = 16
HGRN_WIDTH = HGRN_HEADS * HEAD_DIM
HGRN_HEADS_PER_STEP = 4
ATT_UNROLL = 16

VMEM_LIMIT = 60 * 1024 * 1024


def _sigmoid(x):
    return 0.5 * jnp.tanh(0.5 * x) + 0.5


def _rms_scale(x):
    return x * lax.rsqrt(jnp.mean(x * x, axis=-1, keepdims=True) + EPS)


def _dot_nt(a, b):
    return lax.dot_general(a, b, (((1,), (1,)), ((), ())), preferred_element_type=F32)


def _ffn_body(x_ref, g_ref, wg_ref, wu_ref, wd_ref, o_ref, xn_ref):
    def hidden(xn):
        hg = jnp.dot(xn, wg_ref[...], preferred_element_type=F32)
        hu = jnp.dot(xn, wu_ref[...], preferred_element_type=F32)
        return ((hg * _sigmoid(hg)) * hu * 0.5).astype(BF16)

    @pl.when(pl.program_id(1) == 0)
    def _():
        x = x_ref[...]
        xn = (_rms_scale(x) * g_ref[...]).astype(BF16)
        xn_ref[...] = xn
        o_ref[...] = x + jnp.dot(hidden(xn), wd_ref[...].astype(BF16), preferred_element_type=F32)

    @pl.when(pl.program_id(1) > 0)
    def _():
        o_ref[...] += jnp.dot(hidden(xn_ref[...]), wd_ref[...].astype(BF16), preferred_element_type=F32)


def _ffn(x2, g, wg, wu, wd, layer, gu_layer, *, tm=1024, tf=512):
    m, d = x2.shape
    f = wg.shape[2]
    tm = min(tm, m)
    return pl.pallas_call(
        _ffn_body,
        grid=(m // tm, f // tf),
        in_specs=[
            pl.BlockSpec((tm, d), lambda i, k: (i, 0)),
            pl.BlockSpec((None, 1, d), lambda i, k: (layer, 0, 0)),
            pl.BlockSpec((None, d, tf), lambda i, k: (gu_layer, 0, k)),
            pl.BlockSpec((None, d, tf), lambda i, k: (gu_layer, 0, k)),
            pl.BlockSpec((None, tf, d), lambda i, k: (layer, k, 0)),
        ],
        out_specs=pl.BlockSpec((tm, d), lambda i, k: (i, 0)),
        out_shape=jax.ShapeDtypeStruct((m, d), F32),
        scratch_shapes=[pltpu.VMEM((tm, d), BF16)],
        compiler_params=pltpu.CompilerParams(
            dimension_semantics=("parallel", "arbitrary"), vmem_limit_bytes=VMEM_LIMIT),
        name="ffn",
    )(x2, g, wg, wu, wd)


def _norm_matmul_body(x_ref, g_ref, w_ref, o_ref, xn_ref):
    @pl.when(pl.program_id(1) == 0)
    def _():
        xn_ref[...] = (_rms_scale(x_ref[...]) * g_ref[...]).astype(BF16)

    o_ref[...] = jnp.dot(xn_ref[...], w_ref[...], preferred_element_type=F32).astype(o_ref.dtype)


def _norm_matmul(x2, g, w, layer, wl, out_dtype, *, tm=1024, tn=1024):
    m, d = x2.shape
    n = w.shape[2]
    tm = min(tm, m)
    return pl.pallas_call(
        _norm_matmul_body,
        grid=(m // tm, n // tn),
        in_specs=[
            pl.BlockSpec((tm, d), lambda i, k: (i, 0)),
            pl.BlockSpec((None, 1, d), lambda i, k: (layer, 0, 0)),
            pl.BlockSpec((None, d, tn), lambda i, k: (wl, 0, k)),
        ],
        out_specs=pl.BlockSpec((tm, tn), lambda i, k: (i, k)),
        out_shape=jax.ShapeDtypeStruct((m, n), out_dtype),
        scratch_shapes=[pltpu.VMEM((tm, d), BF16)],
        compiler_params=pltpu.CompilerParams(
            dimension_semantics=("parallel", "arbitrary"), vmem_limit_bytes=VMEM_LIMIT),
        name="norm_matmul",
    )(x2, g, w)


def _hgrn_inproj_body(x_ref, g_ref, w_ref, lb_ref, q_ref, kk_ref, logf_ref, v_ref, gate_ref, xn_ref, *, per):
    k = pl.program_id(1)

    @pl.when(k == 0)
    def _():
        xn_ref[...] = (_rms_scale(x_ref[...]) * g_ref[...]).astype(BF16)

    def pre():
        return jnp.dot(xn_ref[...], w_ref[...], preferred_element_type=F32)

    @pl.when(k < per)
    def _():
        z = pre()
        q_ref[...] = (z * _sigmoid(z)).astype(q_ref.dtype)

    @pl.when((k >= per) & (k < 2 * per))
    def _():
        fz = pre()
        kk = (1.0 - lb_ref[...]) * (0.5 - 0.5 * jnp.tanh(0.5 * fz))
        kk_ref[...] = kk.astype(kk_ref.dtype)
        logf_ref[...] = jnp.log(jnp.maximum(1.0 - kk, F32_TINY))

    @pl.when((k >= 2 * per) & (k < 3 * per))
    def _():
        v_ref[...] = pre().astype(v_ref.dtype)

    @pl.when(k >= 3 * per)
    def _():
        z = pre()
        gate_ref[...] = (z * _sigmoid(z)).astype(gate_ref.dtype)


def _hgrn_inproj(x2, g, w, lb, layer, wl, *, tm=1024, tn=1024):
    m, d = x2.shape
    tm = min(tm, m)
    width = w.shape[2] // 4
    per = width // tn

    def slab(n):
        return pl.BlockSpec((tm, tn), lambda i, k: (i, jnp.clip(k - n * per, 0, per - 1)))

    bf = jax.ShapeDtypeStruct((m, width), BF16)
    return pl.pallas_call(
        functools.partial(_hgrn_inproj_body, per=per),
        grid=(m // tm, 4 * per),
        in_specs=[
            pl.BlockSpec((tm, d), lambda i, k: (i, 0)),
            pl.BlockSpec((None, 1, d), lambda i, k: (layer, 0, 0)),
            pl.BlockSpec((None, d, tn), lambda i, k: (wl, 0, k)),
            pl.BlockSpec((1, tn), lambda i, k: (0, jnp.clip(k - per, 0, per - 1))),
        ],
        out_specs=[slab(0), slab(1), slab(1), slab(2), slab(3)],
        out_shape=[bf, bf, jax.ShapeDtypeStruct((m, width), F32), bf, bf],
        scratch_shapes=[pltpu.VMEM((tm, d), BF16)],
        compiler_params=pltpu.CompilerParams(
            dimension_semantics=("parallel", "arbitrary"), vmem_limit_bytes=VMEM_LIMIT),
        name="hgrn_inproj",
    )(x2, g, w, lb)


def _outproj_body(*refs, n_in):
    x_ref, o_ref = refs[0], refs[-1]
    acc = x_ref[...]
    for y_ref, w_ref in zip(refs[1:1 + n_in], refs[1 + n_in:1 + 2 * n_in]):
        acc = acc + jnp.dot(y_ref[...], w_ref[...].astype(BF16), preferred_element_type=F32)
    o_ref[...] = acc


def _outproj(x2, ys, w, wl, *, tm=512):
    m, d = x2.shape
    tm = min(tm, m)
    in_specs = [pl.BlockSpec((tm, d), lambda i: (i, 0))]
    in_specs += [pl.BlockSpec((tm, y.shape[1]), lambda i: (i, 0)) for y in ys]
    row0 = 0
    for y in ys:
        rows = y.shape[1]
        assert row0 % rows == 0
        in_specs.append(pl.BlockSpec((None, rows, d), functools.partial(lambda i, rb: (wl, rb, 0), rb=row0 // rows)))
        row0 += rows
    return pl.pallas_call(
        functools.partial(_outproj_body, n_in=len(ys)),
        grid=(m // tm,),
        in_specs=in_specs,
        out_specs=pl.BlockSpec((tm, d), lambda i: (i, 0)),
        out_shape=jax.ShapeDtypeStruct((m, d), F32),
        compiler_params=pltpu.CompilerParams(
            dimension_semantics=("parallel",), vmem_limit_bytes=VMEM_LIMIT),
        name="outproj",
    )(x2, *ys, *([w] * len(ys)))


def _cast_job_specs(jobs, n_steps, step_index):
    in_specs, out_specs, out_shapes, args = [], [], [], []
    for arr, row0, nrows in jobs:
        rows, cols = nrows // n_steps, arr.shape[1]
        assert nrows % n_steps == 0 and rows % 16 == 0 and row0 % rows == 0
        in_specs.append(pl.BlockSpec((rows, cols), functools.partial(
            lambda *g, first: (step_index(*g) + first, 0), first=row0 // rows)))
        out_specs.append(pl.BlockSpec((rows, cols), lambda *g: (step_index(*g), 0)))
        out_shapes.append(jax.ShapeDtypeStruct((nrows, cols), BF16))
        args.append(arr)
    return in_specs, out_specs, out_shapes, args


def _run_cast_jobs(cast_in, cast_out):
    for src, dst in zip(cast_in, cast_out):
        dst[...] = src[...].astype(dst.dtype)


def _conv_body(*refs, ts, n_cast):
    val_ref, gate_ref, hval_ref, hgate_ref, w_ref, b_ref, lg_ref, lb_ref = refs[:8]
    cast_in, refs = refs[8:8 + n_cast], refs[8 + n_cast:]
    o_ref, cast_out = refs[0], refs[1:1 + n_cast]
    buf_ref, sh_ref, y_ref = refs[1 + n_cast:]
    _run_cast_jobs(cast_in, cast_out)
    ah = hval_ref[0].astype(F32) * _sigmoid(hgate_ref[0].astype(F32))
    buf_ref[0:CONV_HALO, :] = jnp.where(pl.program_id(1) > 0, ah, 0.0)
    buf_ref[CONV_HALO:, :] = val_ref[0].astype(F32) * _sigmoid(gate_ref[0].astype(F32))
    span = ts + CONV_HALO - SUBLANES
    for r in range(1, SUBLANES):
        sh_ref[r - 1, :, :] = buf_ref[r:r + span, :]
    first = CONV_HALO - (CONV_WIDTH - 1)
    rb = min(ts, 16)

    def taps(i, carry):
        r0 = pl.multiple_of(i * rb, rb)
        for cb in range(CONV_CH // LANES):
            cs = slice(cb * LANES, (cb + 1) * LANES)
            acc = jnp.broadcast_to(b_ref[:, cs], (rb, LANES))
            for k in range(CONV_WIDTH):
                off = first + k
                rows = pl.ds(r0 + (off - off % SUBLANES), rb)
                if off % SUBLANES == 0:
                    tap = buf_ref[rows, cs]
                else:
                    tap = sh_ref[off % SUBLANES - 1, rows, cs]
                acc = acc + w_ref[k:k + 1, cs] * tap
            y_ref[pl.ds(r0, rb), cs] = acc
        return carry

    lax.fori_loop(0, ts // rb, taps, 0)

    nb = min(ts, 128)

    def norm(i, carry):
        rs = pl.ds(pl.multiple_of(i * nb, nb), nb)
        y = y_ref[rs, :]
        mu = jnp.mean(y, axis=-1, keepdims=True)
        yc = y - mu
        var = jnp.mean(yc * yc, axis=-1, keepdims=True)
        z = yc * lax.rsqrt(var + EPS) * lg_ref[...] + lb_ref[...]
        o_ref[0, rs, :] = (z * _sigmoid(z)).astype(o_ref.dtype)
        return carry

    lax.fori_loop(0, ts // nb, norm, 0)


def _conv_module(u3, conv_w, conv_b, cn_g, cn_b, wl, cast_jobs=(), *, ts=512):
    b, s, _ = u3.shape
    ts = min(ts, s)
    hb = ts // CONV_HALO
    nt = s // ts
    c_in, c_out, c_shapes, c_args = _cast_job_specs(cast_jobs, b * nt, lambda i, t: i * nt + t)

    def halo_idx(col):
        return lambda i, t: (i, jnp.maximum(t * hb - 1, 0), col)

    vec = pl.BlockSpec((None, 1, CONV_CH), lambda i, t: (wl, 0, 0))
    outs = pl.pallas_call(
        functools.partial(_conv_body, ts=ts, n_cast=len(c_args)),
        grid=(b, nt),
        in_specs=[
            pl.BlockSpec((1, ts, CONV_CH), lambda i, t: (i, t, 0)),
            pl.BlockSpec((1, ts, CONV_CH), lambda i, t: (i, t, 1)),
            pl.BlockSpec((1, CONV_HALO, CONV_CH), halo_idx(0)),
            pl.BlockSpec((1, CONV_HALO, CONV_CH), halo_idx(1)),
            pl.BlockSpec((None, CONV_WIDTH, CONV_CH), lambda i, t: (wl, 0, 0)),
            vec, vec, vec,
        ] + c_in,
        out_specs=[pl.BlockSpec((1, ts, CONV_CH), lambda i, t: (i, t, 0))] + c_out,
        out_shape=[jax.ShapeDtypeStruct((b, s, CONV_CH), BF16)] + c_shapes,
        scratch_shapes=[pltpu.VMEM((ts + CONV_HALO, CONV_CH), F32),
                        pltpu.VMEM((SUBLANES - 1, ts + CONV_HALO - SUBLANES, CONV_CH), F32),
                        pltpu.VMEM((ts, CONV_CH), F32)],
        compiler_params=pltpu.CompilerParams(
            dimension_semantics=("parallel", "parallel"), vmem_limit_bytes=VMEM_LIMIT),
        name="conv_module",
    )(u3, u3, u3, u3, conv_w, conv_b, cn_g, cn_b, *c_args)
    return outs[0], list(outs[1:])


def _attn_body(*refs, n_cast):
    q_ref, k_ref, v_ref, cos_ref, sin_ref, qg_ref, kg_ref = refs[:7]
    cast_in, refs = refs[7:7 + n_cast], refs[7 + n_cast:]
    o_ref, cast_out = refs[0], refs[1:1 + n_cast]
    qn_ref, kn_ref, vn_ref, qc_ref, kc_ref, vc_ref, acc_n, m_n, l_n, acc_c, m_c, l_c = refs[1 + n_cast:]
    _run_cast_jobs(cast_in, cast_out)
    g = pl.program_id(2)
    grp = ATT_GROUP
    assert tuple(d for _, d in DIL_PATTERNS) == (1, 4, 16) and all(w // d == Q_BLOCK for w, d in DIL_PATTERNS)
    ncls = 4
    qcls = grp // ncls
    kcls = 2 * grp // ncls

    @pl.when(g == 0)
    def _():
        kn_ref[0:grp, :] = jnp.zeros((grp, HEAD_DIM), F32)
        vn_ref[0:grp, :] = jnp.zeros((grp, HEAD_DIM), F32)

    @pl.when(g > 0)
    def _():
        kn_ref[0:grp, :] = kn_ref[grp:2 * grp, :]
        vn_ref[0:grp, :] = vn_ref[grp:2 * grp, :]

    scale = HEAD_DIM ** -0.5
    prep_rows = 512

    def prep(i, carry):
        r0 = pl.multiple_of(i * prep_rows, prep_rows)
        rs = pl.ds(r0, prep_rows)
        cur = pl.ds(grp + r0, prep_rows)
        cos = cos_ref[rs, :]
        sin = sin_ref[rs, :]

        def rope(ref, gain_ref):
            y = _rms_scale(ref[0, rs, :].astype(F32)) * gain_ref[...]
            return y * cos + pltpu.roll(y, HEAD_DIM // 2, 1) * sin

        qn_ref[rs, :] = rope(q_ref, qg_ref) * (scale * LOG2E)
        kn_ref[cur, :] = rope(k_ref, kg_ref)
        vn_ref[cur, :] = v_ref[0, rs, :].astype(F32)
        return carry

    lax.fori_loop(0, grp // prep_rows, prep, 0)

    for c in range(ncls):
        qc_ref[c * qcls:(c + 1) * qcls, :] = qn_ref[pl.ds(c, qcls, stride=ncls), :]
        kc_ref[c * kcls:(c + 1) * kcls, :] = kn_ref[pl.ds(c, kcls, stride=ncls), :]
        vc_ref[c * kcls:(c + 1) * kcls, :] = vn_ref[pl.ds(c, kcls, stride=ncls), :]

    row = lax.broadcasted_iota(jnp.int32, (Q_BLOCK, 2 * Q_BLOCK), 0)
    col = lax.broadcasted_iota(jnp.int32, (Q_BLOCK, 2 * Q_BLOCK), 1)
    key_rank = jnp.where(col < Q_BLOCK, jnp.where(col >= row, 0, 1), jnp.where(col - Q_BLOCK <= row, -1, 1))

    def aligned(start, size):
        return pl.ds(pl.multiple_of(start, Q_BLOCK), size)

    def unit_dil1(u):
        qsl = aligned(u * Q_BLOCK, Q_BLOCK)
        return qsl, aligned(grp + (u - 1) * Q_BLOCK, 2 * Q_BLOCK), None, qsl, u > 0

    def unit_dil4(u):
        c, nb = u % ncls, u // ncls
        qsl = aligned(c * qcls + nb * Q_BLOCK, Q_BLOCK)
        ksl = aligned(c * kcls + grp // ncls + (nb - 1) * Q_BLOCK, 2 * Q_BLOCK)
        return qsl, ksl, pl.ds(nb * (Q_BLOCK * ncls) + c, Q_BLOCK, stride=ncls), qsl, nb > 0

    def unit_dil16(u):
        c, a = u % ncls, u // ncls
        qsl = pl.ds(c * qcls + a, Q_BLOCK, stride=ncls)
        return qsl, pl.ds(c * kcls + a, 2 * Q_BLOCK, stride=ncls), qsl, qsl, False

    def branch(unit, qkv, st_in, st_out, first, last):
        qb, kb, vb = qkv

        def units(it, carry):
            sl = [unit(it * ATT_UNROLL + t) for t in range(ATT_UNROLL)]
            scores = []
            for qsl, ksl, _, _, has_prev in sl:
                s = _dot_nt(qb[qsl, :].astype(BF16), kb[ksl, :].astype(BF16))
                rank_limit = jnp.where((g > 0) | has_prev, 1, 0)
                scores.append(jnp.where(key_rank < rank_limit, s, -jnp.inf))
            probs, m_news, alphas, l_news = [], [], [], []
            for (_, _, isl, _, _), s in zip(sl, scores):
                mx = jnp.max(s, axis=-1, keepdims=True)
                if first:
                    m_new = jnp.broadcast_to(mx, (Q_BLOCK, LANES))
                else:
                    m_old = st_in[1][isl, :]
                    m_new = jnp.maximum(m_old, mx)
                    alphas.append(jnp.exp2(m_old - m_new))
                p = jnp.exp2(s - jnp.concatenate([m_new, m_new], axis=1))
                l_news.append(jnp.broadcast_to(jnp.sum(p, axis=-1, keepdims=True), (Q_BLOCK, LANES)))
                probs.append(p.astype(BF16))
                m_news.append(m_new)
            accs = [jnp.dot(p, vb[ksl, :].astype(BF16), preferred_element_type=F32)
                    for (_, ksl, _, _, _), p in zip(sl, probs)]
            if not first:
                l_news = [a * st_in[2][isl, :] + ln for (_, _, isl, _, _), a, ln in zip(sl, alphas, l_news)]
                accs = [a * st_in[0][isl, :] + ac for (_, _, isl, _, _), a, ac in zip(sl, alphas, accs)]
            for (_, _, _, osl, _), ac, mn, ln in zip(sl, accs, m_news, l_news):
                if last:
                    st_out[0][osl, :] = ac / ln
                else:
                    st_out[0][osl, :] = ac
                    st_out[1][osl, :] = mn
                    st_out[2][osl, :] = ln
            return carry

        lax.fori_loop(0, grp // Q_BLOCK // ATT_UNROLL, units, 0)

    natural, classes = (acc_n, m_n, l_n), (acc_c, m_c, l_c)
    branch(unit_dil1, (qn_ref, kn_ref, vn_ref), None, natural, True, False)
    branch(unit_dil4, (qc_ref, kc_ref, vc_ref), natural, classes, False, False)
    branch(unit_dil16, (qc_ref, kc_ref, vc_ref), classes, classes, False, True)
    for c in range(ncls):
        acc_n[pl.ds(c, qcls, stride=ncls), :] = acc_c[c * qcls:(c + 1) * qcls, :]
    o_ref[0] = acc_n[...].astype(o_ref.dtype)


def _fused_attention(u3, cos, sin, qn_g, kn_g, wl=0, cast_jobs=()):
    b, s, width = u3.shape
    grp = ATT_GROUP
    assert s % grp == 0
    ng = s // grp
    c_in, c_out, c_shapes, c_args = _cast_job_specs(
        cast_jobs, b * ATT_HEADS * ng, lambda i, h, t: (i * ATT_HEADS + h) * ng + t)
    q_col = (width - 3 * ATT_DIM) // HEAD_DIM

    def col(slab):
        return pl.BlockSpec((1, grp, HEAD_DIM), lambda i, h, t: (i, t, q_col + slab * ATT_HEADS + h))

    tab = pl.BlockSpec((grp, HEAD_DIM), lambda i, h, t: (t, 0))
    vec = pl.BlockSpec((None, 1, HEAD_DIM), lambda i, h, t: (wl, 0, 0))
    scr = lambda rows: pltpu.VMEM((rows, HEAD_DIM), F32)
    outs = pl.pallas_call(
        functools.partial(_attn_body, n_cast=len(c_args)),
        grid=(b, ATT_HEADS, ng),
        in_specs=[col(0), col(1), col(2), tab, tab, vec, vec] + c_in,
        out_specs=[pl.BlockSpec((1, grp, HEAD_DIM), lambda i, h, t: (i, t, h))] + c_out,
        out_shape=[jax.ShapeDtypeStruct((b, s, ATT_DIM), BF16)] + c_shapes,
        scratch_shapes=[scr(grp), scr(2 * grp), scr(2 * grp)] * 2 + [scr(grp)] * 6,
        compiler_params=pltpu.CompilerParams(
            dimension_semantics=("parallel", "parallel", "arbitrary"), vmem_limit_bytes=VMEM_LIMIT),
        name="dilated_attn",
    )(u3, u3, u3, cos, sin, qn_g, kn_g, *c_args)
    return outs[0], list(outs[1:])


def _hgrn_body(q_ref, kk_ref, logf_ref, v_ref, gate_ref, gn_ref, o_ref, st_ref, b_ref, cum_ref, mask_ref,
               sgn_ref, *, rows, chunk):
    sub = chunk // 2
    levels = [sub >> (i + 1) for i in range(sub.bit_length() - 1)]
    small = [h for h in levels if 2 * h < SUBLANES]

    @pl.when(pl.program_id(2) == 0)
    def _():
        st_ref[...] = jnp.zeros_like(st_ref)

    ri = lax.broadcasted_iota(jnp.int32, (chunk, chunk), 0)
    ci = lax.broadcasted_iota(jnp.int32, (chunk, chunk), 1)
    cum_ref[...] = (ci <= ri).astype(BF16)
    row = lax.broadcasted_iota(jnp.int32, (chunk, HEAD_DIM), 0)
    rs_ = lax.broadcasted_iota(jnp.int32, (sub, sub), 0)
    cs_ = lax.broadcasted_iota(jnp.int32, (sub, sub), 1)
    for n, half in enumerate(levels):
        blk = 2 * half
        keep = ((rs_ // blk) == (cs_ // blk)) & ((rs_ & (blk - 1)) >= half) & ((cs_ & (blk - 1)) < half)
        mask_ref[n] = keep.astype(F32)
        sgn_ref[n] = jnp.where((row & (blk - 1)) >= half, 1.0, -1.0)

    heads = range(HGRN_HEADS_PER_STEP)

    def one_chunk(c, carry):
        rs = pl.ds(pl.multiple_of(c * chunk, chunk), chunk)
        hsl = [slice(hh * HEAD_DIM, (hh + 1) * HEAD_DIM) for hh in heads]
        q = [q_ref[0, rs, hs].astype(F32) for hs in hsl]
        kk = [kk_ref[0, rs, hs].astype(F32) for hs in hsl]
        v = [v_ref[0, rs, hs] for hs in hsl]
        bcum = []
        for hs in hsl:
            logf = logf_ref[0, rs, hs]
            hi = logf.astype(BF16)
            rem = logf - hi.astype(F32)
            mid = rem.astype(BF16)
            lo = (rem - mid.astype(F32)).astype(BF16)
            parts = jnp.dot(cum_ref[...], jnp.concatenate([hi, mid, lo], axis=1), preferred_element_type=F32)
            bcum.append((parts[:, :HEAD_DIM] + parts[:, HEAD_DIM:2 * HEAD_DIM] + parts[:, 2 * HEAD_DIM:]) * LOG2E)
        for hh in heads:
            b_ref[hh] = bcum[hh]

        att = [[jnp.zeros((sub, sub), F32), jnp.zeros((sub, sub), F32)] for _ in heads]
        for n, half in enumerate(levels):
            blk = 2 * half
            for hh in heads:
                if half in small:
                    pos = row & (blk - 1)
                    b_mid = bcum[hh]
                    for p in range(blk):
                        off = half - 1 - p
                        if off != 0:
                            b_mid = jnp.where(pos == p, pltpu.roll(bcum[hh], (-off) % chunk, 0), b_mid)
                else:
                    pieces = [jnp.broadcast_to(b_ref[hh, s0 + half - 1:s0 + half, :], (blk, HEAD_DIM))
                              for s0 in range(0, chunk, blk)]
                    b_mid = jnp.concatenate(pieces, axis=0)
                dec = jnp.exp2((bcum[hh] - b_mid) * sgn_ref[n])
                gq = (q[hh] * dec).astype(BF16)
                hk = (kk[hh] * dec).astype(BF16)
                for d in range(2):
                    blk_rows = slice(d * sub, (d + 1) * sub)
                    att[hh][d] = att[hh][d] + _dot_nt(gq[blk_rows], hk[blk_rows]) * mask_ref[n]
        cross = []
        for hh in heads:
            b_top = b_ref[hh, sub - 1:sub, :]
            cross.append(_dot_nt((q[hh][sub:] * jnp.exp2(bcum[hh][sub:] - b_top)).astype(BF16),
                                 (kk[hh][:sub] * jnp.exp2(b_top - bcum[hh][:sub])).astype(BF16)))

        outs = []
        for hh in heads:
            st = st_ref[hh]
            o_lo = jnp.dot(att[hh][0].astype(BF16), v[hh][:sub], preferred_element_type=F32)
            o_hi = (jnp.dot(att[hh][1].astype(BF16), v[hh][sub:], preferred_element_type=F32)
                    + jnp.dot(cross[hh].astype(BF16), v[hh][:sub], preferred_element_type=F32))
            o = jnp.concatenate([o_lo, o_hi], axis=0)
            vf = v[hh].astype(F32)
            o = o + jnp.sum(q[hh] * kk[hh], axis=-1, keepdims=True) * vf
            o = o + _dot_nt((q[hh] * jnp.exp2(bcum[hh])).astype(BF16), st.astype(BF16))
            b_last = b_ref[hh, chunk - 1:chunk, :]
            k_dec = (kk[hh] * jnp.exp2(b_last - bcum[hh])).astype(BF16)
            st_ref[hh] = st * jnp.exp2(b_last) + jnp.dot(vf.T.astype(BF16), k_dec, preferred_element_type=F32)
            outs.append(o)
        for hh, hs in zip(heads, hsl):
            on = _rms_scale(outs[hh]) * gn_ref[:, hs] * gate_ref[0, rs, hs].astype(F32)
            o_ref[0, rs, hs] = on.astype(o_ref.dtype)
        return carry

    lax.fori_loop(0, rows // chunk, one_chunk, 0)


def _hgrn(q3, kk3, logf3, v3, gate3, gn_g, wl, *, rows=2048, chunk=256):
    b, s, _ = logf3.shape
    rows = min(rows, s)
    hp = HGRN_HEADS_PER_STEP
    wide = hp * HEAD_DIM
    sub = chunk // 2
    n_levels = sub.bit_length() - 1
    blk = pl.BlockSpec((1, rows, wide), lambda i, h, r: (i, r, h))
    return pl.pallas_call(
        functools.partial(_hgrn_body, rows=rows, chunk=chunk),
        grid=(b, HGRN_WIDTH // wide, s // rows),
        in_specs=[blk, blk, blk, blk, blk,
                  pl.BlockSpec((None, 1, wide), lambda i, h, r: (wl, 0, h))],
        out_specs=blk,
        out_shape=jax.ShapeDtypeStruct((b, s, HGRN_WIDTH), BF16),
        scratch_shapes=[pltpu.VMEM((hp, HEAD_DIM, HEAD_DIM), F32),
                        pltpu.VMEM((hp, chunk, HEAD_DIM), F32),
                        pltpu.VMEM((chunk, chunk), BF16),
                        pltpu.VMEM((n_levels, sub, sub), F32),
                        pltpu.VMEM((n_levels, chunk, HEAD_DIM), F32)],
        compiler_params=pltpu.CompilerParams(
            dimension_semantics=("parallel", "parallel", "arbitrary"), vmem_limit_bytes=VMEM_LIMIT),
        name="hgrn2",
    )(q3, kk3, logf3, v3, gate3, gn_g)


def _rope_tables(s):
    half = HEAD_DIM // 2
    inv = jnp.exp(-math.log(ROPE_THETA) * jnp.arange(half, dtype=F32) / half)
    ang = jnp.arange(s, dtype=jnp.int32).astype(F32)[:, None] * inv[None, :]
    cos, sin = jnp.cos(ang), jnp.sin(ang)
    return jnp.concatenate([cos, cos], axis=-1), jnp.concatenate([-sin, sin], axis=-1)


def _conv_attn_mixer(x2, b, s, norm_mix, layer, j, w_in, conv_w, conv_b, cn_g, cn_b, qn_g, kn_g, w_out, cos, sin,
                     conv_casts=(), attn_casts=()):
    m = b * s
    n_in = w_in.shape[2]
    u = _norm_matmul(x2, norm_mix, w_in, layer, j, BF16)
    u3 = u.reshape(b, s, n_in)
    a, conv_cast = _conv_module(u3, conv_w, conv_b, cn_g, cn_b, j, conv_casts)
    o, attn_cast = _fused_attention(u3, cos, sin, qn_g, kn_g, j, attn_casts)
    x2 = _outproj(x2, [a.reshape(m, CONV_CH), o.reshape(m, ATT_DIM)], w_out, j)
    return x2, conv_cast, attn_cast


def _hgrn2_mixer(x2, b, s, norm_mix, layer, j, w_in, lb, gn_g, w_out):
    parts = _hgrn_inproj(x2, norm_mix, w_in, lb, layer, j)
    og = _hgrn(*[p.reshape(b, s, HGRN_WIDTH) for p in parts], gn_g, j)
    return _outproj(x2, [og.reshape(b * s, HGRN_WIDTH)], w_out, j)


def kernel(x, norm_ffn1, ffn1_wg, ffn1_wu, ffn1_wd, norm_mix, norm_ffn2, ffn2_wg, ffn2_wu, ffn2_wd, ev_w_in, ev_conv_w, ev_conv_b, ev_cn_g, ev_cn_b, ev_qn_g, ev_kn_g, ev_w_out, od_w_in, od_lb_logits, od_gn_g, od_w_out):
    b, s, d = x.shape
    depth = norm_ffn1.shape[0]
    cos, sin = _rope_tables(s)
    p = jax.nn.softmax(od_lb_logits.astype(F32), axis=0)
    lower_bounds = jnp.cumsum(p, axis=0) - p[0:1]
    row3 = lambda a: a[:, None, :]
    rows2 = lambda w: w.reshape(-1, w.shape[-1])
    g1, gm, g2 = row3(norm_ffn1), row3(norm_mix), row3(norm_ffn2)
    f = ffn1_wg.shape[2]
    gu1 = {0: (ffn1_wg[0:1].astype(BF16), ffn1_wu[0:1].astype(BF16), 0)}
    later = (depth - 1) * d
    conv_casts = [(rows2(ffn1_wg), d, later), (rows2(ffn1_wu), d, later),
                  (rows2(od_w_in), 0, od_w_in.shape[0] * d)] if depth > 1 else []
    attn_casts = [(rows2(ffn2_wg), 0, depth * d), (rows2(ffn2_wu), 0, depth * d)]
    gu2 = od_in = None
    x2 = x.reshape(b * s, d)
    for l in range(depth):
        j = l // 2
        wg1, wu1, idx1 = gu1[l]
        x2 = _ffn(x2, g1, wg1, wu1, ffn1_wd, l, idx1)
        if l % 2 == 0:
            first = l == 0
            x2, conv_cast, attn_cast = _conv_attn_mixer(
                x2, b, s, gm, l, j, ev_w_in.astype(BF16), ev_conv_w, row3(ev_conv_b), row3(ev_cn_g), row3(ev_cn_b),
                row3(ev_qn_g), row3(ev_kn_g), ev_w_out, cos, sin,
                conv_casts if first else (), attn_casts if first else ())
            if first:
                gu2 = [w.reshape(depth, d, f) for w in attn_cast]
                if conv_cast:
                    wg_l, wu_l, od_in = conv_cast
                    for ll in range(1, depth):
                        gu1[ll] = (wg_l.reshape(depth - 1, d, f), wu_l.reshape(depth - 1, d, f), ll - 1)
                    od_in = od_in.reshape(od_w_in.shape)
        else:
            x2 = _hgrn2_mixer(x2, b, s, gm, l, j, od_in, lower_bounds[l][None], row3(od_gn_g), od_w_out)
        x2 = _ffn(x2, g2, gu2[0], gu2[1], ffn2_wd, l, l)
    return x2.reshape(b, s, d)
```

```python
import functools
import math

import jax
import jax.numpy as jnp
from jax import lax
from jax.experimental import pallas as pl
from jax.experimental.pallas import tpu as pltpu

F32 = jnp.float32
BF16 = jnp.bfloat16
EPS = 1e-6
LOG2E = 1.4426950408889634
F32_TINY = 1.1754944e-38

LANES = 128
SUBLANES = 8
HEAD_DIM = 128
CONV_CH = 1024
CONV_WIDTH = 31
CONV_HALO = 32
ATT_HEADS = 8
ATT_DIM = ATT_HEADS * HEAD_DIM
DIL_PATTERNS = ((128, 1), (512, 4), (2048, 16))
Q_BLOCK = 128
ATT_GROUP = Q_BLOCK * max(d for _, d in DIL_PATTERNS)
ROPE_THETA = 10000.0
HGRN_HEADS = 16
HGRN_WIDTH = HGRN_HEADS * HEAD_DIM
HGRN_HEADS_PER_STEP = 4
ATT_UNROLL = 16

VMEM_LIMIT = 60 * 1024 * 1024


def _sigmoid(x):
    return 0.5 * jnp.tanh(0.5 * x) + 0.5


def _rms_scale(x):
    return x * lax.rsqrt(jnp.mean(x * x, axis=-1, keepdims=True) + EPS)


def _dot_nt(a, b):
    return lax.dot_general(a, b, (((1,), (1,)), ((), ())), preferred_element_type=F32)


def _ffn_body(x_ref, g_ref, wg_ref, wu_ref, wd_ref, o_ref, xn_ref):
    def hidden(xn):
        hg = jnp.dot(xn, wg_ref[...], preferred_element_type=F32)
        hu = jnp.dot(xn, wu_ref[...], preferred_element_type=F32)
        return ((hg * _sigmoid(hg)) * hu * 0.5).astype(BF16)

    @pl.when(pl.program_id(1) == 0)
    def _():
        x = x_ref[...]
        xn = (_rms_scale(x) * g_ref[...]).astype(BF16)
        xn_ref[...] = xn
        o_ref[...] = x + jnp.dot(hidden(xn), wd_ref[...].astype(BF16), preferred_element_type=F32)

    @pl.when(pl.program_id(1) > 0)
    def _():
        o_ref[...] += jnp.dot(hidden(xn_ref[...]), wd_ref[...].astype(BF16), preferred_element_type=F32)


def _ffn(x2, g, wg, wu, wd, layer, gu_layer, *, tm=1024, tf=512):
    m, d = x2.shape
    f = wg.shape[2]
    tm = min(tm, m)
    return pl.pallas_call(
        _ffn_body,
        grid=(m // tm, f // tf),
        in_specs=[
            pl.BlockSpec((tm, d), lambda i, k: (i, 0)),
            pl.BlockSpec((None, 1, d), lambda i, k: (layer, 0, 0)),
            pl.BlockSpec((None, d, tf), lambda i, k: (gu_layer, 0, k)),
            pl.BlockSpec((None, d, tf), lambda i, k: (gu_layer, 0, k)),
            pl.BlockSpec((None, tf, d), lambda i, k: (layer, k, 0)),
        ],
        out_specs=pl.BlockSpec((tm, d), lambda i, k: (i, 0)),
        out_shape=jax.ShapeDtypeStruct((m, d), F32),
        scratch_shapes=[pltpu.VMEM((tm, d), BF16)],
        compiler_params=pltpu.CompilerParams(
            dimension_semantics=("parallel", "arbitrary"), vmem_limit_bytes=VMEM_LIMIT),
        name="ffn",
    )(x2, g, wg, wu, wd)


def _norm_matmul_body(x_ref, g_ref, w_ref, o_ref, xn_ref):
    @pl.when(pl.program_id(1) == 0)
    def _():
        xn_ref[...] = (_rms_scale(x_ref[...]) * g_ref[...]).astype(BF16)

    o_ref[...] = jnp.dot(xn_ref[...], w_ref[...], preferred_element_type=F32).astype(o_ref.dtype)


def _norm_matmul(x2, g, w, layer, wl, out_dtype, *, tm=1024, tn=1024):
    m, d = x2.shape
    n = w.shape[2]
    tm = min(tm, m)
    return pl.pallas_call(
        _norm_matmul_body,
        grid=(m // tm, n // tn),
        in_specs=[
            pl.BlockSpec((tm, d), lambda i, k: (i, 0)),
            pl.BlockSpec((None, 1, d), lambda i, k: (layer, 0, 0)),
            pl.BlockSpec((None, d, tn), lambda i, k: (wl, 0, k)),
        ],
        out_specs=pl.BlockSpec((tm, tn), lambda i, k: (i, k)),
        out_shape=jax.ShapeDtypeStruct((m, n), out_dtype),
        scratch_shapes=[pltpu.VMEM((tm, d), BF16)],
        compiler_params=pltpu.CompilerParams(
            dimension_semantics=("parallel", "arbitrary"), vmem_limit_bytes=VMEM_LIMIT),
        name="norm_matmul",
    )(x2, g, w)


def _hgrn_inproj_body(x_ref, g_ref, w_ref, lb_ref, q_ref, kk_ref, logf_ref, v_ref, gate_ref, xn_ref, *, per):
    k = pl.program_id(1)

    @pl.when(k == 0)
    def _():
        xn_ref[...] = (_rms_scale(x_ref[...]) * g_ref[...]).astype(BF16)

    def pre():
        return jnp.dot(xn_ref[...], w_ref[...], preferred_element_type=F32)

    @pl.when(k < per)
    def _():
        z = pre()
        q_ref[...] = (z * _sigmoid(z)).astype(q_ref.dtype)

    @pl.when((k >= per) & (k < 2 * per))
    def _():
        fz = pre()
        kk = (1.0 - lb_ref[...]) * (0.5 - 0.5 * jnp.tanh(0.5 * fz))
        kk_ref[...] = kk.astype(kk_ref.dtype)
        logf_ref[...] = jnp.log(jnp.maximum(1.0 - kk, F32_TINY))

    @pl.when((k >= 2 * per) & (k < 3 * per))
    def _():
        v_ref[...] = pre().astype(v_ref.dtype)

    @pl.when(k >= 3 * per)
    def _():
        z = pre()
        gate_ref[...] = (z * _sigmoid(z)).astype(gate_ref.dtype)


def _hgrn_inproj(x2, g, w, lb, layer, wl, *, tm=1024, tn=1024):
    m, d = x2.shape
    tm = min(tm, m)
    width = w.shape[2] // 4
    per = width // tn

    def slab(n):
        return pl.BlockSpec((tm, tn), lambda i, k: (i, jnp.clip(k - n * per, 0, per - 1)))

    bf = jax.ShapeDtypeStruct((m, width), BF16)
    return pl.pallas_call(
        functools.partial(_hgrn_inproj_body, per=per),
        grid=(m // tm, 4 * per),
        in_specs=[
            pl.BlockSpec((tm, d), lambda i, k: (i, 0)),
            pl.BlockSpec((None, 1, d), lambda i, k: (layer, 0, 0)),
            pl.BlockSpec((None, d, tn), lambda i, k: (wl, 0, k)),
            pl.BlockSpec((1, tn), lambda i, k: (0, jnp.clip(k - per, 0, per - 1))),
        ],
        out_specs=[slab(0), slab(1), slab(1), slab(2), slab(3)],
        out_shape=[bf, bf, jax.ShapeDtypeStruct((m, width), F32), bf, bf],
        scratch_shapes=[pltpu.VMEM((tm, d), BF16)],
        compiler_params=pltpu.CompilerParams(
            dimension_semantics=("parallel", "arbitrary"), vmem_limit_bytes=VMEM_LIMIT),
        name="hgrn_inproj",
    )(x2, g, w, lb)


def _outproj_body(*refs, n_in):
    x_ref, o_ref = refs[0], refs[-1]
    acc = x_ref[...]
    for y_ref, w_ref in zip(refs[1:1 + n_in], refs[1 + n_in:1 + 2 * n_in]):
        acc = acc + jnp.dot(y_ref[...], w_ref[...].astype(BF16), preferred_element_type=F32)
    o_ref[...] = acc


def _outproj(x2, ys, w, wl, *, tm=512):
    m, d = x2.shape
    tm = min(tm, m)
    in_specs = [pl.BlockSpec((tm, d), lambda i: (i, 0))]
    in_specs += [pl.BlockSpec((tm, y.shape[1]), lambda i: (i, 0)) for y in ys]
    row0 = 0
    for y in ys:
        rows = y.shape[1]
        assert row0 % rows == 0
        in_specs.append(pl.BlockSpec((None, rows, d), functools.partial(lambda i, rb: (wl, rb, 0), rb=row0 // rows)))
        row0 += rows
    return pl.pallas_call(
        functools.partial(_outproj_body, n_in=len(ys)),
        grid=(m // tm,),
        in_specs=in_specs,
        out_specs=pl.BlockSpec((tm, d), lambda i: (i, 0)),
        out_shape=jax.ShapeDtypeStruct((m, d), F32),
        compiler_params=pltpu.CompilerParams(
            dimension_semantics=("parallel",), vmem_limit_bytes=VMEM_LIMIT),
        name="outproj",
    )(x2, *ys, *([w] * len(ys)))


def _cast_job_specs(jobs, n_steps, step_index):
    in_specs, out_specs, out_shapes, args = [], [], [], []
    for arr, row0, nrows in jobs:
        rows, cols = nrows // n_steps, arr.shape[1]
        assert nrows % n_steps == 0 and rows % 16 == 0 and row0 % rows == 0
        in_specs.append(pl.BlockSpec((rows, cols), functools.partial(
            lambda *g, first: (step_index(*g) + first, 0), first=row0 // rows)))
        out_specs.append(pl.BlockSpec((rows, cols), lambda *g: (step_index(*g), 0)))
        out_shapes.append(jax.ShapeDtypeStruct((nrows, cols), BF16))
        args.append(arr)
    return in_specs, out_specs, out_shapes, args


def _run_cast_jobs(cast_in, cast_out):
    for src, dst in zip(cast_in, cast_out):
        dst[...] = src[...].astype(dst.dtype)


def _conv_body(*refs, ts, n_cast):
    val_ref, gate_ref, hval_ref, hgate_ref, w_ref, b_ref, lg_ref, lb_ref = refs[:8]
    cast_in, refs = refs[8:8 + n_cast], refs[8 + n_cast:]
    o_ref, cast_out = refs[0], refs[1:1 + n_cast]
    buf_ref, sh_ref, y_ref = refs[1 + n_cast:]
    _run_cast_jobs(cast_in, cast_out)
    ah = hval_ref[0].astype(F32) * _sigmoid(hgate_ref[0].astype(F32))
    buf_ref[0:CONV_HALO, :] = jnp.where(pl.program_id(1) > 0, ah, 0.0)
    buf_ref[CONV_HALO:, :] = val_ref[0].astype(F32) * _sigmoid(gate_ref[0].astype(F32))
    span = ts + CONV_HALO - SUBLANES
    for r in range(1, SUBLANES):
        sh_ref[r - 1, :, :] = buf_ref[r:r + span, :]
    first = CONV_HALO - (CONV_WIDTH - 1)
    rb = min(ts, 16)

    def taps(i, carry):
        r0 = pl.multiple_of(i * rb, rb)
        for cb in range(CONV_CH // LANES):
            cs = slice(cb * LANES, (cb + 1) * LANES)
            acc = jnp.broadcast_to(b_ref[:, cs], (rb, LANES))
            for k in range(CONV_WIDTH):
                off = first + k
                rows = pl.ds(r0 + (off - off % SUBLANES), rb)
                if off % SUBLANES == 0:
                    tap = buf_ref[rows, cs]
                else:
                    tap = sh_ref[off % SUBLANES - 1, rows, cs]
                acc = acc + w_ref[k:k + 1, cs] * tap
            y_ref[pl.ds(r0, rb), cs] = acc
        return carry

    lax.fori_loop(0, ts // rb, taps, 0)

    nb = min(ts, 128)

    def norm(i, carry):
        rs = pl.ds(pl.multiple_of(i * nb, nb), nb)
        y = y_ref[rs, :]
        mu = jnp.mean(y, axis=-1, keepdims=True)
        yc = y - mu
        var = jnp.mean(yc * yc, axis=-1, keepdims=True)
        z = yc * lax.rsqrt(var + EPS) * lg_ref[...] + lb_ref[...]
        o_ref[0, rs, :] = (z * _sigmoid(z)).astype(o_ref.dtype)
        return carry

    lax.fori_loop(0, ts // nb, norm, 0)


def _conv_module(u3, conv_w, conv_b, cn_g, cn_b, wl, cast_jobs=(), *, ts=512):
    b, s, _ = u3.shape
    ts = min(ts, s)
    hb = ts // CONV_HALO
    nt = s // ts
    c_in, c_out, c_shapes, c_args = _cast_job_specs(cast_jobs, b * nt, lambda i, t: i * nt + t)

    def halo_idx(col):
        return lambda i, t: (i, jnp.maximum(t * hb - 1, 0), col)

    vec = pl.BlockSpec((None, 1, CONV_CH), lambda i, t: (wl, 0, 0))
    outs = pl.pallas_call(
        functools.partial(_conv_body, ts=ts, n_cast=len(c_args)),
        grid=(b, nt),
        in_specs=[
            pl.BlockSpec((1, ts, CONV_CH), lambda i, t: (i, t, 0)),
            pl.BlockSpec((1, ts, CONV_CH), lambda i, t: (i, t, 1)),
            pl.BlockSpec((1, CONV_HALO, CONV_CH), halo_idx(0)),
            pl.BlockSpec((1, CONV_HALO, CONV_CH), halo_idx(1)),
            pl.BlockSpec((None, CONV_WIDTH, CONV_CH), lambda i, t: (wl, 0, 0)),
            vec, vec, vec,
        ] + c_in,
        out_specs=[pl.BlockSpec((1, ts, CONV_CH), lambda i, t: (i, t, 0))] + c_out,
        out_shape=[jax.ShapeDtypeStruct((b, s, CONV_CH), BF16)] + c_shapes,
        scratch_shapes=[pltpu.VMEM((ts + CONV_HALO, CONV_CH), F32),
                        pltpu.VMEM((SUBLANES - 1, ts + CONV_HALO - SUBLANES, CONV_CH), F32),
                        pltpu.VMEM((ts, CONV_CH), F32)],
        compiler_params=pltpu.CompilerParams(
            dimension_semantics=("parallel", "parallel"), vmem_limit_bytes=VMEM_LIMIT),
        name="conv_module",
    )(u3, u3, u3, u3, conv_w, conv_b, cn_g, cn_b, *c_args)
    return outs[0], list(outs[1:])


def _attn_body(*refs, n_cast):
    q_ref, k_ref, v_ref, cos_ref, sin_ref, qg_ref, kg_ref = refs[:7]
    cast_in, refs = refs[7:7 + n_cast], refs[7 + n_cast:]
    o_ref, cast_out = refs[0], refs[1:1 + n_cast]
    qn_ref, kn_ref, vn_ref, qc_ref, kc_ref, vc_ref, acc_n, m_n, l_n, acc_c, m_c, l_c = refs[1 + n_cast:]
    _run_cast_jobs(cast_in, cast_out)
    g = pl.program_id(2)
    grp = ATT_GROUP
    assert tuple(d for _, d in DIL_PATTERNS) == (1, 4, 16) and all(w // d == Q_BLOCK for w, d in DIL_PATTERNS)
    ncls = 4
    qcls = grp // ncls
    kcls = 2 * grp // ncls

    @pl.when(g == 0)
    def _():
        kn_ref[0:grp, :] = jnp.zeros((grp, HEAD_DIM), F32)
        vn_ref[0:grp, :] = jnp.zeros((grp, HEAD_DIM), F32)

    @pl.when(g > 0)
    def _():
        kn_ref[0:grp, :] = kn_ref[grp:2 * grp, :]
        vn_ref[0:grp, :] = vn_ref[grp:2 * grp, :]

    scale = HEAD_DIM ** -0.5
    prep_rows = 1024

    def prep(i, carry):
        r0 = pl.multiple_of(i * prep_rows, prep_rows)
        rs = pl.ds(r0, prep_rows)
        cur = pl.ds(grp + r0, prep_rows)
        cos = cos_ref[rs, :]
        sin = sin_ref[rs, :]

        def rope(ref, gain_ref):
            y = _rms_scale(ref[0, rs, :].astype(F32)) * gain_ref[...]
            return y * cos + pltpu.roll(y, HEAD_DIM // 2, 1) * sin

        qn_ref[rs, :] = rope(q_ref, qg_ref) * (scale * LOG2E)
        kn_ref[cur, :] = rope(k_ref, kg_ref)
        vn_ref[cur, :] = v_ref[0, rs, :].astype(F32)
        return carry

    lax.fori_loop(0, grp // prep_rows, prep, 0)

    for c in range(ncls):
        qc_ref[c * qcls:(c + 1) * qcls, :] = qn_ref[pl.ds(c, qcls, stride=ncls), :]
        kc_ref[c * kcls:(c + 1) * kcls, :] = kn_ref[pl.ds(c, kcls, stride=ncls), :]
        vc_ref[c * kcls:(c + 1) * kcls, :] = vn_ref[pl.ds(c, kcls, stride=ncls), :]

    row = lax.broadcasted_iota(jnp.int32, (Q_BLOCK, 2 * Q_BLOCK), 0)
    col = lax.broadcasted_iota(jnp.int32, (Q_BLOCK, 2 * Q_BLOCK), 1)
    key_rank = jnp.where(col < Q_BLOCK, jnp.where(col >= row, 0, 1), jnp.where(col - Q_BLOCK <= row, -1, 1))

    def aligned(start, size):
        return pl.ds(pl.multiple_of(start, Q_BLOCK), size)

    def unit_dil1(u):
        qsl = aligned(u * Q_BLOCK, Q_BLOCK)
        return qsl, aligned(grp + (u - 1) * Q_BLOCK, 2 * Q_BLOCK), None, qsl, u > 0

    def unit_dil4(u):
        c, nb = u % ncls, u // ncls
        qsl = aligned(c * qcls + nb * Q_BLOCK, Q_BLOCK)
        ksl = aligned(c * kcls + grp // ncls + (nb - 1) * Q_BLOCK, 2 * Q_BLOCK)
        return qsl, ksl, pl.ds(nb * (Q_BLOCK * ncls) + c, Q_BLOCK, stride=ncls), qsl, nb > 0

    def unit_dil16(u):
        c, a = u % ncls, u // ncls
        qsl = pl.ds(c * qcls + a, Q_BLOCK, stride=ncls)
        return qsl, pl.ds(c * kcls + a, 2 * Q_BLOCK, stride=ncls), qsl, qsl, False

    def branch(unit, qkv, st_in, st_out, first, last):
        qb, kb, vb = qkv

        def units(it, carry):
            sl = [unit(it * ATT_UNROLL + t) for t in range(ATT_UNROLL)]
            scores = []
            for qsl, ksl, _, _, has_prev in sl:
                s = _dot_nt(qb[qsl, :].astype(BF16), kb[ksl, :].astype(BF16))
                rank_limit = jnp.where((g > 0) | has_prev, 1, 0)
                scores.append(jnp.where(key_rank < rank_limit, s, -jnp.inf))
            probs, m_news, alphas, l_news = [], [], [], []
            for (_, _, isl, _, _), s in zip(sl, scores):
                mx = jnp.max(s, axis=-1, keepdims=True)
                if first:
                    m_new = jnp.broadcast_to(mx, (Q_BLOCK, LANES))
                else:
                    m_old = st_in[1][isl, :]
                    m_new = jnp.maximum(m_old, mx)
                    alphas.append(jnp.exp2(m_old - m_new))
                p = jnp.exp2(s - jnp.concatenate([m_new, m_new], axis=1))
                l_news.append(jnp.broadcast_to(jnp.sum(p, axis=-1, keepdims=True), (Q_BLOCK, LANES)))
                probs.append(p.astype(BF16))
                m_news.append(m_new)
            accs = [jnp.dot(p, vb[ksl, :].astype(BF16), preferred_element_type=F32)
                    for (_, ksl, _, _, _), p in zip(sl, probs)]
            if not first:
                l_news = [a * st_in[2][isl, :] + ln for (_, _, isl, _, _), a, ln in zip(sl, alphas, l_news)]
                accs = [a * st_in[0][isl, :] + ac for (_, _, isl, _, _), a, ac in zip(sl, alphas, accs)]
            for (_, _, _, osl, _), ac, mn, ln in zip(sl, accs, m_news, l_news):
                if last:
                    st_out[0][osl, :] = ac / ln
                else:
                    st_out[0][osl, :] = ac
                    st_out[1][osl, :] = mn
                    st_out[2][osl, :] = ln
            return carry

        lax.fori_loop(0, grp // Q_BLOCK // ATT_UNROLL, units, 0)

    natural, classes = (acc_n, m_n, l_n), (acc_c, m_c, l_c)
    branch(unit_dil1, (qn_ref, kn_ref, vn_ref), None, natural, True, False)
    branch(unit_dil4, (qc_ref, kc_ref, vc_ref), natural, classes, False, False)
    branch(unit_dil16, (qc_ref, kc_ref, vc_ref), classes, classes, False, True)
    for c in range(ncls):
        acc_n[pl.ds(c, qcls, stride=ncls), :] = acc_c[c * qcls:(c + 1) * qcls, :]
    o_ref[0] = acc_n[...].astype(o_ref.dtype)


def _fused_attention(u3, cos, sin, qn_g, kn_g, wl=0, cast_jobs=()):
    b, s, width = u3.shape
    grp = ATT_GROUP
    assert s % grp == 0
    ng = s // grp
    c_in, c_out, c_shapes, c_args = _cast_job_specs(
        cast_jobs, b * ATT_HEADS * ng, lambda i, h, t: (i * ATT_HEADS + h) * ng + t)
    q_col = (width - 3 * ATT_DIM) // HEAD_DIM

    def col(slab):
        return pl.BlockSpec((1, grp, HEAD_DIM), lambda i, h, t: (i, t, q_col + slab * ATT_HEADS + h))

    tab = pl.BlockSpec((grp, HEAD_DIM), lambda i, h, t: (t, 0))
    vec = pl.BlockSpec((None, 1, HEAD_DIM), lambda i, h, t: (wl, 0, 0))
    scr = lambda rows: pltpu.VMEM((rows, HEAD_DIM), F32)
    outs = pl.pallas_call(
        functools.partial(_attn_body, n_cast=len(c_args)),
        grid=(b, ATT_HEADS, ng),
        in_specs=[col(0), col(1), col(2), tab, tab, vec, vec] + c_in,
        out_specs=[pl.BlockSpec((1, grp, HEAD_DIM), lambda i, h, t: (i, t, h))] + c_out,
        out_shape=[jax.ShapeDtypeStruct((b, s, ATT_DIM), BF16)] + c_shapes,
        scratch_shapes=[scr(grp), scr(2 * grp), scr(2 * grp)] * 2 + [scr(grp)] * 6,
        compiler_params=pltpu.CompilerParams(
            dimension_semantics=("parallel", "parallel", "arbitrary"), vmem_limit_bytes=VMEM_LIMIT),
        name="dilated_attn",
    )(u3, u3, u3, cos, sin, qn_g, kn_g, *c_args)
    return outs[0], list(outs[1:])


def _hgrn_body(q_ref, kk_ref, logf_ref, v_ref, gate_ref, gn_ref, o_ref, st_ref, b_ref, cum_ref, mask_ref,
               sgn_ref, *, rows, chunk):
    sub = chunk // 2
    levels = [sub >> (i + 1) for i in range(sub.bit_length() - 1)]
    small = [h for h in levels if 2 * h < SUBLANES]

    @pl.when(pl.program_id(2) == 0)
    def _():
        st_ref[...] = jnp.zeros_like(st_ref)

    ri = lax.broadcasted_iota(jnp.int32, (chunk, chunk), 0)
    ci = lax.broadcasted_iota(jnp.int32, (chunk, chunk), 1)
    cum_ref[...] = (ci <= ri).astype(BF16)
    row = lax.broadcasted_iota(jnp.int32, (chunk, HEAD_DIM), 0)
    rs_ = lax.broadcasted_iota(jnp.int32, (sub, sub), 0)
    cs_ = lax.broadcasted_iota(jnp.int32, (sub, sub), 1)
    for n, half in enumerate(levels):
        blk = 2 * half
        keep = ((rs_ // blk) == (cs_ // blk)) & ((rs_ & (blk - 1)) >= half) & ((cs_ & (blk - 1)) < half)
        mask_ref[n] = keep.astype(F32)
        sgn_ref[n] = jnp.where((row & (blk - 1)) >= half, 1.0, -1.0)

    heads = range(HGRN_HEADS_PER_STEP)

    def one_chunk(c, carry):
        rs = pl.ds(pl.multiple_of(c * chunk, chunk), chunk)
        hsl = [slice(hh * HEAD_DIM, (hh + 1) * HEAD_DIM) for hh in heads]
        q = [q_ref[0, rs, hs].astype(F32) for hs in hsl]
        kk = [kk_ref[0, rs, hs].astype(F32) for hs in hsl]
        v = [v_ref[0, rs, hs] for hs in hsl]
        bcum = []
        for hs in hsl:
            logf = logf_ref[0, rs, hs]
            hi = logf.astype(BF16)
            rem = logf - hi.astype(F32)
            mid = rem.astype(BF16)
            lo = (rem - mid.astype(F32)).astype(BF16)
            parts = jnp.dot(cum_ref[...], jnp.concatenate([hi, mid, lo], axis=1), preferred_element_type=F32)
            bcum.append((parts[:, :HEAD_DIM] + parts[:, HEAD_DIM:2 * HEAD_DIM] + parts[:, 2 * HEAD_DIM:]) * LOG2E)
        for hh in heads:
            b_ref[hh] = bcum[hh]

        att = [[jnp.zeros((sub, sub), F32), jnp.zeros((sub, sub), F32)] for _ in heads]
        for n, half in enumerate(levels):
            blk = 2 * half
            for hh in heads:
                if half in small:
                    pos = row & (blk - 1)
                    b_mid = bcum[hh]
                    for p in range(blk):
                        off = half - 1 - p
                        if off != 0:
                            b_mid = jnp.where(pos == p, pltpu.roll(bcum[hh], (-off) % chunk, 0), b_mid)
                else:
                    pieces = [jnp.broadcast_to(b_ref[hh, s0 + half - 1:s0 + half, :], (blk, HEAD_DIM))
                              for s0 in range(0, chunk, blk)]
                    b_mid = jnp.concatenate(pieces, axis=0)
                dec = jnp.exp2((bcum[hh] - b_mid) * sgn_ref[n])
                gq = (q[hh] * dec).astype(BF16)
                hk = (kk[hh] * dec).astype(BF16)
                for d in range(2):
                    blk_rows = slice(d * sub, (d + 1) * sub)
                    att[hh][d] = att[hh][d] + _dot_nt(gq[blk_rows], hk[blk_rows]) * mask_ref[n]
        cross = []
        for hh in heads:
            b_top = b_ref[hh, sub - 1:sub, :]
            cross.append(_dot_nt((q[hh][sub:] * jnp.exp2(bcum[hh][sub:] - b_top)).astype(BF16),
                                 (kk[hh][:sub] * jnp.exp2(b_top - bcum[hh][:sub])).astype(BF16)))

        outs = []
        for hh in heads:
            st = st_ref[hh]
            o_lo = jnp.dot(att[hh][0].astype(BF16), v[hh][:sub], preferred_element_type=F32)
            o_hi = (jnp.dot(att[hh][1].astype(BF16), v[hh][sub:], preferred_element_type=F32)
                    + jnp.dot(cross[hh].astype(BF16), v[hh][:sub], preferred_element_type=F32))
            o = jnp.concatenate([o_lo, o_hi], axis=0)
            vf = v[hh].astype(F32)
            o = o + jnp.sum(q[hh] * kk[hh], axis=-1, keepdims=True) * vf
            o = o + _dot_nt((q[hh] * jnp.exp2(bcum[hh])).astype(BF16), st.astype(BF16))
            b_last = b_ref[hh, chunk - 1:chunk, :]
            k_dec = (kk[hh] * jnp.exp2(b_last - bcum[hh])).astype(BF16)
            st_ref[hh] = st * jnp.exp2(b_last) + jnp.dot(vf.T.astype(BF16), k_dec, preferred_element_type=F32)
            outs.append(o)
        for hh, hs in zip(heads, hsl):
            on = _rms_scale(outs[hh]) * gn_ref[:, hs] * gate_ref[0, rs, hs].astype(F32)
            o_ref[0, rs, hs] = on.astype(o_ref.dtype)
        return carry

    lax.fori_loop(0, rows // chunk, one_chunk, 0)


def _hgrn(q3, kk3, logf3, v3, gate3, gn_g, wl, *, rows=2048, chunk=256):
    b, s, _ = logf3.shape
    rows = min(rows, s)
    hp = HGRN_HEADS_PER_STEP
    wide = hp * HEAD_DIM
    sub = chunk // 2
    n_levels = sub.bit_length() - 1
    blk = pl.BlockSpec((1, rows, wide), lambda i, h, r: (i, r, h))
    return pl.pallas_call(
        functools.partial(_hgrn_body, rows=rows, chunk=chunk),
        grid=(b, HGRN_WIDTH // wide, s // rows),
        in_specs=[blk, blk, blk, blk, blk,
                  pl.BlockSpec((None, 1, wide), lambda i, h, r: (wl, 0, h))],
        out_specs=blk,
        out_shape=jax.ShapeDtypeStruct((b, s, HGRN_WIDTH), BF16),
        scratch_shapes=[pltpu.VMEM((hp, HEAD_DIM, HEAD_DIM), F32),
                        pltpu.VMEM((hp, chunk, HEAD_DIM), F32),
                        pltpu.VMEM((chunk, chunk), BF16),
                        pltpu.VMEM((n_levels, sub, sub), F32),
                        pltpu.VMEM((n_levels, chunk, HEAD_DIM), F32)],
        compiler_params=pltpu.CompilerParams(
            dimension_semantics=("parallel", "parallel", "arbitrary"), vmem_limit_bytes=VMEM_LIMIT),
        name="hgrn2",
    )(q3, kk3, logf3, v3, gate3, gn_g)


def _rope_tables(s):
    half = HEAD_DIM // 2
    inv = jnp.exp(-math.log(ROPE_THETA) * jnp.arange(half, dtype=F32) / half)
    ang = jnp.arange(s, dtype=jnp.int32).astype(F32)[:, None] * inv[None, :]
    cos, sin = jnp.cos(ang), jnp.sin(ang)
    return jnp.concatenate([cos, cos], axis=-1), jnp.concatenate([-sin, sin], axis=-1)


def _conv_attn_mixer(x2, b, s, norm_mix, layer, j, w_in, conv_w, conv_b, cn_g, cn_b, qn_g, kn_g, w_out, cos, sin,
                     conv_casts=(), attn_casts=()):
    m = b * s
    n_in = w_in.shape[2]
    u = _norm_matmul(x2, norm_mix, w_in, layer, j, BF16)
    u3 = u.reshape(b, s, n_in)
    a, conv_cast = _conv_module(u3, conv_w, conv_b, cn_g, cn_b, j, conv_casts)
    o, attn_cast = _fused_attention(u3, cos, sin, qn_g, kn_g, j, attn_casts)
    x2 = _outproj(x2, [a.reshape(m, CONV_CH), o.reshape(m, ATT_DIM)], w_out, j)
    return x2, conv_cast, attn_cast


def _hgrn2_mixer(x2, b, s, norm_mix, layer, j, w_in, lb, gn_g, w_out):
    parts = _hgrn_inproj(x2, norm_mix, w_in, lb, layer, j)
    og = _hgrn(*[p.reshape(b, s, HGRN_WIDTH) for p in parts], gn_g, j)
    return _outproj(x2, [og.reshape(b * s, HGRN_WIDTH)], w_out, j)


def kernel(x, norm_ffn1, ffn1_wg, ffn1_wu, ffn1_wd, norm_mix, norm_ffn2, ffn2_wg, ffn2_wu, ffn2_wd, ev_w_in, ev_conv_w, ev_conv_b, ev_cn_g, ev_cn_b, ev_qn_g, ev_kn_g, ev_w_out, od_w_in, od_lb_logits, od_gn_g, od_w_out):
    b, s, d = x.shape
    depth = norm_ffn1.shape[0]
    cos, sin = _rope_tables(s)
    p = jax.nn.softmax(od_lb_logits.astype(F32), axis=0)
    lower_bounds = jnp.cumsum(p, axis=0) - p[0:1]
    row3 = lambda a: a[:, None, :]
    rows2 = lambda w: w.reshape(-1, w.shape[-1])
    g1, gm, g2 = row3(norm_ffn1), row3(norm_mix), row3(norm_ffn2)
    f = ffn1_wg.shape[2]
    gu1 = {0: (ffn1_wg[0:1].astype(BF16), ffn1_wu[0:1].astype(BF16), 0)}
    later = (depth - 1) * d
    conv_casts = [(rows2(ffn1_wg), d, later), (rows2(ffn1_wu), d, later),
                  (rows2(od_w_in), 0, od_w_in.shape[0] * d)] if depth > 1 else []
    attn_casts = [(rows2(ffn2_wg), 0, depth * d), (rows2(ffn2_wu), 0, depth * d)]
    gu2 = od_in = None
    x2 = x.reshape(b * s, d)
    for l in range(depth):
        j = l // 2
        wg1, wu1, idx1 = gu1[l]
        x2 = _ffn(x2, g1, wg1, wu1, ffn1_wd, l, idx1)
        if l % 2 == 0:
            first = l == 0
            x2, conv_cast, attn_cast = _conv_attn_mixer(
                x2, b, s, gm, l, j, ev_w_in.astype(BF16), ev_conv_w, row3(ev_conv_b), row3(ev_cn_g), row3(ev_cn_b),
                row3(ev_qn_g), row3(ev_kn_g), ev_w_out, cos, sin,
                conv_casts if first else (), attn_casts if first else ())
            if first:
                gu2 = [w.reshape(depth, d, f) for w in attn_cast]
                if conv_cast:
                    wg_l, wu_l, od_in = conv_cast
                    for ll in range(1, depth):
                        gu1[ll] = (wg_l.reshape(depth - 1, d, f), wu_l.reshape(depth - 1, d, f), ll - 1)
                    od_in = od_in.reshape(od_w_in.shape)
        else:
            x2 = _hgrn2_mixer(x2, b, s, gm, l, j, od_in, lower_bounds[l][None], row3(od_gn_g), od_w_out)
        x2 = _ffn(x2, g2, gu2[0], gu2[1], ffn2_wd, l, l)
    return x2.reshape(b, s, d)
```

```python
import functools
import math

import jax
import jax.numpy as jnp
from jax import lax
from jax.experimental import pallas as pl
from jax.experimental.pallas import tpu as pltpu

F32 = jnp.float32
BF16 = jnp.bfloat16
EPS = 1e-6
LOG2E = 1.4426950408889634
F32_TINY = 1.1754944e-38

LANES = 128
SUBLANES = 8
HEAD_DIM = 128
CONV_CH = 1024
CONV_WIDTH = 31
CONV_HALO = 32
ATT_HEADS = 8
ATT_DIM = ATT_HEADS * HEAD_DIM
DIL_PATTERNS = ((128, 1), (512, 4), (2048, 16))
Q_BLOCK = 128
ATT_GROUP = Q_BLOCK * max(d for _, d in DIL_PATTERNS)
ROPE_THETA = 10000.0
HGRN_HEADS = 16
HGRN_WIDTH = HGRN_HEADS * HEAD_DIM
HGRN_HEADS_PER_STEP = 4
ATT_UNROLL = 16

VMEM_LIMIT = 60 * 1024 * 1024


def _sigmoid(x):
    return 0.5 * jnp.tanh(0.5 * x) + 0.5


def _rms_scale(x):
    return x * lax.rsqrt(jnp.mean(x * x, axis=-1, keepdims=True) + EPS)


def _dot_nt(a, b):
    return lax.dot_general(a, b, (((1,), (1,)), ((), ())), preferred_element_type=F32)


def _ffn_body(x_ref, g_ref, wg_ref, wu_ref, wd_ref, o_ref, xn_ref):
    def hidden(xn):
        hg = jnp.dot(xn, wg_ref[...], preferred_element_type=F32)
        hu = jnp.dot(xn, wu_ref[...], preferred_element_type=F32)
        return ((hg * _sigmoid(hg)) * hu * 0.5).astype(BF16)

    @pl.when(pl.program_id(1) == 0)
    def _():
        x = x_ref[...]
        xn = (_rms_scale(x) * g_ref[...]).astype(BF16)
        xn_ref[...] = xn
        o_ref[...] = x + jnp.dot(hidden(xn), wd_ref[...].astype(BF16), preferred_element_type=F32)

    @pl.when(pl.program_id(1) > 0)
    def _():
        o_ref[...] += jnp.dot(hidden(xn_ref[...]), wd_ref[...].astype(BF16), preferred_element_type=F32)


def _ffn(x2, g, wg, wu, wd, layer, gu_layer, *, tm=1024, tf=512):
    m, d = x2.shape
    f = wg.shape[2]
    tm = min(tm, m)
    return pl.pallas_call(
        _ffn_body,
        grid=(m // tm, f // tf),
        in_specs=[
            pl.BlockSpec((tm, d), lambda i, k: (i, 0)),
            pl.BlockSpec((None, 1, d), lambda i, k: (layer, 0, 0)),
            pl.BlockSpec((None, d, tf), lambda i, k: (gu_layer, 0, k)),
            pl.BlockSpec((None, d, tf), lambda i, k: (gu_layer, 0, k)),
            pl.BlockSpec((None, tf, d), lambda i, k: (layer, k, 0)),
        ],
        out_specs=pl.BlockSpec((tm, d), lambda i, k: (i, 0)),
        out_shape=jax.ShapeDtypeStruct((m, d), F32),
        scratch_shapes=[pltpu.VMEM((tm, d), BF16)],
        compiler_params=pltpu.CompilerParams(
            dimension_semantics=("parallel", "arbitrary"), vmem_limit_bytes=VMEM_LIMIT),
        name="ffn",
    )(x2, g, wg, wu, wd)


def _norm_matmul_body(x_ref, g_ref, w_ref, o_ref, xn_ref):
    @pl.when(pl.program_id(1) == 0)
    def _():
        xn_ref[...] = (_rms_scale(x_ref[...]) * g_ref[...]).astype(BF16)

    o_ref[...] = jnp.dot(xn_ref[...], w_ref[...], preferred_element_type=F32).astype(o_ref.dtype)


def _norm_matmul(x2, g, w, layer, wl, out_dtype, *, tm=1024, tn=1024):
    m, d = x2.shape
    n = w.shape[2]
    tm = min(tm, m)
    return pl.pallas_call(
        _norm_matmul_body,
        grid=(m // tm, n // tn),
        in_specs=[
            pl.BlockSpec((tm, d), lambda i, k: (i, 0)),
            pl.BlockSpec((None, 1, d), lambda i, k: (layer, 0, 0)),
            pl.BlockSpec((None, d, tn), lambda i, k: (wl, 0, k)),
        ],
        out_specs=pl.BlockSpec((tm, tn), lambda i, k: (i, k)),
        out_shape=jax.ShapeDtypeStruct((m, n), out_dtype),
        scratch_shapes=[pltpu.VMEM((tm, d), BF16)],
        compiler_params=pltpu.CompilerParams(
            dimension_semantics=("parallel", "arbitrary"), vmem_limit_bytes=VMEM_LIMIT),
        name="norm_matmul",
    )(x2, g, w)


def _hgrn_inproj_body(x_ref, g_ref, w_ref, lb_ref, q_ref, kk_ref, logf_ref, v_ref, gate_ref, xn_ref, *, per):
    k = pl.program_id(1)

    @pl.when(k == 0)
    def _():
        xn_ref[...] = (_rms_scale(x_ref[...]) * g_ref[...]).astype(BF16)

    def pre():
        return jnp.dot(xn_ref[...], w_ref[...], preferred_element_type=F32)

    @pl.when(k < per)
    def _():
        z = pre()
        q_ref[...] = (z * _sigmoid(z)).astype(q_ref.dtype)

    @pl.when((k >= per) & (k < 2 * per))
    def _():
        fz = pre()
        kk = (1.0 - lb_ref[...]) * (0.5 - 0.5 * jnp.tanh(0.5 * fz))
        kk_ref[...] = kk.astype(kk_ref.dtype)
        logf_ref[...] = jnp.log(jnp.maximum(1.0 - kk, F32_TINY))

    @pl.when((k >= 2 * per) & (k < 3 * per))
    def _():
        v_ref[...] = pre().astype(v_ref.dtype)

    @pl.when(k >= 3 * per)
    def _():
        z = pre()
        gate_ref[...] = (z * _sigmoid(z)).astype(gate_ref.dtype)


def _hgrn_inproj(x2, g, w, lb, layer, wl, *, tm=1024, tn=1024):
    m, d = x2.shape
    tm = min(tm, m)
    width = w.shape[2] // 4
    per = width // tn

    def slab(n):
        return pl.BlockSpec((tm, tn), lambda i, k: (i, jnp.clip(k - n * per, 0, per - 1)))

    bf = jax.ShapeDtypeStruct((m, width), BF16)
    return pl.pallas_call(
        functools.partial(_hgrn_inproj_body, per=per),
        grid=(m // tm, 4 * per),
        in_specs=[
            pl.BlockSpec((tm, d), lambda i, k: (i, 0)),
            pl.BlockSpec((None, 1, d), lambda i, k: (layer, 0, 0)),
            pl.BlockSpec((None, d, tn), lambda i, k: (wl, 0, k)),
            pl.BlockSpec((1, tn), lambda i, k: (0, jnp.clip(k - per, 0, per - 1))),
        ],
        out_specs=[slab(0), slab(1), slab(1), slab(2), slab(3)],
        out_shape=[bf, bf, jax.ShapeDtypeStruct((m, width), F32), bf, bf],
        scratch_shapes=[pltpu.VMEM((tm, d), BF16)],
        compiler_params=pltpu.CompilerParams(
            dimension_semantics=("parallel", "arbitrary"), vmem_limit_bytes=VMEM_LIMIT),
        name="hgrn_inproj",
    )(x2, g, w, lb)


def _outproj_body(*refs, n_in):
    x_ref, o_ref = refs[0], refs[-1]
    acc = x_ref[...]
    for y_ref, w_ref in zip(refs[1:1 + n_in], refs[1 + n_in:1 + 2 * n_in]):
        acc = acc + jnp.dot(y_ref[...], w_ref[...], preferred_element_type=F32)
    o_ref[...] = acc


def _outproj(x2, ys, w, wl, *, tm=1024):
    m, d = x2.shape
    tm = min(tm, m)
    in_specs = [pl.BlockSpec((tm, d), lambda i: (i, 0))]
    in_specs += [pl.BlockSpec((tm, y.shape[1]), lambda i: (i, 0)) for y in ys]
    row0 = 0
    for y in ys:
        rows = y.shape[1]
        assert row0 % rows == 0
        in_specs.append(pl.BlockSpec((None, rows, d), functools.partial(lambda i, rb: (wl, rb, 0), rb=row0 // rows),
                                     pipeline_mode=pl.Buffered(1)))
        row0 += rows
    return pl.pallas_call(
        functools.partial(_outproj_body, n_in=len(ys)),
        grid=(m // tm,),
        in_specs=in_specs,
        out_specs=pl.BlockSpec((tm, d), lambda i: (i, 0)),
        out_shape=jax.ShapeDtypeStruct((m, d), F32),
        compiler_params=pltpu.CompilerParams(
            dimension_semantics=("parallel",), vmem_limit_bytes=VMEM_LIMIT),
        name="outproj",
    )(x2, *ys, *([w] * len(ys)))


def _cast_job_specs(jobs, n_steps, step_index):
    in_specs, out_specs, out_shapes, args = [], [], [], []
    for arr, row0, nrows in jobs:
        rows, cols = nrows // n_steps, arr.shape[1]
        assert nrows % n_steps == 0 and rows % 16 == 0 and row0 % rows == 0
        in_specs.append(pl.BlockSpec((rows, cols), functools.partial(
            lambda *g, first: (step_index(*g) + first, 0), first=row0 // rows)))
        out_specs.append(pl.BlockSpec((rows, cols), lambda *g: (step_index(*g), 0)))
        out_shapes.append(jax.ShapeDtypeStruct((nrows, cols), BF16))
        args.append(arr)
    return in_specs, out_specs, out_shapes, args


def _run_cast_jobs(cast_in, cast_out):
    for src, dst in zip(cast_in, cast_out):
        dst[...] = src[...].astype(dst.dtype)


def _conv_body(*refs, ts, n_cast):
    val_ref, gate_ref, hval_ref, hgate_ref, w_ref, b_ref, lg_ref, lb_ref = refs[:8]
    cast_in, refs = refs[8:8 + n_cast], refs[8 + n_cast:]
    o_ref, cast_out = refs[0], refs[1:1 + n_cast]
    buf_ref, sh_ref, y_ref = refs[1 + n_cast:]
    _run_cast_jobs(cast_in, cast_out)
    ah = hval_ref[0].astype(F32) * _sigmoid(hgate_ref[0].astype(F32))
    buf_ref[0:CONV_HALO, :] = jnp.where(pl.program_id(1) > 0, ah, 0.0)
    buf_ref[CONV_HALO:, :] = val_ref[0].astype(F32) * _sigmoid(gate_ref[0].astype(F32))
    span = ts + CONV_HALO - SUBLANES
    for r in range(1, SUBLANES):
        sh_ref[r - 1, :, :] = buf_ref[r:r + span, :]
    first = CONV_HALO - (CONV_WIDTH - 1)
    rb = min(ts, 16)

    def taps(i, carry):
        r0 = pl.multiple_of(i * rb, rb)
        for cb in range(CONV_CH // LANES):
            cs = slice(cb * LANES, (cb + 1) * LANES)
            acc = jnp.broadcast_to(b_ref[:, cs], (rb, LANES))
            for k in range(CONV_WIDTH):
                off = first + k
                rows = pl.ds(r0 + (off - off % SUBLANES), rb)
                if off % SUBLANES == 0:
                    tap = buf_ref[rows, cs]
                else:
                    tap = sh_ref[off % SUBLANES - 1, rows, cs]
                acc = acc + w_ref[k:k + 1, cs] * tap
            y_ref[pl.ds(r0, rb), cs] = acc
        return carry

    lax.fori_loop(0, ts // rb, taps, 0)

    nb = min(ts, 128)

    def norm(i, carry):
        rs = pl.ds(pl.multiple_of(i * nb, nb), nb)
        y = y_ref[rs, :]
        mu = jnp.mean(y, axis=-1, keepdims=True)
        yc = y - mu
        var = jnp.mean(yc * yc, axis=-1, keepdims=True)
        z = yc * lax.rsqrt(var + EPS) * lg_ref[...] + lb_ref[...]
        o_ref[0, rs, :] = (z * _sigmoid(z)).astype(o_ref.dtype)
        return carry

    lax.fori_loop(0, ts // nb, norm, 0)


def _conv_module(u3, conv_w, conv_b, cn_g, cn_b, wl, cast_jobs=(), *, ts=512):
    b, s, _ = u3.shape
    ts = min(ts, s)
    hb = ts // CONV_HALO
    nt = s // ts
    c_in, c_out, c_shapes, c_args = _cast_job_specs(cast_jobs, b * nt, lambda i, t: i * nt + t)

    def halo_idx(col):
        return lambda i, t: (i, jnp.maximum(t * hb - 1, 0), col)

    vec = pl.BlockSpec((None, 1, CONV_CH), lambda i, t: (wl, 0, 0))
    outs = pl.pallas_call(
        functools.partial(_conv_body, ts=ts, n_cast=len(c_args)),
        grid=(b, nt),
        in_specs=[
            pl.BlockSpec((1, ts, CONV_CH), lambda i, t: (i, t, 0)),
            pl.BlockSpec((1, ts, CONV_CH), lambda i, t: (i, t, 1)),
            pl.BlockSpec((1, CONV_HALO, CONV_CH), halo_idx(0)),
            pl.BlockSpec((1, CONV_HALO, CONV_CH), halo_idx(1)),
            pl.BlockSpec((None, CONV_WIDTH, CONV_CH), lambda i, t: (wl, 0, 0)),
            vec, vec, vec,
        ] + c_in,
        out_specs=[pl.BlockSpec((1, ts, CONV_CH), lambda i, t: (i, t, 0))] + c_out,
        out_shape=[jax.ShapeDtypeStruct((b, s, CONV_CH), BF16)] + c_shapes,
        scratch_shapes=[pltpu.VMEM((ts + CONV_HALO, CONV_CH), F32),
                        pltpu.VMEM((SUBLANES - 1, ts + CONV_HALO - SUBLANES, CONV_CH), F32),
                        pltpu.VMEM((ts, CONV_CH), F32)],
        compiler_params=pltpu.CompilerParams(
            dimension_semantics=("parallel", "parallel"), vmem_limit_bytes=VMEM_LIMIT),
        name="conv_module",
    )(u3, u3, u3, u3, conv_w, conv_b, cn_g, cn_b, *c_args)
    return outs[0], list(outs[1:])


def _attn_body(*refs, n_cast):
    q_ref, k_ref, v_ref, cos_ref, sin_ref, qg_ref, kg_ref = refs[:7]
    cast_in, refs = refs[7:7 + n_cast], refs[7 + n_cast:]
    o_ref, cast_out = refs[0], refs[1:1 + n_cast]
    qn_ref, kn_ref, vn_ref, qc_ref, kc_ref, vc_ref, acc_n, m_n, l_n, acc_c, m_c, l_c = refs[1 + n_cast:]
    _run_cast_jobs(cast_in, cast_out)
    g = pl.program_id(2)
    grp = ATT_GROUP
    assert tuple(d for _, d in DIL_PATTERNS) == (1, 4, 16) and all(w // d == Q_BLOCK for w, d in DIL_PATTERNS)
    ncls = 4
    qcls = grp // ncls
    kcls = 2 * grp // ncls

    @pl.when(g == 0)
    def _():
        kn_ref[0:grp, :] = jnp.zeros((grp, HEAD_DIM), F32)
        vn_ref[0:grp, :] = jnp.zeros((grp, HEAD_DIM), F32)

    @pl.when(g > 0)
    def _():
        kn_ref[0:grp, :] = kn_ref[grp:2 * grp, :]
        vn_ref[0:grp, :] = vn_ref[grp:2 * grp, :]

    scale = HEAD_DIM ** -0.5
    prep_rows = 1024

    def prep(i, carry):
        r0 = pl.multiple_of(i * prep_rows, prep_rows)
        rs = pl.ds(r0, prep_rows)
        cur = pl.ds(grp + r0, prep_rows)
        cos = cos_ref[rs, :]
        sin = sin_ref[rs, :]

        def rope(ref, gain_ref):
            y = _rms_scale(ref[0, rs, :].astype(F32)) * gain_ref[...]
            return y * cos + pltpu.roll(y, HEAD_DIM // 2, 1) * sin

        qn_ref[rs, :] = rope(q_ref, qg_ref) * (scale * LOG2E)
        kn_ref[cur, :] = rope(k_ref, kg_ref)
        vn_ref[cur, :] = v_ref[0, rs, :].astype(F32)
        return carry

    lax.fori_loop(0, grp // prep_rows, prep, 0)

    for c in range(ncls):
        qc_ref[c * qcls:(c + 1) * qcls, :] = qn_ref[pl.ds(c, qcls, stride=ncls), :]
        kc_ref[c * kcls:(c + 1) * kcls, :] = kn_ref[pl.ds(c, kcls, stride=ncls), :]
        vc_ref[c * kcls:(c + 1) * kcls, :] = vn_ref[pl.ds(c, kcls, stride=ncls), :]

    row = lax.broadcasted_iota(jnp.int32, (Q_BLOCK, 2 * Q_BLOCK), 0)
    col = lax.broadcasted_iota(jnp.int32, (Q_BLOCK, 2 * Q_BLOCK), 1)
    key_rank = jnp.where(col < Q_BLOCK, jnp.where(col >= row, 0, 1), jnp.where(col - Q_BLOCK <= row, -1, 1))

    def aligned(start, size):
        return pl.ds(pl.multiple_of(start, Q_BLOCK), size)

    def unit_dil1(u):
        qsl = aligned(u * Q_BLOCK, Q_BLOCK)
        return qsl, aligned(grp + (u - 1) * Q_BLOCK, 2 * Q_BLOCK), None, qsl, u > 0

    def unit_dil4(u):
        c, nb = u % ncls, u // ncls
        qsl = aligned(c * qcls + nb * Q_BLOCK, Q_BLOCK)
        ksl = aligned(c * kcls + grp // ncls + (nb - 1) * Q_BLOCK, 2 * Q_BLOCK)
        return qsl, ksl, pl.ds(nb * (Q_BLOCK * ncls) + c, Q_BLOCK, stride=ncls), qsl, nb > 0

    def unit_dil16(u):
        c, a = u % ncls, u // ncls
        qsl = pl.ds(c * qcls + a, Q_BLOCK, stride=ncls)
        return qsl, pl.ds(c * kcls + a, 2 * Q_BLOCK, stride=ncls), qsl, qsl, False

    def branch(unit, qkv, st_in, st_out, first, last):
        qb, kb, vb = qkv

        def units(it, carry):
            sl = [unit(it * ATT_UNROLL + t) for t in range(ATT_UNROLL)]
            scores = []
            for qsl, ksl, _, _, has_prev in sl:
                s = _dot_nt(qb[qsl, :].astype(BF16), kb[ksl, :].astype(BF16))
                rank_limit = jnp.where((g > 0) | has_prev, 1, 0)
                scores.append(jnp.where(key_rank < rank_limit, s, -jnp.inf))
            probs, m_news, alphas, l_news = [], [], [], []
            for (_, _, isl, _, _), s in zip(sl, scores):
                mx = jnp.max(s, axis=-1, keepdims=True)
                if first:
                    m_new = jnp.broadcast_to(mx, (Q_BLOCK, LANES))
                else:
                    m_old = st_in[1][isl, :]
                    m_new = jnp.maximum(m_old, mx)
                    alphas.append(jnp.exp2(m_old - m_new))
                p = jnp.exp2(s - jnp.concatenate([m_new, m_new], axis=1))
                l_news.append(jnp.broadcast_to(jnp.sum(p, axis=-1, keepdims=True), (Q_BLOCK, LANES)))
                probs.append(p.astype(BF16))
                m_news.append(m_new)
            accs = [jnp.dot(p, vb[ksl, :].astype(BF16), preferred_element_type=F32)
                    for (_, ksl, _, _, _), p in zip(sl, probs)]
            if not first:
                l_news = [a * st_in[2][isl, :] + ln for (_, _, isl, _, _), a, ln in zip(sl, alphas, l_news)]
                accs = [a * st_in[0][isl, :] + ac for (_, _, isl, _, _), a, ac in zip(sl, alphas, accs)]
            for (_, _, _, osl, _), ac, mn, ln in zip(sl, accs, m_news, l_news):
                if last:
                    st_out[0][osl, :] = ac / ln
                else:
                    st_out[0][osl, :] = ac
                    st_out[1][osl, :] = mn
                    st_out[2][osl, :] = ln
            return carry

        lax.fori_loop(0, grp // Q_BLOCK // ATT_UNROLL, units, 0)

    natural, classes = (acc_n, m_n, l_n), (acc_c, m_c, l_c)
    branch(unit_dil1, (qn_ref, kn_ref, vn_ref), None, natural, True, False)
    branch(unit_dil4, (qc_ref, kc_ref, vc_ref), natural, classes, False, False)
    branch(unit_dil16, (qc_ref, kc_ref, vc_ref), classes, classes, False, True)
    for c in range(ncls):
        acc_n[pl.ds(c, qcls, stride=ncls), :] = acc_c[c * qcls:(c + 1) * qcls, :]
    o_ref[0] = acc_n[...].astype(o_ref.dtype)


def _fused_attention(u3, cos, sin, qn_g, kn_g, wl=0, cast_jobs=()):
    b, s, width = u3.shape
    grp = ATT_GROUP
    assert s % grp == 0
    ng = s // grp
    c_in, c_out, c_shapes, c_args = _cast_job_specs(
        cast_jobs, b * ATT_HEADS * ng, lambda i, h, t: (i * ATT_HEADS + h) * ng + t)
    q_col = (width - 3 * ATT_DIM) // HEAD_DIM

    def col(slab):
        return pl.BlockSpec((1, grp, HEAD_DIM), lambda i, h, t: (i, t, q_col + slab * ATT_HEADS + h))

    tab = pl.BlockSpec((grp, HEAD_DIM), lambda i, h, t: (t, 0))
    vec = pl.BlockSpec((None, 1, HEAD_DIM), lambda i, h, t: (wl, 0, 0))
    scr = lambda rows: pltpu.VMEM((rows, HEAD_DIM), F32)
    outs = pl.pallas_call(
        functools.partial(_attn_body, n_cast=len(c_args)),
        grid=(b, ATT_HEADS, ng),
        in_specs=[col(0), col(1), col(2), tab, tab, vec, vec] + c_in,
        out_specs=[pl.BlockSpec((1, grp, HEAD_DIM), lambda i, h, t: (i, t, h))] + c_out,
        out_shape=[jax.ShapeDtypeStruct((b, s, ATT_DIM), BF16)] + c_shapes,
        scratch_shapes=[scr(grp), scr(2 * grp), scr(2 * grp)] * 2 + [scr(grp)] * 6,
        compiler_params=pltpu.CompilerParams(
            dimension_semantics=("parallel", "parallel", "arbitrary"), vmem_limit_bytes=VMEM_LIMIT),
        name="dilated_attn",
    )(u3, u3, u3, cos, sin, qn_g, kn_g, *c_args)
    return outs[0], list(outs[1:])


def _hgrn_body(q_ref, kk_ref, logf_ref, v_ref, gate_ref, gn_ref, o_ref, st_ref, b_ref, cum_ref, mask_ref,
               sgn_ref, *, rows, chunk):
    sub = chunk // 2
    levels = [sub >> (i + 1) for i in range(sub.bit_length() - 1)]
    small = [h for h in levels if 2 * h < SUBLANES]

    @pl.when(pl.program_id(2) == 0)
    def _():
        st_ref[...] = jnp.zeros_like(st_ref)

    ri = lax.broadcasted_iota(jnp.int32, (chunk, chunk), 0)
    ci = lax.broadcasted_iota(jnp.int32, (chunk, chunk), 1)
    cum_ref[...] = (ci <= ri).astype(BF16)
    row = lax.broadcasted_iota(jnp.int32, (chunk, HEAD_DIM), 0)
    rs_ = lax.broadcasted_iota(jnp.int32, (sub, sub), 0)
    cs_ = lax.broadcasted_iota(jnp.int32, (sub, sub), 1)
    for n, half in enumerate(levels):
        blk = 2 * half
        keep = ((rs_ // blk) == (cs_ // blk)) & ((rs_ & (blk - 1)) >= half) & ((cs_ & (blk - 1)) < half)
        mask_ref[n] = keep.astype(F32)
        sgn_ref[n] = jnp.where((row & (blk - 1)) >= half, 1.0, -1.0)

    heads = range(HGRN_HEADS_PER_STEP)

    def one_chunk(c, carry):
        rs = pl.ds(pl.multiple_of(c * chunk, chunk), chunk)
        hsl = [slice(hh * HEAD_DIM, (hh + 1) * HEAD_DIM) for hh in heads]
        q = [q_ref[0, rs, hs].astype(F32) for hs in hsl]
        kk = [kk_ref[0, rs, hs].astype(F32) for hs in hsl]
        v = [v_ref[0, rs, hs] for hs in hsl]
        bcum = []
        for hs in hsl:
            logf = logf_ref[0, rs, hs]
            hi = logf.astype(BF16)
            rem = logf - hi.astype(F32)
            mid = rem.astype(BF16)
            lo = (rem - mid.astype(F32)).astype(BF16)
            parts = jnp.dot(cum_ref[...], jnp.concatenate([hi, mid, lo], axis=1), preferred_element_type=F32)
            bcum.append((parts[:, :HEAD_DIM] + parts[:, HEAD_DIM:2 * HEAD_DIM] + parts[:, 2 * HEAD_DIM:]) * LOG2E)
        for hh in heads:
            b_ref[hh] = bcum[hh]

        att = [[jnp.zeros((sub, sub), F32), jnp.zeros((sub, sub), F32)] for _ in heads]
        for n, half in enumerate(levels):
            blk = 2 * half
            for hh in heads:
                if half in small:
                    pos = row & (blk - 1)
                    b_mid = bcum[hh]
                    for p in range(blk):
                        off = half - 1 - p
                        if off != 0:
                            b_mid = jnp.where(pos == p, pltpu.roll(bcum[hh], (-off) % chunk, 0), b_mid)
                else:
                    pieces = [jnp.broadcast_to(b_ref[hh, s0 + half - 1:s0 + half, :], (blk, HEAD_DIM))
                              for s0 in range(0, chunk, blk)]
                    b_mid = jnp.concatenate(pieces, axis=0)
                dec = jnp.exp2((bcum[hh] - b_mid) * sgn_ref[n])
                gq = (q[hh] * dec).astype(BF16)
                hk = (kk[hh] * dec).astype(BF16)
                for d in range(2):
                    blk_rows = slice(d * sub, (d + 1) * sub)
                    att[hh][d] = att[hh][d] + _dot_nt(gq[blk_rows], hk[blk_rows]) * mask_ref[n]
        cross = []
        for hh in heads:
            b_top = b_ref[hh, sub - 1:sub, :]
            cross.append(_dot_nt((q[hh][sub:] * jnp.exp2(bcum[hh][sub:] - b_top)).astype(BF16),
                                 (kk[hh][:sub] * jnp.exp2(b_top - bcum[hh][:sub])).astype(BF16)))

        outs = []
        for hh in heads:
            st = st_ref[hh]
            o_lo = jnp.dot(att[hh][0].astype(BF16), v[hh][:sub], preferred_element_type=F32)
            o_hi = (jnp.dot(att[hh][1].astype(BF16), v[hh][sub:], preferred_element_type=F32)
                    + jnp.dot(cross[hh].astype(BF16), v[hh][:sub], preferred_element_type=F32))
            o = jnp.concatenate([o_lo, o_hi], axis=0)
            vf = v[hh].astype(F32)
            o = o + jnp.sum(q[hh] * kk[hh], axis=-1, keepdims=True) * vf
            o = o + _dot_nt((q[hh] * jnp.exp2(bcum[hh])).astype(BF16), st.astype(BF16))
            b_last = b_ref[hh, chunk - 1:chunk, :]
            k_dec = (kk[hh] * jnp.exp2(b_last - bcum[hh])).astype(BF16)
            st_ref[hh] = st * jnp.exp2(b_last) + jnp.dot(vf.T.astype(BF16), k_dec, preferred_element_type=F32)
            outs.append(o)
        for hh, hs in zip(heads, hsl):
            on = _rms_scale(outs[hh]) * gn_ref[:, hs] * gate_ref[0, rs, hs].astype(F32)
            o_ref[0, rs, hs] = on.astype(o_ref.dtype)
        return carry

    lax.fori_loop(0, rows // chunk, one_chunk, 0)


def _hgrn(q3, kk3, logf3, v3, gate3, gn_g, wl, *, rows=2048, chunk=256):
    b, s, _ = logf3.shape
    rows = min(rows, s)
    hp = HGRN_HEADS_PER_STEP
    wide = hp * HEAD_DIM
    sub = chunk // 2
    n_levels = sub.bit_length() - 1
    blk = pl.BlockSpec((1, rows, wide), lambda i, h, r: (i, r, h))
    return pl.pallas_call(
        functools.partial(_hgrn_body, rows=rows, chunk=chunk),
        grid=(b, HGRN_WIDTH // wide, s // rows),
        in_specs=[blk, blk, blk, blk, blk,
                  pl.BlockSpec((None, 1, wide), lambda i, h, r: (wl, 0, h))],
        out_specs=blk,
        out_shape=jax.ShapeDtypeStruct((b, s, HGRN_WIDTH), BF16),
        scratch_shapes=[pltpu.VMEM((hp, HEAD_DIM, HEAD_DIM), F32),
                        pltpu.VMEM((hp, chunk, HEAD_DIM), F32),
                        pltpu.VMEM((chunk, chunk), BF16),
                        pltpu.VMEM((n_levels, sub, sub), F32),
                        pltpu.VMEM((n_levels, chunk, HEAD_DIM), F32)],
        compiler_params=pltpu.CompilerParams(
            dimension_semantics=("parallel", "parallel", "arbitrary"), vmem_limit_bytes=VMEM_LIMIT),
        name="hgrn2",
    )(q3, kk3, logf3, v3, gate3, gn_g)


def _rope_tables(s):
    half = HEAD_DIM // 2
    inv = jnp.exp(-math.log(ROPE_THETA) * jnp.arange(half, dtype=F32) / half)
    ang = jnp.arange(s, dtype=jnp.int32).astype(F32)[:, None] * inv[None, :]
    cos, sin = jnp.cos(ang), jnp.sin(ang)
    return jnp.concatenate([cos, cos], axis=-1), jnp.concatenate([-sin, sin], axis=-1)


def _conv_attn_mixer(x2, b, s, norm_mix, layer, j, w_in, conv_w, conv_b, cn_g, cn_b, qn_g, kn_g, w_out, cos, sin,
                     conv_casts=(), attn_casts=()):
    m = b * s
    n_in = w_in.shape[2]
    u = _norm_matmul(x2, norm_mix, w_in, layer, j, BF16)
    u3 = u.reshape(b, s, n_in)
    d = x2.shape[1]
    own = [(w_out.reshape(-1, d), j * w_out.shape[1], w_out.shape[1])]
    a, conv_cast = _conv_module(u3, conv_w, conv_b, cn_g, cn_b, j, list(conv_casts) + own)
    o, attn_cast = _fused_attention(u3, cos, sin, qn_g, kn_g, j, attn_casts)
    w_out_b = conv_cast[-1].reshape(1, w_out.shape[1], d)
    x2 = _outproj(x2, [a.reshape(m, CONV_CH), o.reshape(m, ATT_DIM)], w_out_b, 0, tm=512)
    return x2, conv_cast[:-1], attn_cast


def _hgrn2_mixer(x2, b, s, norm_mix, layer, j, w_in, lb, gn_g, w_out):
    parts = _hgrn_inproj(x2, norm_mix, w_in, lb, layer, j)
    og = _hgrn(*[p.reshape(b, s, HGRN_WIDTH) for p in parts], gn_g, j)
    return _outproj(x2, [og.reshape(b * s, HGRN_WIDTH)], w_out, j)


def kernel(x, norm_ffn1, ffn1_wg, ffn1_wu, ffn1_wd, norm_mix, norm_ffn2, ffn2_wg, ffn2_wu, ffn2_wd, ev_w_in, ev_conv_w, ev_conv_b, ev_cn_g, ev_cn_b, ev_qn_g, ev_kn_g, ev_w_out, od_w_in, od_lb_logits, od_gn_g, od_w_out):
    b, s, d = x.shape
    depth = norm_ffn1.shape[0]
    cos, sin = _rope_tables(s)
    p = jax.nn.softmax(od_lb_logits.astype(F32), axis=0)
    lower_bounds = jnp.cumsum(p, axis=0) - p[0:1]
    row3 = lambda a: a[:, None, :]
    rows2 = lambda w: w.reshape(-1, w.shape[-1])
    g1, gm, g2 = row3(norm_ffn1), row3(norm_mix), row3(norm_ffn2)
    f = ffn1_wg.shape[2]
    gu1 = {0: (ffn1_wg[0:1].astype(BF16), ffn1_wu[0:1].astype(BF16), 0)}
    later = (depth - 1) * d
    conv_casts = [(rows2(ffn1_wg), d, later), (rows2(ffn1_wu), d, later),
                  (rows2(od_w_in), 0, od_w_in.shape[0] * d)] if depth > 1 else []
    attn_casts = [(rows2(ffn2_wg), 0, depth * d), (rows2(ffn2_wu), 0, depth * d)]
    if depth > 1:
        attn_casts.append((rows2(od_w_out), 0, od_w_out.shape[0] * od_w_out.shape[1]))
    gu2 = od_in = od_out = None
    x2 = x.reshape(b * s, d)
    for l in range(depth):
        j = l // 2
        wg1, wu1, idx1 = gu1[l]
        x2 = _ffn(x2, g1, wg1, wu1, ffn1_wd, l, idx1)
        if l % 2 == 0:
            first = l == 0
            x2, conv_cast, attn_cast = _conv_attn_mixer(
                x2, b, s, gm, l, j, ev_w_in.astype(BF16), ev_conv_w, row3(ev_conv_b), row3(ev_cn_g), row3(ev_cn_b),
                row3(ev_qn_g), row3(ev_kn_g), ev_w_out, cos, sin,
                conv_casts if first else (), attn_casts if first else ())
            if first:
                gu2 = [w.reshape(depth, d, f) for w in attn_cast[:2]]
                if depth > 1:
                    od_out = attn_cast[2].reshape(od_w_out.shape)
                if conv_cast:
                    wg_l, wu_l, od_in = conv_cast
                    for ll in range(1, depth):
                        gu1[ll] = (wg_l.reshape(depth - 1, d, f), wu_l.reshape(depth - 1, d, f), ll - 1)
                    od_in = od_in.reshape(od_w_in.shape)
        else:
            x2 = _hgrn2_mixer(x2, b, s, gm, l, j, od_in, lower_bounds[l][None], row3(od_gn_g), od_out)
        x2 = _ffn(x2, g2, gu2[0], gu2[1], ffn2_wd, l, l)
    return x2.reshape(b, s, d)
```
